```python
import math
import jax, jax.numpy as jnp
from jax import lax
import numpy as np


D_MODEL = 1024
BATCH = 2
SEQ = 8192
DEPTH = 4
DEC_BATCH = 128
DEC_SEQ = 8
PAST_LEN = 8192
PAGE_SIZE = 128

MIX_WIDTH = D_MODEL
A_WIDTH = MIX_WIDTH // 4
A_HEADS = 4
A_HEAD_DIM = A_WIDTH // A_HEADS
A_CHUNK = 128
B_WIDTH = MIX_WIDTH // 4
B_CONV = 31
C_HEADS = 4
C_KV_HEADS = 2
C_GROUP = C_HEADS // C_KV_HEADS
C_HEAD_DIM = MIX_WIDTH // 16
C_WIDTH = C_HEADS * C_HEAD_DIM
C_KV_WIDTH = C_KV_HEADS * C_HEAD_DIM
WINDOW = 128
C_BLOCK = WINDOW
D_WIDTH = MIX_WIDTH - A_WIDTH - B_WIDTH - C_WIDTH
D_HEAD_DIM = 64
D_HEADS = D_WIDTH // D_HEAD_DIM
D_GROUPS = 2
D_STATE = 128
D_CONV = 4
D_CHUNK = 128
D_CONV_DIM = D_WIDTH + 2 * D_GROUPS * D_STATE
IN_DIM = 3 * A_WIDTH + 3 * B_WIDTH + (2 * C_WIDTH + 2 * C_KV_WIDTH) + (D_WIDTH + D_CONV_DIM + D_HEADS)
EPS = 1e-6

kernel_name = 'hybrid_parallel_groups_decode_step'


def rms_norm(x, g):
    xf = x.astype(jnp.float32)
    y = xf * lax.rsqrt(jnp.mean(xf * xf, axis=-1, keepdims=True) + EPS)
    return (y * g.astype(jnp.float32)).astype(x.dtype)


def layer_norm(x, g, b):
    xf = x.astype(jnp.float32)
    mu = jnp.mean(xf, axis=-1, keepdims=True)
    xc = xf - mu
    y = xc * lax.rsqrt(jnp.mean(xc * xc, axis=-1, keepdims=True) + EPS)
    return (y * g.astype(jnp.float32) + b.astype(jnp.float32)).astype(x.dtype)


def causal_dwconv(x_hist, w, b):
    c = x_hist.shape[-1]
    y = lax.conv_general_dilated(x_hist.astype(w.dtype), w[:, None, :], window_strides=(1,),
                                 padding='VALID', dimension_numbers=('NWC', 'WIO', 'NWC'),
                                 feature_group_count=c)
    return y + b


def alibi_slopes(n):
    return jnp.asarray(2.0 ** (-8.0 * np.arange(1, n + 1) / n), dtype=jnp.float32)


def split_in(proj):
    sizes = [A_WIDTH] * 3 + [B_WIDTH] * 3 + [C_WIDTH, C_KV_WIDTH, C_KV_WIDTH, C_WIDTH] + [D_WIDTH, D_CONV_DIM, D_HEADS]
    offs = [int(o) for o in np.cumsum(sizes)[:-1]]
    return jnp.split(proj, offs, axis=-1)


def mixer_a(u_raw, v_raw, ln_g, ln_b, ws, wb):
    n, l, _ = u_raw.shape
    q = min(l, A_CHUNK)
    u = jax.nn.gelu(u_raw)
    v = layer_norm(jax.nn.gelu(v_raw), ln_g, ln_b)
    vc = v.reshape(n, l // q, q, A_HEADS, A_HEAD_DIM)
    tri = jnp.tril(jnp.ones((q, q), dtype=bool))
    wm = jnp.where(tri, ws[:, :q, :q], 0.0)
    s = jnp.einsum('hts,bcshd->bcthd', wm, vc) + wb[:, :q].T[:, :, None]
    return u * s.reshape(n, l, A_WIDTH), v


def mixer_b(bv, bg, hist, conv_w, conv_b, ln_g, ln_b, pw_w, pw_b, prompt):
    glu = bv * jax.nn.sigmoid(bg)
    c = causal_dwconv(jnp.concatenate([hist, glu], axis=1), conv_w, conv_b)
    y = jax.nn.silu(layer_norm(c, ln_g, ln_b)) @ pw_w + pw_b
    new_rows = glu[:, -(B_CONV - 1):] if prompt else glu
    return y, new_rows


def sink_attention(qb, kb, vb, rel, valid, sinks):
    slopes = alibi_slopes(C_HEADS).reshape(C_KV_HEADS, C_GROUP)[:, :, None, None]
    s = jnp.einsum('bnqkgd,bnskd->bnkgqs', qb, kb).astype(jnp.float32)
    s = s - slopes * rel.astype(jnp.float32)
    s = jnp.where(valid[:, None, None], s, -jnp.inf)
    sink = sinks.astype(jnp.float32).reshape(C_KV_HEADS, C_GROUP)[:, :, None, None]
    m = jnp.maximum(jnp.max(s, axis=-1, keepdims=True), sink)
    p = jnp.exp(s - m)
    denom = jnp.sum(p, axis=-1, keepdims=True) + jnp.exp(sink - m)
    return jnp.einsum('bnkgqs,bnskd->bnqkgd', (p / denom).astype(vb.dtype), vb)


def mixer_c(q, k, v, k_past, v_past, sinks, prompt):
    n, l, _ = q.shape
    qh = q.reshape(n, l, C_KV_HEADS, C_GROUP, C_HEAD_DIM) * (C_HEAD_DIM ** -0.5)
    kh = k.reshape(n, l, C_KV_HEADS, C_HEAD_DIM)
    vh = v.reshape(n, l, C_KV_HEADS, C_HEAD_DIM)
    if prompt:
        nb = l // C_BLOCK
        pad = jnp.zeros((n, C_BLOCK, C_KV_HEADS, C_HEAD_DIM), kh.dtype)
        kp = jnp.concatenate([pad, kh], axis=1).reshape(n, nb + 1, C_BLOCK, C_KV_HEADS, C_HEAD_DIM)
        vp = jnp.concatenate([pad, vh], axis=1).reshape(n, nb + 1, C_BLOCK, C_KV_HEADS, C_HEAD_DIM)
        kb = jnp.concatenate([kp[:, :-1], kp[:, 1:]], axis=2)
        vb = jnp.concatenate([vp[:, :-1], vp[:, 1:]], axis=2)
        qb = qh.reshape(n, nb, C_BLOCK, C_KV_HEADS, C_GROUP, C_HEAD_DIM)
        si = jnp.arange(2 * C_BLOCK)
        rel = jnp.arange(C_BLOCK)[:, None] + C_BLOCK - si[None, :]
        key_pos = (jnp.arange(nb) * C_BLOCK - C_BLOCK)[:, None, None] + si[None, None, :]
        valid = (rel >= 0) & (rel <= WINDOW) & (key_pos >= 0)
        keep = min(WINDOW, l)
        k_new, v_new = kh[:, l - keep:], vh[:, l - keep:]
    else:
        past = k_past.shape[1]
        kb = jnp.concatenate([k_past, kh], axis=1)[:, None]
        vb = jnp.concatenate([v_past, vh], axis=1)[:, None]
        qb = qh[:, None]
        rel = jnp.arange(l)[:, None] + past - jnp.arange(past + l)[None, :]
        valid = ((rel >= 0) & (rel <= WINDOW))[None]
        k_new, v_new = kh, vh
    o = sink_attention(qb, kb, vb, rel, valid, sinks)
    return o.reshape(n, l, C_WIDTH), k_new, v_new


def ssd(x, dt, a, bm, cm, h0, chunk):
    n, l, h, p = x.shape
    nc = l // chunk
    f32 = jnp.float32
    xc = x.astype(f32).reshape(n, nc, chunk, h, p)
    dtc = dt.astype(f32).reshape(n, nc, chunk, h)
    bc = bm.astype(f32).reshape(n, nc, chunk, h, -1)
    cc = cm.astype(f32).reshape(n, nc, chunk, h, -1)
    acum = jnp.cumsum(dtc * a, axis=2)
    seg = acum[:, :, :, None, :] - acum[:, :, None, :, :]
    tri = jnp.tril(jnp.ones((chunk, chunk), dtype=bool))[:, :, None]
    decay = jnp.exp(jnp.where(tri, seg, -jnp.inf))
    scores = jnp.einsum('bcthn,bcshn->bctsh', cc, bc) * decay
    y_intra = jnp.einsum('bctsh,bcsh,bcshp->bcthp', scores, dtc, xc)
    decay_end = jnp.exp(acum[:, :, -1:, :] - acum)
    states = jnp.einsum('bcsh,bcsh,bcshn,bcshp->bchpn', decay_end, dtc, bc, xc)
    chunk_decay = jnp.exp(acum[:, :, -1, :])

    def step(h_prev, inp):
        st, cd = inp
        return h_prev * cd[:, :, None, None] + st, h_prev

    h_final, h_in = lax.scan(step, h0.astype(f32),
                             (jnp.moveaxis(states, 1, 0), jnp.moveaxis(chunk_decay, 1, 0)))
    h_in = jnp.moveaxis(h_in, 0, 1)
    y_inter = jnp.einsum('bcthn,bchpn->bcthp', cc, h_in) * jnp.exp(acum)[..., None]
    y = (y_intra + y_inter).reshape(n, l, h, p).astype(x.dtype)
    return y, h_final.astype(h0.dtype)


def mixer_d(z, xbc_raw, dt_raw, hist, h0, conv_w, conv_b, dt_bias, a_log, d_skip, norm_g, prompt):
    n, l, _ = z.shape
    xbc = jax.nn.silu(causal_dwconv(jnp.concatenate([hist, xbc_raw], axis=1), conv_w, conv_b))
    xs, bm, cm = jnp.split(xbc, [D_WIDTH, D_WIDTH + D_GROUPS * D_STATE], axis=-1)
    xs = xs.reshape(n, l, D_HEADS, D_HEAD_DIM)
    rep = D_HEADS // D_GROUPS
    bm = jnp.repeat(bm.reshape(n, l, D_GROUPS, D_STATE), rep, axis=2)
    cm = jnp.repeat(cm.reshape(n, l, D_GROUPS, D_STATE), rep, axis=2)
    dt = jax.nn.softplus(dt_raw.astype(jnp.float32) + dt_bias.astype(jnp.float32))
    a = -jnp.exp(a_log.astype(jnp.float32))
    y, h_new = ssd(xs, dt, a, bm, cm, h0, min(l, D_CHUNK))
    y = (y + d_skip[:, None] * xs).reshape(n, l, D_WIDTH) * jax.nn.silu(z)
    y = rms_norm(y.reshape(n, l, D_GROUPS, D_WIDTH // D_GROUPS),
                 norm_g.reshape(D_GROUPS, D_WIDTH // D_GROUPS)).reshape(n, l, D_WIDTH)
    new_rows = xbc_raw[:, -(D_CONV - 1):] if prompt else xbc_raw
    return y, new_rows, h_new


def layer(x, past, w, prompt):
    (g_pre, g_post, w_in, w_out, a_ln_g, a_ln_b, a_ws, a_wb, b_conv_w, b_conv_b, b_ln_g, b_ln_b,
     b_pw_w, b_pw_b, c_sinks, d_conv_w, d_conv_b, d_dt_bias, d_a_log, d_skip, d_norm_g) = w
    n, l, _ = x.shape
    h = rms_norm(x, g_pre)
    (a_u, a_v, a_z, b_v, b_g, b_z, c_q, c_k, c_v, c_z, d_z, d_xbc, d_dt) = split_in(h @ w_in)
    if prompt:
        k_past = v_past = None
        cb_past = jnp.zeros((n, B_CONV - 1, B_WIDTH), x.dtype)
        cd_past = jnp.zeros((n, D_CONV - 1, D_CONV_DIM), x.dtype)
        ssm_past = jnp.zeros((n, D_HEADS, D_HEAD_DIM, D_STATE), x.dtype)
    else:
        k_past, v_past, cb_past, cd_past, ssm_past = past
    ya, v_rows = mixer_a(a_u, a_v, a_ln_g, a_ln_b, a_ws, a_wb)
    yb, cb_new = mixer_b(b_v, b_g, cb_past, b_conv_w, b_conv_b, b_ln_g, b_ln_b, b_pw_w, b_pw_b, prompt)
    yc, k_new, v_new = mixer_c(c_q, c_k, c_v, k_past, v_past, c_sinks, prompt)
    yd, cd_new, ssm_new = mixer_d(d_z, d_xbc, d_dt, cd_past, ssm_past, d_conv_w, d_conv_b,
                                  d_dt_bias, d_a_log, d_skip, d_norm_g, prompt)
    mixed = jnp.concatenate([ya * jax.nn.silu(a_z), yb * jax.nn.silu(b_z),
                             yc * jax.nn.silu(c_z), yd], axis=-1) @ w_out
    return x + rms_norm(mixed, g_post), (k_new, v_new, cb_new, cd_new, ssm_new, v_rows)


def setup_inputs(seed: int = 0) -> dict:
    key = jax.random.key(seed)
    ks = jax.random.split(key, 32)

    def nrm(k, shape, scale):
        return jax.random.normal(k, shape, jnp.float32) * scale

    win_buf = min(WINDOW, PAST_LEN)
    dt0 = jnp.exp(jax.random.uniform(ks[24], (DEPTH, D_HEADS), jnp.float32, math.log(1e-3), math.log(1e-1)))
    return {
        'x_prompt': nrm(ks[0], (BATCH, SEQ, D_MODEL), 1.0),
        'x_sample': nrm(ks[1], (DEC_BATCH, DEC_SEQ, D_MODEL), 1.0),
        'cache_win_k': nrm(ks[2], (DEPTH, DEC_BATCH, win_buf, C_KV_HEADS, C_HEAD_DIM), 1.0),
        'cache_win_v': nrm(ks[3], (DEPTH, DEC_BATCH, win_buf, C_KV_HEADS, C_HEAD_DIM), 1.0),
        'state_conv_b': nrm(ks[4], (DEPTH, DEC_BATCH, B_CONV - 1, B_WIDTH), 0.5),
        'state_conv_d': nrm(ks[5], (DEPTH, DEC_BATCH, D_CONV - 1, D_CONV_DIM), 1.0),
        'state_ssm': nrm(ks[6], (DEPTH, DEC_BATCH, D_HEADS, D_HEAD_DIM, D_STATE), 0.5),
        'norm_pre': 1.0 + nrm(ks[7], (DEPTH, D_MODEL), 0.02),
        'norm_post': 1.0 + nrm(ks[8], (DEPTH, D_MODEL), 0.02),
        'w_in': nrm(ks[9], (DEPTH, D_MODEL, IN_DIM), D_MODEL ** -0.5),
        'w_out': nrm(ks[10], (DEPTH, MIX_WIDTH, D_MODEL), MIX_WIDTH ** -0.5),
        'a_ln_g': 1.0 + nrm(ks[11], (DEPTH, A_WIDTH), 0.02),
        'a_ln_b': nrm(ks[12], (DEPTH, A_WIDTH), 0.02),
        'a_ws': nrm(ks[13], (DEPTH, A_HEADS, A_CHUNK, A_CHUNK), A_CHUNK ** -0.5),
        'a_wb': 1.0 + nrm(ks[14], (DEPTH, A_HEADS, A_CHUNK), 0.1),
        'b_conv_w': nrm(ks[15], (DEPTH, B_CONV, B_WIDTH), B_CONV ** -0.5),
        'b_conv_b': nrm(ks[16], (DEPTH, B_WIDTH), 0.02),
        'b_ln_g': 1.0 + nrm(ks[17], (DEPTH, B_WIDTH), 0.02),
        'b_ln_b': nrm(ks[18], (DEPTH, B_WIDTH), 0.02),
        'b_pw_w': nrm(ks[19], (DEPTH, B_WIDTH, B_WIDTH), B_WIDTH ** -0.5),
        'b_pw_b': nrm(ks[20], (DEPTH, B_WIDTH), 0.02),
        'c_sinks': nrm(ks[21], (DEPTH, C_HEADS), 1.0),
        'd_conv_w': nrm(ks[22], (DEPTH, D_CONV, D_CONV_DIM), D_CONV ** -0.5),
        'd_conv_b': nrm(ks[23], (DEPTH, D_CONV_DIM), 0.02),
        'd_dt_bias': dt0 + jnp.log(-jnp.expm1(-dt0)),
        'd_a_log': jnp.log(jax.random.uniform(ks[25], (DEPTH, D_HEADS), jnp.float32, 1.0, 16.0)),
        'd_skip': 1.0 + nrm(ks[26], (DEPTH, D_HEADS), 0.1),
        'd_norm_g': 1.0 + nrm(ks[27], (DEPTH, D_WIDTH), 0.02),
    }


def reference(x_prompt, x_sample, cache_win_k, cache_win_v, state_conv_b, state_conv_d, state_ssm,
              norm_pre, norm_post, w_in, w_out, a_ln_g, a_ln_b, a_ws, a_wb, b_conv_w, b_conv_b,
              b_ln_g, b_ln_b, b_pw_w, b_pw_b, c_sinks, d_conv_w, d_conv_b, d_dt_bias, d_a_log,
              d_skip, d_norm_g):
    xp, xs = x_prompt, x_sample
    p_k, p_v, p_cb, p_cd, p_ssm = [], [], [], [], []
    s_k, s_v, s_cb, s_cd, s_ssm, s_av = [], [], [], [], [], []
    for i in range(DEPTH):
        w = (norm_pre[i], norm_post[i], w_in[i], w_out[i], a_ln_g[i], a_ln_b[i], a_ws[i], a_wb[i],
             b_conv_w[i], b_conv_b[i], b_ln_g[i], b_ln_b[i], b_pw_w[i], b_pw_b[i], c_sinks[i],
             d_conv_w[i], d_conv_b[i], d_dt_bias[i], d_a_log[i], d_skip[i], d_norm_g[i])
        xp, (k1, v1, cb1, cd1, ssm1, _) = layer(xp, None, w, True)
        past = (cache_win_k[i], cache_win_v[i], state_conv_b[i], state_conv_d[i], state_ssm[i])
        xs, (k2, v2, cb2, cd2, ssm2, av2) = layer(xs, past, w, False)
        p_k.append(k1); p_v.append(v1); p_cb.append(cb1); p_cd.append(cd1); p_ssm.append(ssm1)
        s_k.append(k2); s_v.append(v2); s_cb.append(cb2); s_cd.append(cd2); s_ssm.append(ssm2); s_av.append(av2)
    return (xp, xs,
            jnp.stack(p_k), jnp.stack(s_k), jnp.stack(p_v), jnp.stack(s_v),
            jnp.stack(p_cb), jnp.stack(s_cb), jnp.stack(p_cd), jnp.stack(s_cd),
            jnp.stack(p_ssm), jnp.stack(s_ssm), jnp.stack(s_av))
```

```python
import functools

import numpy as np
import jax
import jax.numpy as jnp
from jax import lax
from jax.experimental import pallas as pl
from jax.experimental.pallas import tpu as pltpu

F32 = jnp.float32
BF16 = jnp.bfloat16
EPS = 1e-6

D_MODEL = 1024
LANES = 128
CH = 128
HEAD = 64
N_HEADS = 4
SEG = 256
KV_W = 128
B_TAPS = 31
D_TAPS = 4
D_CONV_DIM = 768
D_STATE = 128
DT_PAD = 128
SAMPLE_SEQ = 8
ATT_SUB = 8

O_AU, O_AV, O_AZ = 0, 256, 512
O_BV, O_BG, O_BZ = 768, 1024, 1280
O_CQ, O_CK, O_CV, O_CZ = 1536, 1792, 1920, 2048
O_DZ, O_DX, O_DT = 2304, 2560, 3328
IN_RAW = 3332
IN_PAD = 3456
PROJ_TILE = 1152

ALIBI = tuple(float(2.0 ** (-8.0 * (i + 1) / N_HEADS)) for i in range(N_HEADS))
VMEM_LIMIT_BYTES = 56 * 1024 * 1024


def _iota(shape, dim):
    return lax.broadcasted_iota(jnp.int32, shape, dim)


def _dot(a, b):
    return jnp.dot(a, b, preferred_element_type=F32)


def _dot_nt(a, b):
    return lax.dot_general(a, b, (((1,), (1,)), ((), ())), preferred_element_type=F32)


def _rms(x, g):
    return x * lax.rsqrt(jnp.mean(x * x, axis=-1, keepdims=True) + EPS) * g


def _ln(x, g, b):
    mu = jnp.mean(x, axis=-1, keepdims=True)
    xc = x - mu
    return xc * lax.rsqrt(jnp.mean(xc * xc, axis=-1, keepdims=True) + EPS) * g + b


def _silu(x):
    return x * jax.nn.sigmoid(x)


def _softplus(x):
    return jnp.maximum(x, 0.0) + jnp.log1p(jnp.exp(-jnp.abs(x)))


def _hi_lo(x):
    hi = x.astype(BF16)
    return hi, (x - hi.astype(F32)).astype(BF16)


def _head_of_lane(width):
    return _iota((CH, width), 1) >> 6


def _lane_bcast_heads(cols, width=SEG):
    head = _head_of_lane(width)
    out = jnp.zeros((CH, width), F32)
    for h in range(N_HEADS):
        out = jnp.where(head == h, cols[:, h:h + 1], out)
    return out


def _chunk_masks(seq_rows):
    shift = seq_rows.bit_length() - 1
    r = _iota((CH, CH), 0)
    c = _iota((CH, CH), 1)
    same = (r >> shift) == (c >> shift)
    causal = same & (c <= r)
    return same, causal


def _project(x, gpre, win_ref, proj_s):
    h = _rms(x, gpre).astype(BF16)
    for n0 in range(0, IN_PAD, PROJ_TILE):
        proj_s[:, n0:n0 + PROJ_TILE] = _dot(h, win_ref[0, :, n0:n0 + PROJ_TILE])


def _mixer_a(u_raw, v_raw, z, ln_g, ln_b, ws_ref, wb, causal):
    u = jax.nn.gelu(u_raw)
    v = _ln(jax.nn.gelu(v_raw), ln_g, ln_b)
    vb = v.astype(BF16)
    head = _head_of_lane(SEG)
    s = jnp.zeros((CH, SEG), F32)
    for h in range(N_HEADS):
        wm = jnp.where(causal, ws_ref[0, h], 0.0).astype(BF16)
        s = jnp.where(head == h, _dot(wm, vb), s)
    return u * (s + wb) * _silu(z), v


def _mixer_b_post(c, z, ln_g, ln_b, pw_ref, pw_b):
    y = _dot(_silu(_ln(c, ln_g, ln_b)).astype(BF16), pw_ref[0]) + pw_b
    return y * _silu(z)


def _attn(q_rows, k_a, v_a, k_b, v_b, rel_a, ok_a, rel_b, ok_b, slope, sink):
    neg = -jnp.inf
    s_a = jnp.where(ok_a, _dot_nt(q_rows, k_a) - slope * rel_a.astype(F32), neg)
    s_b = jnp.where(ok_b, _dot_nt(q_rows, k_b) - slope * rel_b.astype(F32), neg)
    m = jnp.maximum(jnp.maximum(jnp.max(s_a, axis=-1, keepdims=True),
                                jnp.max(s_b, axis=-1, keepdims=True)), sink)
    p_a = jnp.exp(s_a - m)
    p_b = jnp.exp(s_b - m)
    den = (jnp.sum(p_a, axis=-1, keepdims=True) + jnp.sum(p_b, axis=-1, keepdims=True)
           + jnp.exp(sink - m))
    o = _dot(p_a.astype(BF16), v_a) + _dot(p_b.astype(BF16), v_b)
    return o / den


def _attn_q_rows(q, kvh):
    lane = _iota((q.shape[0], KV_W), 1)
    keep = (lane < HEAD) if kvh == 0 else (lane >= HEAD)
    t0 = jnp.where(keep, q[:, 0:KV_W], 0.0)
    t1 = jnp.where(keep, q[:, KV_W:2 * KV_W], 0.0)
    return jnp.concatenate([t0, t1], axis=0).astype(BF16)


def _attn_merge(o0, o1):
    half = o0.shape[0] // 2
    lo = _iota((half, KV_W), 1) < HEAD
    return jnp.concatenate([jnp.where(lo, o0[:half], o1[:half]),
                            jnp.where(lo, o0[half:], o1[half:])], axis=1)


def _head_cols(sink_ref, layer, kvh, rows):
    half = rows // 2
    first = _iota((rows, 1), 0) < half
    slope = jnp.where(first, ALIBI[2 * kvh], ALIBI[2 * kvh + 1])
    sink = jnp.where(first, sink_ref[layer, 2 * kvh], sink_ref[layer, 2 * kvh + 1])
    return slope, sink


def _ssd_intra(xbc, dt_raw, dtb, alog, same, causal):
    xs = xbc[:, 0:SEG]
    bm_b = xbc[:, SEG:2 * SEG].astype(BF16)
    cm_b = xbc[:, 2 * SEG:3 * SEG].astype(BF16)
    xs_b = xs.astype(BF16)
    dt = _softplus(dt_raw + dtb)
    da_hi, da_lo = _hi_lo(dt * (-jnp.exp(alog)))
    tri_b = jnp.where(causal, 1.0, 0.0).astype(BF16)
    same_b = jnp.where(same, 1.0, 0.0).astype(BF16)
    acum = _dot(tri_b, da_hi) + _dot(tri_b, da_lo)
    alast = _dot(same_b, da_hi) + _dot(same_b, da_lo)
    acum_t = acum.T
    dt_t = dt.T
    head = _head_of_lane(SEG)
    y = jnp.zeros((CH, SEG), F32)
    for g in range(2):
        sc = _dot_nt(cm_b[:, g * D_STATE:(g + 1) * D_STATE], bm_b[:, g * D_STATE:(g + 1) * D_STATE])
        for hh in range(2):
            h = 2 * g + hh
            seg = acum[:, h:h + 1] - acum_t[h:h + 1, :]
            decay = jnp.exp(jnp.where(causal, seg, -jnp.inf))
            m = (sc * decay * dt_t[h:h + 1, :]).astype(BF16)
            y = jnp.where(head == h, _dot(m, xs_b), y)
    return xs, xbc[:, SEG:2 * SEG], cm_b, dt, acum, alast, y


def _ssd_finish(y, xs, z, dskip, dng):
    y = (y + dskip * xs) * _silu(z)
    parts = []
    for g in range(2):
        yg = y[:, g * D_STATE:(g + 1) * D_STATE]
        parts.append(_rms(yg, dng[:, g * D_STATE:(g + 1) * D_STATE]))
    return jnp.concatenate(parts, axis=1)


def _finish_layer(x, mix_s, wout_ref, gpost):
    return x + _rms(_dot(mix_s[...], wout_ref[0]), gpost)


def _prompt_kernel(layer, rows,
                   sink_ref, x_ref, gpre, gpost, win, wout, alng, alnb, aws, awb, bcw, bcb, blng, blnb,
                   bpw, bpb, dcw, dcb, dtb, alog, dskip, dng,
                   y_ref, kn_ref, vn_ref, cbn_ref, cdn_ref, ssm_ref,
                   proj_s, glu_s, xbc_s, conv_s, kprev_s, vprev_s, h_s, mix_s):
    j = pl.program_id(1)
    n_chunks = rows // CH
    b_hist, d_hist = 32, 8

    @pl.when(j == 0)
    def _():
        glu_s[0:b_hist, :] = jnp.zeros((b_hist, SEG), F32)
        xbc_s[0:d_hist, :] = jnp.zeros((d_hist, D_CONV_DIM), F32)
        kprev_s[...] = jnp.zeros((CH, KV_W), BF16)
        vprev_s[...] = jnp.zeros((CH, KV_W), BF16)
        h_s[...] = jnp.zeros((N_HEADS * HEAD, D_STATE), F32)

    @pl.when(j > 0)
    def _():
        glu_s[0:b_hist, :] = glu_s[rows:rows + b_hist, :]
        xbc_s[0:d_hist, :] = xbc_s[rows:rows + d_hist, :]

    x = x_ref[0]
    _project(x, gpre[0], win, proj_s)

    glu_s[b_hist:b_hist + rows, :] = (proj_s[:, O_BV:O_BV + SEG]
                                      * jax.nn.sigmoid(proj_s[:, O_BG:O_BG + SEG]))
    xbc_s[d_hist:d_hist + rows, :] = proj_s[:, O_DX:O_DX + D_CONV_DIM]
    tile = 32
    for r0 in range(0, rows, tile):
        acc = glu_s[r0 + b_hist - (B_TAPS - 1):r0 + b_hist - (B_TAPS - 1) + tile, :] * bcw[0, 0:1, :]
        for k in range(1, B_TAPS):
            s0 = r0 + b_hist - (B_TAPS - 1) + k
            acc = acc + glu_s[s0:s0 + tile, :] * bcw[0, k:k + 1, :]
        conv_s[r0:r0 + tile, 0:SEG] = acc + bcb[0]
        acc = xbc_s[r0 + d_hist - (D_TAPS - 1):r0 + d_hist - (D_TAPS - 1) + tile, :] * dcw[0, 0:1, :]
        for k in range(1, D_TAPS):
            s0 = r0 + d_hist - (D_TAPS - 1) + k
            acc = acc + xbc_s[s0:s0 + tile, :] * dcw[0, k:k + 1, :]
        conv_s[r0:r0 + tile, SEG:SEG + D_CONV_DIM] = _silu(acc + dcb[0])

    same, causal = _chunk_masks(CH)
    row2 = _iota((2 * CH, CH), 0) & (CH - 1)
    col2 = _iota((2 * CH, CH), 1)
    k_prev = kprev_s[...]
    v_prev = vprev_s[...]
    for c in range(n_chunks):
        rs = slice(c * CH, (c + 1) * CH)

        ya, _ = _mixer_a(proj_s[rs, O_AU:O_AU + SEG], proj_s[rs, O_AV:O_AV + SEG], proj_s[rs, O_AZ:O_AZ + SEG],
                         alng[0], alnb[0], aws, awb[0], causal)
        mix_s[rs, 0:SEG] = ya.astype(BF16)

        yb = _mixer_b_post(conv_s[rs, 0:SEG], proj_s[rs, O_BZ:O_BZ + SEG], blng[0], blnb[0], bpw, bpb[0])
        mix_s[rs, SEG:2 * SEG] = yb.astype(BF16)

        q = proj_s[rs, O_CQ:O_CQ + SEG] * (HEAD ** -0.5)
        k_cur = proj_s[rs, O_CK:O_CK + KV_W].astype(BF16)
        v_cur = proj_s[rs, O_CV:O_CV + KV_W].astype(BF16)
        has_prev = (j > 0) if c == 0 else True
        outs = []
        for kvh in range(2):
            slope, sink = _head_cols(sink_ref, layer, kvh, 2 * CH)
            outs.append(_attn(_attn_q_rows(q, kvh), k_prev, v_prev, k_cur, v_cur,
                              row2 + CH - col2, (col2 >= row2) & has_prev,
                              row2 - col2, col2 <= row2, slope, sink))
        yc = _attn_merge(outs[0], outs[1]) * _silu(proj_s[rs, O_CZ:O_CZ + SEG])
        mix_s[rs, 2 * SEG:3 * SEG] = yc.astype(BF16)
        k_prev, v_prev = k_cur, v_cur

        xs, bm, cm_b, dt, acum, alast, y = _ssd_intra(conv_s[rs, SEG:SEG + D_CONV_DIM],
                                                      proj_s[rs, O_DT:O_DT + DT_PAD], dtb[0], alog[0],
                                                      same, causal)
        h_prev = h_s[...]
        h_prev_b = h_prev.astype(BF16)
        y_inter = jnp.concatenate(
            [_dot_nt(cm_b[:, g * D_STATE:(g + 1) * D_STATE], h_prev_b[g * 2 * HEAD:(g + 1) * 2 * HEAD])
             for g in range(2)], axis=1)
        y = y + y_inter * _lane_bcast_heads(jnp.exp(acum))
        wcol = jnp.exp(alast - acum) * dt
        carry = jnp.exp(alast)
        xs_t = xs.T.astype(BF16)
        for h in range(N_HEADS):
            g = h // 2
            wb_h = (bm[:, g * D_STATE:(g + 1) * D_STATE] * wcol[:, h:h + 1]).astype(BF16)
            hs = slice(h * HEAD, (h + 1) * HEAD)
            h_s[hs, :] = h_prev[hs] * carry[0:HEAD, h:h + 1] + _dot(xs_t[hs], wb_h)
        mix_s[rs, 3 * SEG:4 * SEG] = _ssd_finish(y, xs, proj_s[rs, O_DZ:O_DZ + SEG], dskip[0], dng[0]).astype(BF16)

    kprev_s[...] = k_prev
    vprev_s[...] = v_prev
    y_ref[0] = _finish_layer(x, mix_s, wout, gpost[0])

    @pl.when(j == pl.num_programs(1) - 1)
    def _():
        kn_ref[0] = proj_s[rows - CH:rows, O_CK:O_CK + KV_W]
        vn_ref[0] = proj_s[rows - CH:rows, O_CV:O_CV + KV_W]
        cbn_ref[0] = glu_s[b_hist + rows - (B_TAPS - 1):b_hist + rows, :]
        cdn_ref[0] = xbc_s[d_hist + rows - (D_TAPS - 1):d_hist + rows, :]
        ssm_ref[0] = h_s[...].reshape(N_HEADS, HEAD, D_STATE)


def _prompt_layer(layer, x, p, rows):
    batch, seq, _ = x.shape
    nb = seq // rows

    def wspec(a):
        blk = (1,) + a.shape[1:]
        zeros = (0,) * (a.ndim - 1)
        return pl.BlockSpec(blk, lambda b, j: (layer,) + zeros)

    weights = [p[k] for k in ("gpre", "gpost", "win", "wout", "alng", "alnb", "aws", "awb_p", "bcw", "bcb", "blng",
                              "blnb", "bpw", "bpb", "dcw", "dcb", "dtb", "alog", "dskip", "dng")]
    in_specs = ([pl.BlockSpec(memory_space=pltpu.SMEM),
                 pl.BlockSpec((1, rows, D_MODEL), lambda b, j: (b, j, 0))]
                + [wspec(a) for a in weights])
    out_shape = (jax.ShapeDtypeStruct((batch, seq, D_MODEL), F32),
                 jax.ShapeDtypeStruct((batch, CH, KV_W), F32),
                 jax.ShapeDtypeStruct((batch, CH, KV_W), F32),
                 jax.ShapeDtypeStruct((batch, B_TAPS - 1, SEG), F32),
                 jax.ShapeDtypeStruct((batch, D_TAPS - 1, D_CONV_DIM), F32),
                 jax.ShapeDtypeStruct((batch, N_HEADS, HEAD, D_STATE), F32))
    out_specs = (pl.BlockSpec((1, rows, D_MODEL), lambda b, j: (b, j, 0)),
                 pl.BlockSpec((1, CH, KV_W), lambda b, j: (b, 0, 0)),
                 pl.BlockSpec((1, CH, KV_W), lambda b, j: (b, 0, 0)),
                 pl.BlockSpec((1, B_TAPS - 1, SEG), lambda b, j: (b, 0, 0)),
                 pl.BlockSpec((1, D_TAPS - 1, D_CONV_DIM), lambda b, j: (b, 0, 0)),
                 pl.BlockSpec((1, N_HEADS, HEAD, D_STATE), lambda b, j: (b, 0, 0, 0)))
    scratch = [pltpu.VMEM((rows, IN_PAD), F32),
               pltpu.VMEM((rows + 32, SEG), F32),
               pltpu.VMEM((rows + 8, D_CONV_DIM), F32),
               pltpu.VMEM((rows, SEG + D_CONV_DIM), F32),
               pltpu.VMEM((CH, KV_W), BF16),
               pltpu.VMEM((CH, KV_W), BF16),
               pltpu.VMEM((N_HEADS * HEAD, D_STATE), F32),
               pltpu.VMEM((rows, D_MODEL), BF16)]
    return pl.pallas_call(
        functools.partial(_prompt_kernel, layer, rows),
        grid=(batch, nb),
        in_specs=in_specs, out_specs=out_specs, out_shape=out_shape, scratch_shapes=scratch,
        compiler_params=pltpu.CompilerParams(dimension_semantics=("arbitrary", "arbitrary"),
                                             vmem_limit_bytes=VMEM_LIMIT_BYTES),
        name=f"prompt_layer{layer}",
    )(p["sinks"], x, *weights)


def _sample_kernel(group,
                   sink_ref, x_ref, kc_ref, vc_ref, cbh_ref, cdh_ref, ssm0_ref,
                   gpre, gpost, win, wout, alng, alnb, aws, awb, bcw, bcb, blng, blnb,
                   bpw, bpb, dcw, dcb, dtb, alog, dskip, dng,
                   y_ref, kn_ref, vn_ref, cbn_ref, cdn_ref, ssm_ref, avn_ref,
                   xin_s, proj_s, new_s, conv_s, mix_s):
    layer = pl.program_id(0)
    s = pl.program_id(1)
    rows = group * SAMPLE_SEQ
    row0 = pl.multiple_of(s * rows, rows)

    @pl.when(layer == 0)
    def _():
        xin_s[...] = x_ref[...]

    @pl.when(layer > 0)
    def _():
        xin_s[...] = y_ref[pl.ds(row0, rows), :]

    x = xin_s[...]
    _project(x, gpre[0], win, proj_s)
    same, causal = _chunk_masks(SAMPLE_SEQ)

    ya, v_rows = _mixer_a(proj_s[:, O_AU:O_AU + SEG], proj_s[:, O_AV:O_AV + SEG], proj_s[:, O_AZ:O_AZ + SEG],
                          alng[0], alnb[0], aws, awb[0], causal)
    mix_s[:, 0:SEG] = ya.astype(BF16)
    avn_ref[0] = v_rows

    glu = proj_s[:, O_BV:O_BV + SEG] * jax.nn.sigmoid(proj_s[:, O_BG:O_BG + SEG])
    cbn_ref[0] = glu
    cdn_ref[0] = proj_s[:, O_DX:O_DX + D_CONV_DIM]

    def conv_tiles(hist_ref, new, w_ref, b_ref, n_taps, tile0, act):
        n_hist = n_taps - 1
        n_tiles = new.shape[1] // LANES
        for c in range(n_tiles):
            cs = slice(c * LANES, (c + 1) * LANES)
            new_s[c] = new[:, cs]
            slabs = [hist_ref[0, pl.ds(n_tiles * t + c, group, stride=n_hist * n_tiles), :] for t in range(n_hist)]
            slabs += [new_s[c, pl.ds(t, group, stride=SAMPLE_SEQ), :] for t in range(SAMPLE_SEQ)]
            for t in range(SAMPLE_SEQ):
                acc = slabs[t] * w_ref[0, 0:1, cs]
                for k in range(1, n_taps):
                    acc = acc + slabs[t + k] * w_ref[0, k:k + 1, cs]
                conv_s[tile0 + c, pl.ds(t, group, stride=SAMPLE_SEQ), :] = act(acc + b_ref[0, :, cs])

    conv_tiles(cbh_ref, glu, bcw, bcb, B_TAPS, 0, lambda a: a)
    conv_tiles(cdh_ref, proj_s[:, O_DX:O_DX + D_CONV_DIM], dcw, dcb, D_TAPS, SEG // LANES, _silu)
    conv_b = jnp.concatenate([conv_s[c] for c in range(SEG // LANES)], axis=1)
    conv_d = jnp.concatenate([conv_s[c] for c in range(SEG // LANES, (SEG + D_CONV_DIM) // LANES)], axis=1)

    yb = _mixer_b_post(conv_b, proj_s[:, O_BZ:O_BZ + SEG], blng[0], blnb[0], bpw, bpb[0])
    mix_s[:, SEG:2 * SEG] = yb.astype(BF16)

    kn_ref[0] = proj_s[:, O_CK:O_CK + KV_W]
    vn_ref[0] = proj_s[:, O_CV:O_CV + KV_W]
    sub_rows = ATT_SUB * SAMPLE_SEQ
    n_past = ATT_SUB * CH
    rr = _iota((2 * sub_rows, n_past), 0) & (sub_rows - 1)
    q_seq, q_t = rr >> 3, rr & (SAMPLE_SEQ - 1)
    col = _iota((2 * sub_rows, n_past), 1)
    rel_a = q_t + CH - (col & (CH - 1))
    ok_a = ((col >> 7) == q_seq) & ((col & (CH - 1)) >= q_t)
    rr_n = _iota((2 * sub_rows, sub_rows), 0) & (sub_rows - 1)
    col_n = _iota((2 * sub_rows, sub_rows), 1)
    rel_b = (rr_n & (SAMPLE_SEQ - 1)) - (col_n & (SAMPLE_SEQ - 1))
    ok_b = ((col_n >> 3) == (rr_n >> 3)) & (rel_b >= 0)
    for u in range(group // ATT_SUB):
        us = slice(u * sub_rows, (u + 1) * sub_rows)
        q = proj_s[us, O_CQ:O_CQ + SEG] * (HEAD ** -0.5)
        k_past = kc_ref[0, u * n_past:(u + 1) * n_past, :].astype(BF16)
        v_past = vc_ref[0, u * n_past:(u + 1) * n_past, :].astype(BF16)
        k_new = proj_s[us, O_CK:O_CK + KV_W].astype(BF16)
        v_new = proj_s[us, O_CV:O_CV + KV_W].astype(BF16)
        outs = []
        for kvh in range(2):
            slope, sink = _head_cols(sink_ref, layer, kvh, 2 * sub_rows)
            outs.append(_attn(_attn_q_rows(q, kvh), k_past, v_past, k_new, v_new,
                              rel_a, ok_a, rel_b, ok_b, slope, sink))
        yc = _attn_merge(outs[0], outs[1]) * _silu(proj_s[us, O_CZ:O_CZ + SEG])
        mix_s[us, 2 * SEG:3 * SEG] = yc.astype(BF16)

    xs, bm, cm_b, dt, acum, alast, y = _ssd_intra(conv_d, proj_s[:, O_DT:O_DT + DT_PAD], dtb[0], alog[0],
                                                  same, causal)
    state_rows = group * HEAD
    own = (_iota((state_rows, CH), 0) >> 6) == (_iota((state_rows, CH), 1) >> 3)
    wcol = jnp.exp(alast - acum) * dt
    carry = jnp.exp(alast)
    xs_t = xs.T
    y_inter_t = []
    for h in range(N_HEADS):
        g = h // 2
        gs = slice(g * D_STATE, (g + 1) * D_STATE)
        h0 = ssm0_ref[0, :, h].reshape(state_rows, D_STATE)
        full = jnp.where(own, _dot_nt(h0.astype(BF16), cm_b[:, gs]), 0.0)
        y_inter_t.append(jnp.sum(full.reshape(group, HEAD, CH), axis=0))
        xt_h = xs_t[h * HEAD:(h + 1) * HEAD]
        lhs = jnp.where(own, jnp.concatenate([xt_h] * group, axis=0), 0.0).astype(BF16)
        wb_h = (bm[:, gs] * wcol[:, h:h + 1]).astype(BF16)
        carry_h = jnp.broadcast_to(carry[:, h:h + 1], (CH, D_STATE)).reshape(group, SAMPLE_SEQ, D_STATE)
        carry_h = jnp.concatenate([carry_h] * (HEAD // SAMPLE_SEQ), axis=1).reshape(state_rows, D_STATE)
        ssm_ref[0, :, h] = (h0 * carry_h + _dot(lhs, wb_h)).reshape(group, HEAD, D_STATE)
    y_inter = jnp.concatenate(y_inter_t, axis=0).T
    y = y + y_inter * _lane_bcast_heads(jnp.exp(acum))
    mix_s[:, 3 * SEG:4 * SEG] = _ssd_finish(y, xs, proj_s[:, O_DZ:O_DZ + SEG], dskip[0], dng[0]).astype(BF16)

    y_ref[pl.ds(row0, rows), :] = _finish_layer(x, mix_s, wout, gpost[0])


def _sample_layers(x, kc, vc, cbh, cdh, ssm0, p, group):
    depth = p["win"].shape[0]
    n_seq = ssm0.shape[1]
    n_rows = n_seq * SAMPLE_SEQ
    rows = group * SAMPLE_SEQ
    ns = n_seq // group

    def wspec(a):
        blk = (1,) + a.shape[1:]
        zeros = (0,) * (a.ndim - 1)
        return pl.BlockSpec(blk, lambda i, s: (i,) + zeros)

    def rspec(width, per_seq):
        return pl.BlockSpec((1, group * per_seq, width), lambda i, s: (i, s, 0))

    weights = [p[k] for k in ("gpre", "gpost", "win", "wout", "alng", "alnb", "aws_s", "awb_s", "bcw", "bcb", "blng",
                              "blnb", "bpw", "bpb", "dcw", "dcb", "dtb", "alog", "dskip", "dng")]
    ssm_spec = pl.BlockSpec((1, group, N_HEADS, HEAD, D_STATE), lambda i, s: (i, s, 0, 0, 0))
    in_specs = ([pl.BlockSpec(memory_space=pltpu.SMEM),
                 pl.BlockSpec((rows, D_MODEL), lambda i, s: (s, 0)),
                 rspec(KV_W, CH), rspec(KV_W, CH),
                 rspec(LANES, (B_TAPS - 1) * SEG // LANES), rspec(LANES, (D_TAPS - 1) * D_CONV_DIM // LANES),
                 ssm_spec]
                + [wspec(a) for a in weights])
    out_shape = (jax.ShapeDtypeStruct((n_rows, D_MODEL), F32),
                 jax.ShapeDtypeStruct((depth, n_rows, KV_W), F32),
                 jax.ShapeDtypeStruct((depth, n_rows, KV_W), F32),
                 jax.ShapeDtypeStruct((depth, n_rows, SEG), F32),
                 jax.ShapeDtypeStruct((depth, n_rows, D_CONV_DIM), F32),
                 jax.ShapeDtypeStruct((depth, n_seq, N_HEADS, HEAD, D_STATE), F32),
                 jax.ShapeDtypeStruct((depth, n_rows, SEG), F32))
    out_specs = (pl.BlockSpec((n_rows, D_MODEL), lambda i, s: (0, 0)),
                 rspec(KV_W, SAMPLE_SEQ), rspec(KV_W, SAMPLE_SEQ),
                 rspec(SEG, SAMPLE_SEQ), rspec(D_CONV_DIM, SAMPLE_SEQ), ssm_spec,
                 rspec(SEG, SAMPLE_SEQ))
    scratch = [pltpu.VMEM((rows, D_MODEL), F32),
               pltpu.VMEM((rows, IN_PAD), F32),
               pltpu.VMEM((D_CONV_DIM // LANES, rows, LANES), F32),
               pltpu.VMEM(((SEG + D_CONV_DIM) // LANES, rows, LANES), F32),
               pltpu.VMEM((rows, D_MODEL), BF16)]
    return pl.pallas_call(
        functools.partial(_sample_kernel, group),
        grid=(depth, ns),
        in_specs=in_specs, out_specs=out_specs, out_shape=out_shape, scratch_shapes=scratch,
        compiler_params=pltpu.CompilerParams(dimension_semantics=("arbitrary", "arbitrary"),
                                             vmem_limit_bytes=VMEM_LIMIT_BYTES),
        name="sample_layers",
    )(p["sinks"], x, kc, vc, cbh, cdh, ssm0, *weights)


def _prepare_params(norm_pre, norm_post, w_in, w_out, a_ln_g, a_ln_b, a_ws, a_wb, b_conv_w, b_conv_b,
                    b_ln_g, b_ln_b, b_pw_w, b_pw_b, c_sinks, d_conv_w, d_conv_b, d_dt_bias, d_a_log,
                    d_skip, d_norm_g, group):
    depth = w_in.shape[0]
    head_perm = np.concatenate([np.arange(0, 64), np.arange(128, 192), np.arange(64, 128), np.arange(192, 256)])
    cols = np.arange(IN_RAW)
    cols[O_CQ:O_CQ + SEG] = O_CQ + head_perm
    cols[O_CZ:O_CZ + SEG] = O_CZ + head_perm
    win = jnp.pad(w_in[:, :, cols], ((0, 0), (0, 0), (0, IN_PAD - IN_RAW))).astype(BF16)
    out_rows = np.arange(4 * SEG)
    out_rows[2 * SEG:3 * SEG] = 2 * SEG + head_perm
    wout = w_out[:, out_rows, :].astype(BF16)

    def row(a):
        return a.reshape(depth, 1, -1)

    def per_head(a):
        return jnp.repeat(a, HEAD, axis=-1)

    reps = CH // SAMPLE_SEQ
    return {
        "sinks": c_sinks,
        "gpre": row(norm_pre), "gpost": row(norm_post), "win": win, "wout": wout,
        "alng": row(a_ln_g), "alnb": row(a_ln_b),
        "aws": a_ws,
        "aws_s": jnp.tile(a_ws[:, :, :SAMPLE_SEQ, :SAMPLE_SEQ], (1, 1, reps, reps)),
        "awb_p": per_head(jnp.swapaxes(a_wb, 1, 2)),
        "awb_s": per_head(jnp.tile(jnp.swapaxes(a_wb[:, :, :SAMPLE_SEQ], 1, 2), (1, reps, 1))),
        "bcw": b_conv_w, "bcb": row(b_conv_b), "blng": row(b_ln_g), "blnb": row(b_ln_b),
        "bpw": b_pw_w.astype(BF16), "bpb": row(b_pw_b),
        "dcw": d_conv_w, "dcb": row(d_conv_b),
        "dtb": jnp.pad(row(d_dt_bias), ((0, 0), (0, 0), (0, DT_PAD - N_HEADS))),
        "alog": jnp.pad(row(d_a_log), ((0, 0), (0, 0), (0, DT_PAD - N_HEADS))),
        "dskip": per_head(row(d_skip)), "dng": row(d_norm_g),
    }


PROMPT_ROWS = 256
SAMPLE_GROUP = CH // SAMPLE_SEQ


def kernel(x_prompt, x_sample, cache_win_k, cache_win_v, state_conv_b, state_conv_d, state_ssm, norm_pre, norm_post, w_in, w_out, a_ln_g, a_ln_b, a_ws, a_wb, b_conv_w, b_conv_b, b_ln_g, b_ln_b, b_pw_w, b_pw_b, c_sinks, d_conv_w, d_conv_b, d_dt_bias, d_a_log, d_skip, d_norm_g):
    depth = w_in.shape[0]
    batch, seq, _ = x_prompt.shape
    n_seq, dec_seq, _ = x_sample.shape
    assert dec_seq == SAMPLE_SEQ and cache_win_k.shape[2] == CH and seq % PROMPT_ROWS == 0
    assert n_seq % SAMPLE_GROUP == 0
    p = _prepare_params(norm_pre, norm_post, w_in, w_out, a_ln_g, a_ln_b, a_ws, a_wb, b_conv_w, b_conv_b,
                        b_ln_g, b_ln_b, b_pw_w, b_pw_b, c_sinks, d_conv_w, d_conv_b, d_dt_bias, d_a_log,
                        d_skip, d_norm_g, SAMPLE_GROUP)

    xp = x_prompt
    per_layer = []
    for i in range(depth):
        xp, *state = _prompt_layer(i, xp, p, PROMPT_ROWS)
        per_layer.append(state)
    pk, pv, pcb, pcd, pssm = (jnp.stack(list(a)) for a in zip(*per_layer))

    n_rows = n_seq * SAMPLE_SEQ
    ys, sk, sv, scb, scd, sssm, sav = _sample_layers(
        x_sample.reshape(n_rows, D_MODEL),
        cache_win_k.reshape(depth, n_seq * CH, KV_W), cache_win_v.reshape(depth, n_seq * CH, KV_W),
        state_conv_b.reshape(depth, -1, LANES), state_conv_d.reshape(depth, -1, LANES),
        state_ssm, p, SAMPLE_GROUP)

    kv_p = (depth, batch, CH, 2, HEAD)
    kv_s = (depth, n_seq, SAMPLE_SEQ, 2, HEAD)
    return (xp, ys.reshape(n_seq, SAMPLE_SEQ, D_MODEL),
            pk.reshape(kv_p), sk.reshape(kv_s), pv.reshape(kv_p), sv.reshape(kv_s),
            pcb, scb.reshape(depth, n_seq, SAMPLE_SEQ, SEG),
            pcd, scd.reshape(depth, n_seq, SAMPLE_SEQ, D_CONV_DIM),
            pssm, sssm, sav.reshape(depth, n_seq, SAMPLE_SEQ, SEG))
```

```python
import functools

import jax
import jax.numpy as jnp
from jax import lax
from jax.experimental import pallas as pl
from jax.experimental.pallas import tpu as pltpu

F32 = jnp.float32
BF16 = jnp.bfloat16
EPS = 1e-6

D_MODEL = 1024
LANES = 128
CH = 128
HEAD = 64
N_HEADS = 4
SEG = 256
KV_W = 128
B_TAPS = 31
D_TAPS = 4
D_CONV_DIM = 768
D_STATE = 128
DT_PAD = 128
SAMPLE_SEQ = 8
ATT_SUB = 4

O_AU, O_AV, O_AZ = 0, 256, 512
O_BV, O_BG, O_BZ = 768, 1024, 1280
O_CQ, O_CK, O_CV, O_CZ = 1536, 1792, 1920, 2048
O_DZ, O_DX, O_DT = 2304, 2560, 3328
IN_RAW = 3332
IN_PAD = 3456
PROJ_TILE = 1152

ALIBI = tuple(float(2.0 ** (-8.0 * (i + 1) / N_HEADS)) for i in range(N_HEADS))
VMEM_LIMIT_BYTES = 56 * 1024 * 1024


def _iota(shape, dim):
    return lax.broadcasted_iota(jnp.int32, shape, dim)


def _dot(a, b):
    return jnp.dot(a, b, preferred_element_type=F32)


def _dot_nt(a, b):
    return lax.dot_general(a, b, (((1,), (1,)), ((), ())), preferred_element_type=F32)


def _rms(x, g):
    return x * lax.rsqrt(jnp.mean(x * x, axis=-1, keepdims=True) + EPS) * g


def _ln(x, g, b):
    mu = jnp.mean(x, axis=-1, keepdims=True)
    xc = x - mu
    return xc * lax.rsqrt(jnp.mean(xc * xc, axis=-1, keepdims=True) + EPS) * g + b


def _silu(x):
    return x * jax.nn.sigmoid(x)


def _softplus(x):
    return jnp.maximum(x, 0.0) + jnp.log1p(jnp.exp(-jnp.abs(x)))


def _hi_lo(x):
    hi = x.astype(BF16)
    return hi, (x - hi.astype(F32)).astype(BF16)


def _head_of_lane(width):
    return _iota((CH, width), 1) >> 6


def _lane_bcast_heads(cols, width=SEG):
    head = _head_of_lane(width)
    out = jnp.zeros((CH, width), F32)
    for h in range(N_HEADS):
        out = jnp.where(head == h, cols[:, h:h + 1], out)
    return out


def _chunk_masks(seq_rows):
    shift = seq_rows.bit_length() - 1
    r = _iota((CH, CH), 0)
    c = _iota((CH, CH), 1)
    same = (r >> shift) == (c >> shift)
    causal = same & (c <= r)
    return same, causal


def _project(x, gpre, win_ref, proj_s):
    h = _rms(x, gpre).astype(BF16)
    for n0 in range(0, IN_PAD, PROJ_TILE):
        proj_s[:, n0:n0 + PROJ_TILE] = _dot(h, win_ref[0, :, n0:n0 + PROJ_TILE])


def _mixer_a(u_raw, v_raw, z, ln_g, ln_b, ws_ref, wb, causal, seq_rows):
    u = jax.nn.gelu(u_raw)
    v = _ln(jax.nn.gelu(v_raw), ln_g, ln_b)
    vb = v.astype(BF16)
    head = _head_of_lane(SEG)
    if seq_rows < CH:
        r = _iota((CH, CH), 0)
        c = _iota((CH, CH), 1)
        lane_tile = jnp.where((r < seq_rows) & ((c & (seq_rows - 1)) == r), 1.0, 0.0).astype(BF16)
    s = jnp.zeros((CH, SEG), F32)
    for h in range(N_HEADS):
        if seq_rows == CH:
            w = ws_ref[0, h]
        else:
            top = _dot(ws_ref[0, h, 0:seq_rows, :].astype(BF16), lane_tile)
            w = jnp.concatenate([top] * (CH // seq_rows), axis=0)
        wm = jnp.where(causal, w, 0.0).astype(BF16)
        s = jnp.where(head == h, _dot(wm, vb), s)
    return u * (s + wb) * _silu(z), v


def _mixer_b_post(c, z, ln_g, ln_b, pw_ref, pw_b):
    y = _dot(_silu(_ln(c, ln_g, ln_b)).astype(BF16), pw_ref[0]) + pw_b
    return y * _silu(z)


def _attn(q_rows, k_a, v_a, k_b, v_b, rel_a, ok_a, rel_b, ok_b, slope, sink):
    neg = -jnp.inf
    s_a = jnp.where(ok_a, _dot_nt(q_rows, k_a) - slope * rel_a.astype(F32), neg)
    s_b = jnp.where(ok_b, _dot_nt(q_rows, k_b) - slope * rel_b.astype(F32), neg)
    m = jnp.maximum(jnp.maximum(jnp.max(s_a, axis=-1, keepdims=True),
                                jnp.max(s_b, axis=-1, keepdims=True)), sink)
    p_a = jnp.exp(s_a - m)
    p_b = jnp.exp(s_b - m)
    den = (jnp.sum(p_a, axis=-1, keepdims=True) + jnp.sum(p_b, axis=-1, keepdims=True)
           + jnp.exp(sink - m))
    o = _dot(p_a.astype(BF16), v_a) + _dot(p_b.astype(BF16), v_b)
    return o / den


def _attn_q_rows(q, kvh):
    lane = _iota((q.shape[0], KV_W), 1)
    keep = (lane < HEAD) if kvh == 0 else (lane >= HEAD)
    t0 = jnp.where(keep, q[:, 0:KV_W], 0.0)
    t1 = jnp.where(keep, q[:, KV_W:2 * KV_W], 0.0)
    return jnp.concatenate([t0, t1], axis=0).astype(BF16)


def _attn_merge(o0, o1):
    half = o0.shape[0] // 2
    lo = _iota((half, KV_W), 1) < HEAD
    return jnp.concatenate([jnp.where(lo, o0[:half], o1[:half]),
                            jnp.where(lo, o0[half:], o1[half:])], axis=1)


def _head_cols(sink_ref, layer, kvh, rows):
    half = rows // 2
    first = _iota((rows, 1), 0) < half
    slope = jnp.where(first, ALIBI[2 * kvh], ALIBI[2 * kvh + 1])
    sink = jnp.where(first, sink_ref[layer, 2 * kvh], sink_ref[layer, 2 * kvh + 1])
    return slope, sink


def _ssd_intra(xbc, dt_raw, dtb, alog, same, causal):
    xs = xbc[:, 0:SEG]
    bm_b = xbc[:, SEG:2 * SEG].astype(BF16)
    cm_b = xbc[:, 2 * SEG:3 * SEG].astype(BF16)
    xs_b = xs.astype(BF16)
    dt = _softplus(dt_raw + dtb)
    da_hi, da_lo = _hi_lo(dt * (-jnp.exp(alog)))
    tri_b = jnp.where(causal, 1.0, 0.0).astype(BF16)
    same_b = jnp.where(same, 1.0, 0.0).astype(BF16)
    acum = _dot(tri_b, da_hi) + _dot(tri_b, da_lo)
    alast = _dot(same_b, da_hi) + _dot(same_b, da_lo)
    acum_t = acum.T
    dt_t = dt.T
    head = _head_of_lane(SEG)
    y = jnp.zeros((CH, SEG), F32)
    for g in range(2):
        sc = _dot_nt(cm_b[:, g * D_STATE:(g + 1) * D_STATE], bm_b[:, g * D_STATE:(g + 1) * D_STATE])
        for hh in range(2):
            h = 2 * g + hh
            seg = acum[:, h:h + 1] - acum_t[h:h + 1, :]
            decay = jnp.exp(jnp.where(causal, seg, -jnp.inf))
            m = (sc * decay * dt_t[h:h + 1, :]).astype(BF16)
            y = jnp.where(head == h, _dot(m, xs_b), y)
    return xs, xbc[:, SEG:2 * SEG], cm_b, dt, acum, alast, y


def _ssd_finish(y, xs, z, dskip, dng):
    y = (y + dskip * xs) * _silu(z)
    parts = []
    for g in range(2):
        yg = y[:, g * D_STATE:(g + 1) * D_STATE]
        parts.append(_rms(yg, dng[:, g * D_STATE:(g + 1) * D_STATE]))
    return jnp.concatenate(parts, axis=1)


def _finish_layer(x, mix_s, wout_ref, gpost):
    return x + _rms(_dot(mix_s[...], wout_ref[0]), gpost)


def _prompt_kernel(layer, rows,
                   sink_ref, x_ref, gpre, gpost, win, wout, alng, alnb, aws, awb, bcw, bcb, blng, blnb,
                   bpw, bpb, dcw, dcb, dtb, alog, dskip, dng,
                   y_ref, kn_ref, vn_ref, cbn_ref, cdn_ref, ssm_ref,
                   proj_s, glu_s, xbc_s, conv_s, kprev_s, vprev_s, h_s, mix_s):
    j = pl.program_id(1)
    n_chunks = rows // CH
    b_hist, d_hist = 32, 8

    @pl.when(j == 0)
    def _():
        glu_s[:, 0:b_hist, :] = jnp.zeros((SEG // LANES, b_hist, LANES), F32)
        xbc_s[:, 0:d_hist, :] = jnp.zeros((D_CONV_DIM // LANES, d_hist, LANES), F32)
        kprev_s[...] = jnp.zeros((CH, KV_W), BF16)
        vprev_s[...] = jnp.zeros((CH, KV_W), BF16)
        h_s[...] = jnp.zeros((N_HEADS * HEAD, D_STATE), F32)

    @pl.when(j > 0)
    def _():
        glu_s[:, 0:b_hist, :] = glu_s[:, rows:rows + b_hist, :]
        xbc_s[:, 0:d_hist, :] = xbc_s[:, rows:rows + d_hist, :]

    x = x_ref[0]
    _project(x, gpre[0], win, proj_s)

    def conv(buf, hist, n_tiles, w_ref, b_ref, n_taps, out_col, act):
        tile = 32
        for c in range(n_tiles):
            cs = slice(c * LANES, (c + 1) * LANES)
            for r0 in range(0, rows, tile):
                first = r0 + hist - (n_taps - 1)
                acc = buf[c, pl.ds(first, tile, stride=1), :] * w_ref[0, 0:1, cs]
                for k in range(1, n_taps):
                    acc = acc + buf[c, pl.ds(first + k, tile, stride=1), :] * w_ref[0, k:k + 1, cs]
                conv_s[r0:r0 + tile, out_col + c * LANES:out_col + (c + 1) * LANES] = act(acc + b_ref[0, :, cs])

    for c in range(SEG // LANES):
        glu_s[c, b_hist:b_hist + rows, :] = (proj_s[:, O_BV + c * LANES:O_BV + (c + 1) * LANES]
                                             * jax.nn.sigmoid(proj_s[:, O_BG + c * LANES:O_BG + (c + 1) * LANES]))
    for c in range(D_CONV_DIM // LANES):
        xbc_s[c, d_hist:d_hist + rows, :] = proj_s[:, O_DX + c * LANES:O_DX + (c + 1) * LANES]
    conv(glu_s, b_hist, SEG // LANES, bcw, bcb, B_TAPS, 0, lambda a: a)
    conv(xbc_s, d_hist, D_CONV_DIM // LANES, dcw, dcb, D_TAPS, SEG, _silu)

    same, causal = _chunk_masks(CH)
    row2 = _iota((2 * CH, CH), 0) & (CH - 1)
    col2 = _iota((2 * CH, CH), 1)
    k_prev = kprev_s[...]
    v_prev = vprev_s[...]
    for c in range(n_chunks):
        rs = slice(c * CH, (c + 1) * CH)

        ya, _ = _mixer_a(proj_s[rs, O_AU:O_AU + SEG], proj_s[rs, O_AV:O_AV + SEG], proj_s[rs, O_AZ:O_AZ + SEG],
                         alng[0], alnb[0], aws, awb[0], causal, CH)
        mix_s[rs, 0:SEG] = ya.astype(BF16)

        yb = _mixer_b_post(conv_s[rs, 0:SEG], proj_s[rs, O_BZ:O_BZ + SEG], blng[0], blnb[0], bpw, bpb[0])
        mix_s[rs, SEG:2 * SEG] = yb.astype(BF16)

        q = proj_s[rs, O_CQ:O_CQ + SEG] * (HEAD ** -0.5)
        k_cur = proj_s[rs, O_CK:O_CK + KV_W].astype(BF16)
        v_cur = proj_s[rs, O_CV:O_CV + KV_W].astype(BF16)
        has_prev = (j > 0) if c == 0 else True
        outs = []
        for kvh in range(2):
            slope, sink = _head_cols(sink_ref, layer, kvh, 2 * CH)
            outs.append(_attn(_attn_q_rows(q, kvh), k_prev, v_prev, k_cur, v_cur,
                              row2 + CH - col2, (col2 >= row2) & has_prev,
                              row2 - col2, col2 <= row2, slope, sink))
        yc = _attn_merge(outs[0], outs[1]) * _silu(proj_s[rs, O_CZ:O_CZ + SEG])
        mix_s[rs, 2 * SEG:3 * SEG] = yc.astype(BF16)
        k_prev, v_prev = k_cur, v_cur

        xs, bm, cm_b, dt, acum, alast, y = _ssd_intra(conv_s[rs, SEG:SEG + D_CONV_DIM],
                                                      proj_s[rs, O_DT:O_DT + DT_PAD], dtb[0], alog[0],
                                                      same, causal)
        h_prev = h_s[...]
        h_prev_b = h_prev.astype(BF16)
        y_inter = jnp.concatenate(
            [_dot_nt(cm_b[:, g * D_STATE:(g + 1) * D_STATE], h_prev_b[g * 2 * HEAD:(g + 1) * 2 * HEAD])
             for g in range(2)], axis=1)
        y = y + y_inter * _lane_bcast_heads(jnp.exp(acum))
        wcol = jnp.exp(alast - acum) * dt
        carry = jnp.exp(alast)
        xs_t = xs.T.astype(BF16)
        for h in range(N_HEADS):
            g = h // 2
            wb_h = (bm[:, g * D_STATE:(g + 1) * D_STATE] * wcol[:, h:h + 1]).astype(BF16)
            hs = slice(h * HEAD, (h + 1) * HEAD)
            h_s[hs, :] = h_prev[hs] * carry[0:HEAD, h:h + 1] + _dot(xs_t[hs], wb_h)
        mix_s[rs, 3 * SEG:4 * SEG] = _ssd_finish(y, xs, proj_s[rs, O_DZ:O_DZ + SEG], dskip[0], dng[0]).astype(BF16)

    kprev_s[...] = k_prev
    vprev_s[...] = v_prev
    y_ref[0] = _finish_layer(x, mix_s, wout, gpost[0])

    @pl.when(j == pl.num_programs(1) - 1)
    def _():
        kn_ref[0] = proj_s[rows - CH:rows, O_CK:O_CK + KV_W]
        vn_ref[0] = proj_s[rows - CH:rows, O_CV:O_CV + KV_W]
        for c in range(SEG // LANES):
            cbn_ref[0, :, c * LANES:(c + 1) * LANES] = glu_s[c, b_hist + rows - (B_TAPS - 1):b_hist + rows, :]
        for c in range(D_CONV_DIM // LANES):
            cdn_ref[0, :, c * LANES:(c + 1) * LANES] = xbc_s[c, d_hist + rows - (D_TAPS - 1):d_hist + rows, :]
        ssm_ref[0] = h_s[...].reshape(N_HEADS, HEAD, D_STATE)


def _prompt_layer(layer, x, p, rows):
    batch, seq, _ = x.shape
    nb = seq // rows

    def wspec(a):
        blk = (1,) + a.shape[1:]
        zeros = (0,) * (a.ndim - 1)
        return pl.BlockSpec(blk, lambda b, j: (layer,) + zeros)

    weights = [p[k] for k in ("gpre", "gpost", "win", "wout", "alng", "alnb", "aws", "awb_p", "bcw", "bcb", "blng",
                              "blnb", "bpw", "bpb", "dcw", "dcb", "dtb", "alog", "dskip", "dng")]
    in_specs = ([pl.BlockSpec(memory_space=pltpu.SMEM),
                 pl.BlockSpec((1, rows, D_MODEL), lambda b, j: (b, j, 0))]
                + [wspec(a) for a in weights])
    out_shape = (jax.ShapeDtypeStruct((batch, seq, D_MODEL), F32),
                 jax.ShapeDtypeStruct((batch, CH, KV_W), F32),
                 jax.ShapeDtypeStruct((batch, CH, KV_W), F32),
                 jax.ShapeDtypeStruct((batch, B_TAPS - 1, SEG), F32),
                 jax.ShapeDtypeStruct((batch, D_TAPS - 1, D_CONV_DIM), F32),
                 jax.ShapeDtypeStruct((batch, N_HEADS, HEAD, D_STATE), F32))
    out_specs = (pl.BlockSpec((1, rows, D_MODEL), lambda b, j: (b, j, 0)),
                 pl.BlockSpec((1, CH, KV_W), lambda b, j: (b, 0, 0)),
                 pl.BlockSpec((1, CH, KV_W), lambda b, j: (b, 0, 0)),
                 pl.BlockSpec((1, B_TAPS - 1, SEG), lambda b, j: (b, 0, 0)),
                 pl.BlockSpec((1, D_TAPS - 1, D_CONV_DIM), lambda b, j: (b, 0, 0)),
                 pl.BlockSpec((1, N_HEADS, HEAD, D_STATE), lambda b, j: (b, 0, 0, 0)))
    scratch = [pltpu.VMEM((rows, IN_PAD), F32),
               pltpu.VMEM((SEG // LANES, rows + 32, LANES), F32),
               pltpu.VMEM((D_CONV_DIM // LANES, rows + 8, LANES), F32),
               pltpu.VMEM((rows, SEG + D_CONV_DIM), F32),
               pltpu.VMEM((CH, KV_W), BF16),
               pltpu.VMEM((CH, KV_W), BF16),
               pltpu.VMEM((N_HEADS * HEAD, D_STATE), F32),
               pltpu.VMEM((rows, D_MODEL), BF16)]
    return pl.pallas_call(
        functools.partial(_prompt_kernel, layer, rows),
        grid=(batch, nb),
        in_specs=in_specs, out_specs=out_specs, out_shape=out_shape, scratch_shapes=scratch,
        compiler_params=pltpu.CompilerParams(dimension_semantics=("arbitrary", "arbitrary"),
                                             vmem_limit_bytes=VMEM_LIMIT_BYTES),
        name=f"prompt_layer{layer}",
    )(p["sinks"], x, *weights)


def _sample_kernel(group,
                   sink_ref, x_ref, kc_ref, vc_ref, cbh_ref, cdh_ref, ssm0_ref,
                   gpre, gpost, win, wout, alng, alnb, aws, awb, bcw, bcb, blng, blnb,
                   bpw, bpb, dcw, dcb, dtb, alog, dskip, dng,
                   y_ref, kn_ref, vn_ref, cbn_ref, cdn_ref, ssm_ref, avn_ref,
                   xin_s, proj_s, new_s, conv_s, mix_s):
    layer = pl.program_id(0)
    s = pl.program_id(1)
    rows = group * SAMPLE_SEQ
    row0 = pl.multiple_of(s * rows, rows)

    @pl.when(layer == 0)
    def _():
        xin_s[...] = x_ref[...]

    @pl.when(layer > 0)
    def _():
        xin_s[...] = y_ref[pl.ds(row0, rows), :]

    x = xin_s[...]
    _project(x, gpre[0], win, proj_s)
    same, causal = _chunk_masks(SAMPLE_SEQ)

    ya, v_rows = _mixer_a(proj_s[:, O_AU:O_AU + SEG], proj_s[:, O_AV:O_AV + SEG], proj_s[:, O_AZ:O_AZ + SEG],
                          alng[0], alnb[0], aws, awb[0], causal, SAMPLE_SEQ)
    mix_s[:, 0:SEG] = ya.astype(BF16)
    avn_ref[0] = v_rows

    glu = proj_s[:, O_BV:O_BV + SEG] * jax.nn.sigmoid(proj_s[:, O_BG:O_BG + SEG])
    cbn_ref[0] = glu
    cdn_ref[0] = proj_s[:, O_DX:O_DX + D_CONV_DIM]

    def conv_tiles(hist_ref, new, w_ref, b_ref, n_taps, tile0, act):
        n_hist = n_taps - 1
        n_tiles = new.shape[1] // LANES
        for c in range(n_tiles):
            cs = slice(c * LANES, (c + 1) * LANES)
            new_s[c] = new[:, cs]
            slabs = [hist_ref[0, pl.ds(n_tiles * t + c, group, stride=n_hist * n_tiles), :] for t in range(n_hist)]
            slabs += [new_s[c, pl.ds(t, group, stride=SAMPLE_SEQ), :] for t in range(SAMPLE_SEQ)]
            for t in range(SAMPLE_SEQ):
                acc = slabs[t] * w_ref[0, 0:1, cs]
                for k in range(1, n_taps):
                    acc = acc + slabs[t + k] * w_ref[0, k:k + 1, cs]
                conv_s[tile0 + c, pl.ds(t, group, stride=SAMPLE_SEQ), :] = act(acc + b_ref[0, :, cs])

    conv_tiles(cbh_ref, glu, bcw, bcb, B_TAPS, 0, lambda a: a)
    conv_tiles(cdh_ref, proj_s[:, O_DX:O_DX + D_CONV_DIM], dcw, dcb, D_TAPS, SEG // LANES, _silu)
    conv_b = jnp.concatenate([conv_s[c] for c in range(SEG // LANES)], axis=1)
    conv_d = jnp.concatenate([conv_s[c] for c in range(SEG // LANES, (SEG + D_CONV_DIM) // LANES)], axis=1)

    yb = _mixer_b_post(conv_b, proj_s[:, O_BZ:O_BZ + SEG], blng[0], blnb[0], bpw, bpb[0])
    mix_s[:, SEG:2 * SEG] = yb.astype(BF16)

    kn_ref[0] = proj_s[:, O_CK:O_CK + KV_W]
    vn_ref[0] = proj_s[:, O_CV:O_CV + KV_W]
    sub_rows = ATT_SUB * SAMPLE_SEQ
    sub_shift = sub_rows.bit_length() - 1
    n_q = N_HEADS * sub_rows
    n_past = ATT_SUB * CH * 2
    row = _iota((n_q, n_past), 0)
    col = _iota((n_q, n_past), 1)
    q_t = row & (SAMPLE_SEQ - 1)
    key_pos = (col >> 1) & (CH - 1)
    rel_a = q_t + CH - key_pos
    ok_a = (((col & 1) == ((row >> sub_shift) & 1)) & ((col >> 8) == ((row & (sub_rows - 1)) >> 3))
            & (key_pos >= q_t))
    row_n = _iota((n_q, 2 * sub_rows), 0)
    col_n = _iota((n_q, 2 * sub_rows), 1)
    rel_b = (row_n & (SAMPLE_SEQ - 1)) - (col_n & (SAMPLE_SEQ - 1))
    ok_b = (((col_n >> sub_shift) == ((row_n >> sub_shift) & 1))
            & (((col_n & (sub_rows - 1)) >> 3) == ((row_n & (sub_rows - 1)) >> 3)) & (rel_b >= 0))
    blk = _iota((n_q, 1), 0) >> sub_shift
    head_of_blk = (0, 2, 1, 3)
    slope = jnp.zeros((n_q, 1), F32)
    sink = jnp.zeros((n_q, 1), F32)
    for b, h in enumerate(head_of_blk):
        slope = jnp.where(blk == b, ALIBI[h], slope)
        sink = jnp.where(blk == b, sink_ref[layer, h], sink)
    for u in range(group // ATT_SUB):
        us = slice(u * sub_rows, (u + 1) * sub_rows)
        q = proj_s[us, O_CQ:O_CQ + SEG] * (HEAD ** -0.5)
        q_rows = jnp.concatenate([q[:, b * HEAD:(b + 1) * HEAD] for b in range(N_HEADS)], axis=0).astype(BF16)
        k_past = kc_ref[0, u * n_past:(u + 1) * n_past, :].astype(BF16)
        v_past = vc_ref[0, u * n_past:(u + 1) * n_past, :].astype(BF16)
        k_new = proj_s[us, O_CK:O_CK + KV_W]
        v_new = proj_s[us, O_CV:O_CV + KV_W]
        k_new = jnp.concatenate([k_new[:, 0:HEAD], k_new[:, HEAD:KV_W]], axis=0).astype(BF16)
        v_new = jnp.concatenate([v_new[:, 0:HEAD], v_new[:, HEAD:KV_W]], axis=0).astype(BF16)
        o = _attn(q_rows, k_past, v_past, k_new, v_new, rel_a, ok_a, rel_b, ok_b, slope, sink)
        yc = jnp.concatenate([o[b * sub_rows:(b + 1) * sub_rows] for b in range(N_HEADS)], axis=1)
        mix_s[us, 2 * SEG:3 * SEG] = (yc * _silu(proj_s[us, O_CZ:O_CZ + SEG])).astype(BF16)

    xs, bm, cm_b, dt, acum, alast, y = _ssd_intra(conv_d, proj_s[:, O_DT:O_DT + DT_PAD], dtb[0], alog[0],
                                                  same, causal)
    state_rows = group * HEAD
    own = (_iota((state_rows, CH), 0) >> 6) == (_iota((state_rows, CH), 1) >> 3)
    wcol = jnp.exp(alast - acum) * dt
    carry = jnp.exp(alast)
    xs_t = xs.T
    y_inter_t = []
    for h in range(N_HEADS):
        g = h // 2
        gs = slice(g * D_STATE, (g + 1) * D_STATE)
        h0 = ssm0_ref[0, :, h].reshape(state_rows, D_STATE)
        full = jnp.where(own, _dot_nt(h0.astype(BF16), cm_b[:, gs]), 0.0)
        y_inter_t.append(jnp.sum(full.reshape(group, HEAD, CH), axis=0))
        xt_h = xs_t[h * HEAD:(h + 1) * HEAD]
        lhs = jnp.where(own, jnp.concatenate([xt_h] * group, axis=0), 0.0).astype(BF16)
        wb_h = (bm[:, gs] * wcol[:, h:h + 1]).astype(BF16)
        carry_h = jnp.broadcast_to(carry[:, h:h + 1], (CH, D_STATE)).reshape(group, SAMPLE_SEQ, D_STATE)
        carry_h = jnp.concatenate([carry_h] * (HEAD // SAMPLE_SEQ), axis=1).reshape(state_rows, D_STATE)
        ssm_ref[0, :, h] = (h0 * carry_h + _dot(lhs, wb_h)).reshape(group, HEAD, D_STATE)
    y_inter = jnp.concatenate(y_inter_t, axis=0).T
    y = y + y_inter * _lane_bcast_heads(jnp.exp(acum))
    mix_s[:, 3 * SEG:4 * SEG] = _ssd_finish(y, xs, proj_s[:, O_DZ:O_DZ + SEG], dskip[0], dng[0]).astype(BF16)

    y_ref[pl.ds(row0, rows), :] = _finish_layer(x, mix_s, wout, gpost[0])


def _sample_layers(x, kc, vc, cbh, cdh, ssm0, p, group):
    depth = p["win"].shape[0]
    n_seq = ssm0.shape[1]
    n_rows = n_seq * SAMPLE_SEQ
    rows = group * SAMPLE_SEQ
    ns = n_seq // group

    def wspec(a):
        blk = (1,) + a.shape[1:]
        zeros = (0,) * (a.ndim - 1)
        return pl.BlockSpec(blk, lambda i, s: (i,) + zeros)

    def rspec(width, per_seq):
        return pl.BlockSpec((1, group * per_seq, width), lambda i, s: (i, s, 0))

    weights = [p[k] for k in ("gpre", "gpost", "win", "wout", "alng", "alnb", "aws", "awb_s", "bcw", "bcb", "blng",
                              "blnb", "bpw", "bpb", "dcw", "dcb", "dtb", "alog", "dskip", "dng")]
    ssm_spec = pl.BlockSpec((1, group, N_HEADS, HEAD, D_STATE), lambda i, s: (i, s, 0, 0, 0))
    in_specs = ([pl.BlockSpec(memory_space=pltpu.SMEM),
                 pl.BlockSpec((rows, D_MODEL), lambda i, s: (s, 0)),
                 rspec(HEAD, 2 * CH), rspec(HEAD, 2 * CH),
                 rspec(LANES, (B_TAPS - 1) * SEG // LANES), rspec(LANES, (D_TAPS - 1) * D_CONV_DIM // LANES),
                 ssm_spec]
                + [wspec(a) for a in weights])
    out_shape = (jax.ShapeDtypeStruct((n_rows, D_MODEL), F32),
                 jax.ShapeDtypeStruct((depth, n_rows, KV_W), F32),
                 jax.ShapeDtypeStruct((depth, n_rows, KV_W), F32),
                 jax.ShapeDtypeStruct((depth, n_rows, SEG), F32),
                 jax.ShapeDtypeStruct((depth, n_rows, D_CONV_DIM), F32),
                 jax.ShapeDtypeStruct((depth, n_seq, N_HEADS, HEAD, D_STATE), F32),
                 jax.ShapeDtypeStruct((depth, n_rows, SEG), F32))
    out_specs = (pl.BlockSpec((n_rows, D_MODEL), lambda i, s: (0, 0)),
                 rspec(KV_W, SAMPLE_SEQ), rspec(KV_W, SAMPLE_SEQ),
                 rspec(SEG, SAMPLE_SEQ), rspec(D_CONV_DIM, SAMPLE_SEQ), ssm_spec,
                 rspec(SEG, SAMPLE_SEQ))
    scratch = [pltpu.VMEM((rows, D_MODEL), F32),
               pltpu.VMEM((rows, IN_PAD), F32),
               pltpu.VMEM((D_CONV_DIM // LANES, rows, LANES), F32),
               pltpu.VMEM(((SEG + D_CONV_DIM) // LANES, rows, LANES), F32),
               pltpu.VMEM((rows, D_MODEL), BF16)]
    return pl.pallas_call(
        functools.partial(_sample_kernel, group),
        grid=(depth, ns),
        in_specs=in_specs, out_specs=out_specs, out_shape=out_shape, scratch_shapes=scratch,
        compiler_params=pltpu.CompilerParams(dimension_semantics=("arbitrary", "arbitrary"),
                                             vmem_limit_bytes=VMEM_LIMIT_BYTES),
        name="sample_layers",
    )(p["sinks"], x, kc, vc, cbh, cdh, ssm0, *weights)


def _prepare_params(norm_pre, norm_post, w_in, w_out, a_ln_g, a_ln_b, a_ws, a_wb, b_conv_w, b_conv_b,
                    b_ln_g, b_ln_b, b_pw_w, b_pw_b, c_sinks, d_conv_w, d_conv_b, d_dt_bias, d_a_log,
                    d_skip, d_norm_g, group):
    depth = w_in.shape[0]

    def heads_0213(o):
        return [(o, o + HEAD), (o + 2 * HEAD, o + 3 * HEAD), (o + HEAD, o + 2 * HEAD), (o + 3 * HEAD, o + 4 * HEAD)]

    in_cols = [(0, O_CQ)] + heads_0213(O_CQ) + [(O_CK, O_CZ)] + heads_0213(O_CZ) + [(O_DZ, IN_RAW)]
    win = jnp.concatenate([w_in[:, :, a:b].astype(BF16) for a, b in in_cols]
                          + [jnp.zeros((depth, D_MODEL, IN_PAD - IN_RAW), BF16)], axis=2)
    out_rows = [(0, 2 * SEG)] + heads_0213(2 * SEG) + [(3 * SEG, 4 * SEG)]
    wout = jnp.concatenate([w_out[:, a:b, :].astype(BF16) for a, b in out_rows], axis=1)

    def row(a):
        return a.reshape(depth, 1, -1)

    def per_head(a):
        return jnp.repeat(a, HEAD, axis=-1)

    reps = CH // SAMPLE_SEQ
    return {
        "sinks": c_sinks,
        "gpre": row(norm_pre), "gpost": row(norm_post), "win": win, "wout": wout,
        "alng": row(a_ln_g), "alnb": row(a_ln_b),
        "aws": a_ws,
        "awb_p": per_head(jnp.swapaxes(a_wb, 1, 2)),
        "awb_s": per_head(jnp.tile(jnp.swapaxes(a_wb[:, :, :SAMPLE_SEQ], 1, 2), (1, reps, 1))),
        "bcw": b_conv_w, "bcb": row(b_conv_b), "blng": row(b_ln_g), "blnb": row(b_ln_b),
        "bpw": b_pw_w.astype(BF16), "bpb": row(b_pw_b),
        "dcw": d_conv_w, "dcb": row(d_conv_b),
        "dtb": jnp.pad(row(d_dt_bias), ((0, 0), (0, 0), (0, DT_PAD - N_HEADS))),
        "alog": jnp.pad(row(d_a_log), ((0, 0), (0, 0), (0, DT_PAD - N_HEADS))),
        "dskip": per_head(row(d_skip)), "dng": row(d_norm_g),
    }


PROMPT_ROWS = 256
SAMPLE_GROUP = CH // SAMPLE_SEQ


def kernel(x_prompt, x_sample, cache_win_k, cache_win_v, state_conv_b, state_conv_d, state_ssm, norm_pre, norm_post, w_in, w_out, a_ln_g, a_ln_b, a_ws, a_wb, b_conv_w, b_conv_b, b_ln_g, b_ln_b, b_pw_w, b_pw_b, c_sinks, d_conv_w, d_conv_b, d_dt_bias, d_a_log, d_skip, d_norm_g):
    depth = w_in.shape[0]
    batch, seq, _ = x_prompt.shape
    n_seq, dec_seq, _ = x_sample.shape
    assert dec_seq == SAMPLE_SEQ and cache_win_k.shape[2] == CH and seq % PROMPT_ROWS == 0
    assert n_seq % SAMPLE_GROUP == 0
    p = _prepare_params(norm_pre, norm_post, w_in, w_out, a_ln_g, a_ln_b, a_ws, a_wb, b_conv_w, b_conv_b,
                        b_ln_g, b_ln_b, b_pw_w, b_pw_b, c_sinks, d_conv_w, d_conv_b, d_dt_bias, d_a_log,
                        d_skip, d_norm_g, SAMPLE_GROUP)

    xp = x_prompt
    per_layer = []
    for i in range(depth):
        xp, *state = _prompt_layer(i, xp, p, PROMPT_ROWS)
        per_layer.append(state)
    pk, pv, pcb, pcd, pssm = (jnp.stack(list(a)) for a in zip(*per_layer))

    n_rows = n_seq * SAMPLE_SEQ
    ys, sk, sv, scb, scd, sssm, sav = _sample_layers(
        x_sample.reshape(n_rows, D_MODEL),
        cache_win_k.reshape(depth, n_seq * CH * 2, HEAD), cache_win_v.reshape(depth, n_seq * CH * 2, HEAD),
        state_conv_b.reshape(depth, -1, LANES), state_conv_d.reshape(depth, -1, LANES),
        state_ssm, p, SAMPLE_GROUP)

    kv_p = (depth, batch, CH, 2, HEAD)
    kv_s = (depth, n_seq, SAMPLE_SEQ, 2, HEAD)
    return (xp, ys.reshape(n_seq, SAMPLE_SEQ, D_MODEL),
            pk.reshape(kv_p), sk.reshape(kv_s), pv.reshape(kv_p), sv.reshape(kv_s),
            pcb, scb.reshape(depth, n_seq, SAMPLE_SEQ, SEG),
            pcd, scd.reshape(depth, n_seq, SAMPLE_SEQ, D_CONV_DIM),
            pssm, sssm, sav.reshape(depth, n_seq, SAMPLE_SEQ, SEG))
```

```python
import functools

import jax
import jax.numpy as jnp
from jax import lax
from jax.experimental import pallas as pl
from jax.experimental.pallas import tpu as pltpu

F32 = jnp.float32
BF16 = jnp.bfloat16
EPS = 1e-6

D_MODEL = 1024
LANES = 128
CH = 128
HEAD = 64
N_HEADS = 4
SEG = 256
KV_W = 128
B_TAPS = 31
D_TAPS = 4
D_CONV_DIM = 768
D_STATE = 128
DT_PAD = 128
SAMPLE_SEQ = 8
ATT_SUB = 8

O_AU, O_AV, O_AZ = 0, 256, 512
O_BV, O_BG, O_BZ = 768, 1024, 1280
O_CQ, O_CK, O_CV, O_CZ = 1536, 1792, 1920, 2048
O_DZ, O_DX, O_DT = 2304, 2560, 3328
IN_RAW = 3332
IN_PAD = 3456
PROJ_TILE = 1152

ALIBI = tuple(float(2.0 ** (-8.0 * (i + 1) / N_HEADS)) for i in range(N_HEADS))
VMEM_LIMIT_BYTES = 56 * 1024 * 1024


def _iota(shape, dim):
    return lax.broadcasted_iota(jnp.int32, shape, dim)


def _dot(a, b):
    return jnp.dot(a, b, preferred_element_type=F32)


def _dot_nt(a, b):
    return lax.dot_general(a, b, (((1,), (1,)), ((), ())), preferred_element_type=F32)


def _rms(x, g):
    return x * lax.rsqrt(jnp.mean(x * x, axis=-1, keepdims=True) + EPS) * g


def _ln(x, g, b):
    mu = jnp.mean(x, axis=-1, keepdims=True)
    xc = x - mu
    return xc * lax.rsqrt(jnp.mean(xc * xc, axis=-1, keepdims=True) + EPS) * g + b


def _silu(x):
    return x * jax.nn.sigmoid(x)


def _softplus(x):
    return jnp.maximum(x, 0.0) + jnp.log1p(jnp.exp(-jnp.abs(x)))


def _hi_lo(x):
    hi = x.astype(BF16)
    return hi, (x - hi.astype(F32)).astype(BF16)


def _head_of_lane(width):
    return _iota((CH, width), 1) >> 6


def _lane_bcast_heads(cols, width=SEG):
    head = _head_of_lane(width)
    out = jnp.zeros((CH, width), F32)
    for h in range(N_HEADS):
        out = jnp.where(head == h, cols[:, h:h + 1], out)
    return out


def _chunk_masks(seq_rows):
    shift = seq_rows.bit_length() - 1
    r = _iota((CH, CH), 0)
    c = _iota((CH, CH), 1)
    same = (r >> shift) == (c >> shift)
    causal = same & (c <= r)
    return same, causal


def _project(x, gpre, win_ref, wdt_ref, proj):
    h = _rms(x, gpre).astype(BF16)
    for n0 in range(0, O_DT, PROJ_TILE):
        n1 = min(n0 + PROJ_TILE, O_DT)
        proj[:, n0:n1] = _dot(h, win_ref[0, :, n0:n1])
    proj[:, O_DT:O_DT + DT_PAD] = _dot(h, wdt_ref[0])


def _mixer_a(u_raw, v_raw, z, ln_g, ln_b, ws_ref, wb, causal, seq_rows):
    u = jax.nn.gelu(u_raw)
    v = _ln(jax.nn.gelu(v_raw), ln_g, ln_b)
    vb = v.astype(BF16)
    head = _head_of_lane(SEG)
    if seq_rows < CH:
        r = _iota((CH, CH), 0)
        c = _iota((CH, CH), 1)
        lane_tile = jnp.where((r < seq_rows) & ((c & (seq_rows - 1)) == r), 1.0, 0.0).astype(BF16)
    s = jnp.zeros((CH, SEG), F32)
    for h in range(N_HEADS):
        if seq_rows == CH:
            w = ws_ref[0, h]
        else:
            top = _dot(ws_ref[0, h, 0:seq_rows, :].astype(BF16), lane_tile)
            w = jnp.concatenate([top] * (CH // seq_rows), axis=0)
        wm = jnp.where(causal, w, 0.0).astype(BF16)
        s = jnp.where(head == h, _dot(wm, vb), s)
    return u * (s + wb) * _silu(z), v


def _mixer_b_post(c, z, ln_g, ln_b, pw_ref, pw_b):
    y = _dot(_silu(_ln(c, ln_g, ln_b)).astype(BF16), pw_ref[0]) + pw_b
    return y * _silu(z)


def _attn(qk_a, qk_b, pv_a, pv_b, rel_a, ok_a, rel_b, ok_b, slope, sink):
    neg = -jnp.inf
    s_a = jnp.where(ok_a, qk_a - slope * rel_a.astype(F32), neg)
    s_b = jnp.where(ok_b, qk_b - slope * rel_b.astype(F32), neg)
    m = jnp.maximum(jnp.maximum(jnp.max(s_a, axis=-1, keepdims=True),
                                jnp.max(s_b, axis=-1, keepdims=True)), sink)
    p_a = jnp.exp(s_a - m)
    p_b = jnp.exp(s_b - m)
    den = (jnp.sum(p_a, axis=-1, keepdims=True) + jnp.sum(p_b, axis=-1, keepdims=True)
           + jnp.exp(sink - m))
    return (pv_a(p_a.astype(BF16)) + pv_b(p_b.astype(BF16))) / den


def _swap_inner_heads(x):
    t0, t1 = x[:, 0:KV_W], x[:, KV_W:2 * KV_W]
    lo = _iota(t0.shape, 1) < HEAD
    return jnp.concatenate([jnp.where(lo, t0, pltpu.roll(t1, HEAD, 1)),
                            jnp.where(lo, pltpu.roll(t0, HEAD, 1), t1)], axis=1)


def _attn_q_rows(q, kvh):
    lane = _iota((q.shape[0], KV_W), 1)
    keep = (lane < HEAD) if kvh == 0 else (lane >= HEAD)
    t0 = jnp.where(keep, q[:, 0:KV_W], 0.0)
    t1 = jnp.where(keep, q[:, KV_W:2 * KV_W], 0.0)
    return jnp.concatenate([t0, t1], axis=0).astype(BF16)


def _attn_merge(o0, o1):
    half = o0.shape[0] // 2
    lo = _iota((half, KV_W), 1) < HEAD
    return jnp.concatenate([jnp.where(lo, o0[:half], o1[:half]),
                            jnp.where(lo, o0[half:], o1[half:])], axis=1)


def _head_cols(sink_ref, layer, kvh, rows):
    half = rows // 2
    first = _iota((rows, 1), 0) < half
    slope = jnp.where(first, ALIBI[2 * kvh], ALIBI[2 * kvh + 1])
    sink = jnp.where(first, sink_ref[layer, 2 * kvh], sink_ref[layer, 2 * kvh + 1])
    return slope, sink


def _ssd_intra(xbc, dt_raw, dtb, alog, same, causal):
    xs = xbc[:, 0:SEG]
    bm_b = xbc[:, SEG:2 * SEG].astype(BF16)
    cm_b = xbc[:, 2 * SEG:3 * SEG].astype(BF16)
    xs_b = xs.astype(BF16)
    dt = _softplus(dt_raw + dtb)
    da_hi, da_lo = _hi_lo(dt * (-jnp.exp(alog)))
    tri_b = jnp.where(causal, 1.0, 0.0).astype(BF16)
    same_b = jnp.where(same, 1.0, 0.0).astype(BF16)
    acum = _dot(tri_b, da_hi) + _dot(tri_b, da_lo)
    alast = _dot(same_b, da_hi) + _dot(same_b, da_lo)
    acum_t = acum.T
    dt_t = dt.T
    head = _head_of_lane(SEG)
    y = jnp.zeros((CH, SEG), F32)
    for g in range(2):
        sc = _dot_nt(cm_b[:, g * D_STATE:(g + 1) * D_STATE], bm_b[:, g * D_STATE:(g + 1) * D_STATE])
        for hh in range(2):
            h = 2 * g + hh
            seg = acum[:, h:h + 1] - acum_t[h:h + 1, :]
            decay = jnp.exp(jnp.where(causal, seg, -jnp.inf))
            m = (sc * decay * dt_t[h:h + 1, :]).astype(BF16)
            y = jnp.where(head == h, _dot(m, xs_b), y)
    return xs, xbc[:, SEG:2 * SEG], cm_b, dt, acum, alast, y


def _ssd_finish(y, xs, z, dskip, dng):
    y = (y + dskip * xs) * _silu(z)
    parts = []
    for g in range(2):
        yg = y[:, g * D_STATE:(g + 1) * D_STATE]
        parts.append(_rms(yg, dng[:, g * D_STATE:(g + 1) * D_STATE]))
    return jnp.concatenate(parts, axis=1)


def _finish_layer(x, mix_s, wout_ref, gpost):
    return x + _rms(_dot(mix_s[...], wout_ref[0]), gpost)


def _prompt_kernel(layer, rows,
                   sink_ref, x_ref, gpre, gpost, win, wdt, wout, alng, alnb, aws, awb, bcw, bcb, blng, blnb,
                   bpw, bpb, dcw, dcb, dtb, alog, dskip, dng,
                   y_ref, kn_ref, vn_ref, cbn_ref, cdn_ref, ssm_ref,
                   proj_s, glu_s, xbc_s, conv_s, kprev_s, vprev_s, h_s, mix_s):
    j = pl.program_id(1)
    n_chunks = rows // CH
    b_hist, d_hist = 32, 8

    @pl.when(j == 0)
    def _():
        glu_s[:, 0:b_hist, :] = jnp.zeros((SEG // LANES, b_hist, LANES), F32)
        xbc_s[:, 0:d_hist, :] = jnp.zeros((D_CONV_DIM // LANES, d_hist, LANES), F32)
        kprev_s[...] = jnp.zeros((CH, KV_W), BF16)
        vprev_s[...] = jnp.zeros((CH, KV_W), BF16)
        h_s[...] = jnp.zeros((N_HEADS * HEAD, D_STATE), F32)

    @pl.when(j > 0)
    def _():
        glu_s[:, 0:b_hist, :] = glu_s[:, rows:rows + b_hist, :]
        xbc_s[:, 0:d_hist, :] = xbc_s[:, rows:rows + d_hist, :]

    x = x_ref[0]
    _project(x, gpre[0], win, wdt, proj_s)

    def conv(buf, hist, n_tiles, w_ref, b_ref, n_taps, out_col, act):
        tile = 32
        for c in range(n_tiles):
            cs = slice(c * LANES, (c + 1) * LANES)
            for r0 in range(0, rows, tile):
                first = r0 + hist - (n_taps - 1)
                acc = buf[c, pl.ds(first, tile, stride=1), :] * w_ref[0, 0:1, cs]
                for k in range(1, n_taps):
                    acc = acc + buf[c, pl.ds(first + k, tile, stride=1), :] * w_ref[0, k:k + 1, cs]
                conv_s[r0:r0 + tile, out_col + c * LANES:out_col + (c + 1) * LANES] = act(acc + b_ref[0, :, cs])

    for c in range(SEG // LANES):
        glu_s[c, b_hist:b_hist + rows, :] = (proj_s[:, O_BV + c * LANES:O_BV + (c + 1) * LANES]
                                             * jax.nn.sigmoid(proj_s[:, O_BG + c * LANES:O_BG + (c + 1) * LANES]))
    for c in range(D_CONV_DIM // LANES):
        xbc_s[c, d_hist:d_hist + rows, :] = proj_s[:, O_DX + c * LANES:O_DX + (c + 1) * LANES]
    conv(glu_s, b_hist, SEG // LANES, bcw, bcb, B_TAPS, 0, lambda a: a)
    conv(xbc_s, d_hist, D_CONV_DIM // LANES, dcw, dcb, D_TAPS, SEG, _silu)

    same, causal = _chunk_masks(CH)
    row2 = _iota((2 * CH, CH), 0) & (CH - 1)
    col2 = _iota((2 * CH, CH), 1)
    k_prev = kprev_s[...]
    v_prev = vprev_s[...]
    for c in range(n_chunks):
        rs = slice(c * CH, (c + 1) * CH)

        ya, _ = _mixer_a(proj_s[rs, O_AU:O_AU + SEG], proj_s[rs, O_AV:O_AV + SEG], proj_s[rs, O_AZ:O_AZ + SEG],
                         alng[0], alnb[0], aws, awb[0], causal, CH)
        mix_s[rs, 0:SEG] = ya.astype(BF16)

        yb = _mixer_b_post(conv_s[rs, 0:SEG], proj_s[rs, O_BZ:O_BZ + SEG], blng[0], blnb[0], bpw, bpb[0])
        mix_s[rs, SEG:2 * SEG] = yb.astype(BF16)

        q = _swap_inner_heads(proj_s[rs, O_CQ:O_CQ + SEG] * (HEAD ** -0.5))
        k_cur = proj_s[rs, O_CK:O_CK + KV_W].astype(BF16)
        v_cur = proj_s[rs, O_CV:O_CV + KV_W].astype(BF16)
        has_prev = (j > 0) if c == 0 else True
        outs = []
        for kvh in range(2):
            slope, sink = _head_cols(sink_ref, layer, kvh, 2 * CH)
            q_rows = _attn_q_rows(q, kvh)
            outs.append(_attn(_dot_nt(q_rows, k_prev), _dot_nt(q_rows, k_cur),
                              functools.partial(_dot, b=v_prev), functools.partial(_dot, b=v_cur),
                              row2 + CH - col2, (col2 >= row2) & has_prev,
                              row2 - col2, col2 <= row2, slope, sink))
        yc = _swap_inner_heads(_attn_merge(outs[0], outs[1])) * _silu(proj_s[rs, O_CZ:O_CZ + SEG])
        mix_s[rs, 2 * SEG:3 * SEG] = yc.astype(BF16)
        k_prev, v_prev = k_cur, v_cur

        xs, bm, cm_b, dt, acum, alast, y = _ssd_intra(conv_s[rs, SEG:SEG + D_CONV_DIM],
                                                      proj_s[rs, O_DT:O_DT + DT_PAD], dtb[0], alog[0],
                                                      same, causal)
        h_prev = h_s[...]
        h_prev_b = h_prev.astype(BF16)
        y_inter = jnp.concatenate(
            [_dot_nt(cm_b[:, g * D_STATE:(g + 1) * D_STATE], h_prev_b[g * 2 * HEAD:(g + 1) * 2 * HEAD])
             for g in range(2)], axis=1)
        y = y + y_inter * _lane_bcast_heads(jnp.exp(acum))
        wcol = jnp.exp(alast - acum) * dt
        carry = jnp.exp(alast)
        xs_t = xs.T.astype(BF16)
        for h in range(N_HEADS):
            g = h // 2
            wb_h = (bm[:, g * D_STATE:(g + 1) * D_STATE] * wcol[:, h:h + 1]).astype(BF16)
            hs = slice(h * HEAD, (h + 1) * HEAD)
            h_s[hs, :] = h_prev[hs] * carry[0:HEAD, h:h + 1] + _dot(xs_t[hs], wb_h)
        mix_s[rs, 3 * SEG:4 * SEG] = _ssd_finish(y, xs, proj_s[rs, O_DZ:O_DZ + SEG], dskip[0], dng[0]).astype(BF16)

    kprev_s[...] = k_prev
    vprev_s[...] = v_prev
    y_ref[0] = _finish_layer(x, mix_s, wout, gpost[0])

    @pl.when(j == pl.num_programs(1) - 1)
    def _():
        kn_ref[0] = proj_s[rows - CH:rows, O_CK:O_CK + KV_W]
        vn_ref[0] = proj_s[rows - CH:rows, O_CV:O_CV + KV_W]
        for c in range(SEG // LANES):
            cbn_ref[0, :, c * LANES:(c + 1) * LANES] = glu_s[c, b_hist + rows - (B_TAPS - 1):b_hist + rows, :]
        for c in range(D_CONV_DIM // LANES):
            cdn_ref[0, :, c * LANES:(c + 1) * LANES] = xbc_s[c, d_hist + rows - (D_TAPS - 1):d_hist + rows, :]
        ssm_ref[0] = h_s[...].reshape(N_HEADS, HEAD, D_STATE)


def _prompt_layer(layer, x, p, rows):
    batch, seq, _ = x.shape
    nb = seq // rows

    def wspec(a):
        blk = (1,) + a.shape[1:]
        zeros = (0,) * (a.ndim - 1)
        return pl.BlockSpec(blk, lambda b, j: (layer,) + zeros)

    weights = [p[k] for k in ("gpre", "gpost", "win", "wdt", "wout", "alng", "alnb", "aws", "awb_p", "bcw", "bcb",
                              "blng", "blnb", "bpw", "bpb", "dcw", "dcb", "dtb", "alog", "dskip", "dng")]
    x_spec = pl.BlockSpec((1, rows, D_MODEL), lambda b, j: (b, j, 0))
    in_specs = [pl.BlockSpec(memory_space=pltpu.SMEM), x_spec] + [wspec(a) for a in weights]
    out_shape = (jax.ShapeDtypeStruct((batch, seq, D_MODEL), F32),
                 jax.ShapeDtypeStruct((batch, CH, KV_W), F32),
                 jax.ShapeDtypeStruct((batch, CH, KV_W), F32),
                 jax.ShapeDtypeStruct((batch, B_TAPS - 1, SEG), F32),
                 jax.ShapeDtypeStruct((batch, D_TAPS - 1, D_CONV_DIM), F32),
                 jax.ShapeDtypeStruct((batch, N_HEADS, HEAD, D_STATE), F32))
    out_specs = (x_spec,
                 pl.BlockSpec((1, CH, KV_W), lambda b, j: (b, 0, 0)),
                 pl.BlockSpec((1, CH, KV_W), lambda b, j: (b, 0, 0)),
                 pl.BlockSpec((1, B_TAPS - 1, SEG), lambda b, j: (b, 0, 0)),
                 pl.BlockSpec((1, D_TAPS - 1, D_CONV_DIM), lambda b, j: (b, 0, 0)),
                 pl.BlockSpec((1, N_HEADS, HEAD, D_STATE), lambda b, j: (b, 0, 0, 0)))
    scratch = [pltpu.VMEM((rows, IN_PAD), F32),
               pltpu.VMEM((SEG // LANES, rows + 32, LANES), F32),
               pltpu.VMEM((D_CONV_DIM // LANES, rows + 8, LANES), F32),
               pltpu.VMEM((rows, SEG + D_CONV_DIM), F32),
               pltpu.VMEM((CH, KV_W), BF16),
               pltpu.VMEM((CH, KV_W), BF16),
               pltpu.VMEM((N_HEADS * HEAD, D_STATE), F32),
               pltpu.VMEM((rows, D_MODEL), BF16)]
    return pl.pallas_call(
        functools.partial(_prompt_kernel, layer, rows),
        grid=(batch, nb),
        in_specs=in_specs, out_specs=out_specs, out_shape=out_shape, scratch_shapes=scratch,
        compiler_params=pltpu.CompilerParams(dimension_semantics=("arbitrary", "arbitrary"),
                                             vmem_limit_bytes=VMEM_LIMIT_BYTES),
        name=f"prompt_layer{layer}",
    )(p["sinks"], x, *weights)


def _sample_kernel(group,
                   sink_ref, x_ref, kc_ref, vc_ref, cbh_ref, cdh_ref, ssm0_ref,
                   gpre, gpost, win, wdt, wout, alng, alnb, aws, awb, bcw, bcb, blng, blnb,
                   bpw, bpb, dcw, dcb, dtb, alog, dskip, dng,
                   y_ref, kn_ref, vn_ref, cbn_ref, cdn_ref, ssm_ref, avn_ref,
                   xin_s, proj_s, new_s, conv_s, mix_s):
    layer = pl.program_id(0)
    s = pl.program_id(1)
    rows = group * SAMPLE_SEQ
    row0 = pl.multiple_of(s * rows, rows)

    @pl.when(layer == 0)
    def _():
        xin_s[...] = x_ref[...]

    @pl.when(layer > 0)
    def _():
        xin_s[...] = y_ref[pl.ds(row0, rows), :]

    x = xin_s[...]
    _project(x, gpre[0], win, wdt, proj_s)
    same, causal = _chunk_masks(SAMPLE_SEQ)

    ya, v_rows = _mixer_a(proj_s[:, O_AU:O_AU + SEG], proj_s[:, O_AV:O_AV + SEG], proj_s[:, O_AZ:O_AZ + SEG],
                          alng[0], alnb[0], aws, awb[0], causal, SAMPLE_SEQ)
    mix_s[:, 0:SEG] = ya.astype(BF16)
    avn_ref[0] = v_rows

    glu = proj_s[:, O_BV:O_BV + SEG] * jax.nn.sigmoid(proj_s[:, O_BG:O_BG + SEG])
    cbn_ref[0] = glu
    cdn_ref[0] = proj_s[:, O_DX:O_DX + D_CONV_DIM]

    def conv_tiles(hist_ref, new, w_ref, b_ref, n_taps, tile0, act):
        n_hist = n_taps - 1
        n_tiles = new.shape[1] // LANES
        for c in range(n_tiles):
            cs = slice(c * LANES, (c + 1) * LANES)
            new_s[c] = new[:, cs]
            slabs = [hist_ref[0, t, :, cs] for t in range(n_hist)]
            slabs += [new_s[c, pl.ds(t, group, stride=SAMPLE_SEQ), :] for t in range(SAMPLE_SEQ)]
            for t in range(SAMPLE_SEQ):
                acc = slabs[t] * w_ref[0, 0:1, cs]
                for k in range(1, n_taps):
                    acc = acc + slabs[t + k] * w_ref[0, k:k + 1, cs]
                conv_s[tile0 + c, pl.ds(t, group, stride=SAMPLE_SEQ), :] = act(acc + b_ref[0, :, cs])

    conv_tiles(cbh_ref, glu, bcw, bcb, B_TAPS, 0, lambda a: a)
    conv_tiles(cdh_ref, proj_s[:, O_DX:O_DX + D_CONV_DIM], dcw, dcb, D_TAPS, SEG // LANES, _silu)
    conv_b = jnp.concatenate([conv_s[c] for c in range(SEG // LANES)], axis=1)
    conv_d = jnp.concatenate([conv_s[c] for c in range(SEG // LANES, (SEG + D_CONV_DIM) // LANES)], axis=1)

    yb = _mixer_b_post(conv_b, proj_s[:, O_BZ:O_BZ + SEG], blng[0], blnb[0], bpw, bpb[0])
    mix_s[:, SEG:2 * SEG] = yb.astype(BF16)

    kn_ref[0] = proj_s[:, O_CK:O_CK + KV_W]
    vn_ref[0] = proj_s[:, O_CV:O_CV + KV_W]
    sub_rows = ATT_SUB * SAMPLE_SEQ
    n_past = ATT_SUB * CH
    row = _iota((2 * sub_rows, n_past), 0) & (sub_rows - 1)
    col = _iota((2 * sub_rows, n_past), 1)
    q_t = row & (SAMPLE_SEQ - 1)
    key_pos = col & (CH - 1)
    rel_a = q_t + CH - key_pos
    ok_a = ((col >> 7) == (row >> 3)) & (key_pos >= q_t)
    row_n = _iota((2 * sub_rows, sub_rows), 0) & (sub_rows - 1)
    col_n = _iota((2 * sub_rows, sub_rows), 1)
    rel_b = (row_n & (SAMPLE_SEQ - 1)) - (col_n & (SAMPLE_SEQ - 1))
    ok_b = ((col_n >> 3) == (row_n >> 3)) & (rel_b >= 0)
    for u in range(group // ATT_SUB):
        us = slice(u * sub_rows, (u + 1) * sub_rows)
        q = proj_s[us, O_CQ:O_CQ + SEG] * (HEAD ** -0.5)
        outs = []
        for kvh in range(2):
            hs = [slice((2 * kvh + i) * HEAD, (2 * kvh + i + 1) * HEAD) for i in range(2)]
            q_rows = jnp.concatenate([q[:, hs[0]], q[:, hs[1]]], axis=0).astype(BF16)
            k_t = jnp.concatenate([kc_ref[0, u * ATT_SUB + i, kvh] for i in range(ATT_SUB)], axis=1).astype(BF16)
            v_t = jnp.concatenate([vc_ref[0, u * ATT_SUB + i, kvh] for i in range(ATT_SUB)], axis=1).astype(BF16)
            k_new = proj_s[us, O_CK + kvh * HEAD:O_CK + (kvh + 1) * HEAD].astype(BF16)
            v_new = proj_s[us, O_CV + kvh * HEAD:O_CV + (kvh + 1) * HEAD].astype(BF16)
            slope, sink = _head_cols(sink_ref, layer, kvh, 2 * sub_rows)
            o = _attn(_dot(q_rows, k_t), _dot_nt(q_rows, k_new),
                      functools.partial(_dot_nt, b=v_t), functools.partial(_dot, b=v_new),
                      rel_a, ok_a, rel_b, ok_b, slope, sink)
            outs += [o[:sub_rows], o[sub_rows:]]
        yc = jnp.concatenate(outs, axis=1)
        mix_s[us, 2 * SEG:3 * SEG] = (yc * _silu(proj_s[us, O_CZ:O_CZ + SEG])).astype(BF16)

    xs, bm, cm_b, dt, acum, alast, y = _ssd_intra(conv_d, proj_s[:, O_DT:O_DT + DT_PAD], dtb[0], alog[0],
                                                  same, causal)
    state_rows = group * HEAD
    own = (_iota((state_rows, CH), 0) >> 6) == (_iota((state_rows, CH), 1) >> 3)
    wcol = jnp.exp(alast - acum) * dt
    carry = jnp.exp(alast)
    xs_t = xs.T
    y_inter_t = []
    for h in range(N_HEADS):
        g = h // 2
        gs = slice(g * D_STATE, (g + 1) * D_STATE)
        h0 = ssm0_ref[0, :, h].reshape(state_rows, D_STATE)
        full = jnp.where(own, _dot_nt(h0.astype(BF16), cm_b[:, gs]), 0.0)
        y_inter_t.append(jnp.sum(full.reshape(group, HEAD, CH), axis=0))
        xt_h = xs_t[h * HEAD:(h + 1) * HEAD]
        lhs = jnp.where(own, jnp.concatenate([xt_h] * group, axis=0), 0.0).astype(BF16)
        wb_h = (bm[:, gs] * wcol[:, h:h + 1]).astype(BF16)
        carry_h = jnp.broadcast_to(carry[:, h:h + 1], (CH, D_STATE)).reshape(group, SAMPLE_SEQ, D_STATE)
        carry_h = jnp.concatenate([carry_h] * (HEAD // SAMPLE_SEQ), axis=1).reshape(state_rows, D_STATE)
        ssm_ref[0, :, h] = (h0 * carry_h + _dot(lhs, wb_h)).reshape(group, HEAD, D_STATE)
    y_inter = jnp.concatenate(y_inter_t, axis=0).T
    y = y + y_inter * _lane_bcast_heads(jnp.exp(acum))
    mix_s[:, 3 * SEG:4 * SEG] = _ssd_finish(y, xs, proj_s[:, O_DZ:O_DZ + SEG], dskip[0], dng[0]).astype(BF16)

    y_ref[pl.ds(row0, rows), :] = _finish_layer(x, mix_s, wout, gpost[0])


def _sample_layers(x, kc, vc, cbh, cdh, ssm0, p, group):
    depth = p["win"].shape[0]
    n_seq = ssm0.shape[1]
    n_rows = n_seq * SAMPLE_SEQ
    rows = group * SAMPLE_SEQ
    ns = n_seq // group

    def wspec(a):
        blk = (1,) + a.shape[1:]
        zeros = (0,) * (a.ndim - 1)
        return pl.BlockSpec(blk, lambda i, s: (i,) + zeros)

    def rspec(width, per_seq):
        return pl.BlockSpec((1, group * per_seq, width), lambda i, s: (i, s, 0))

    weights = [p[k] for k in ("gpre", "gpost", "win", "wdt", "wout", "alng", "alnb", "aws", "awb_s", "bcw", "bcb",
                              "blng", "blnb", "bpw", "bpb", "dcw", "dcb", "dtb", "alog", "dskip", "dng")]
    ssm_spec = pl.BlockSpec((1, group, N_HEADS, HEAD, D_STATE), lambda i, s: (i, s, 0, 0, 0))
    cache_spec = pl.BlockSpec((1, group, 2, HEAD, CH), lambda i, s: (i, s, 0, 0, 0))

    def hist_spec(n_hist, width):
        return pl.BlockSpec((1, n_hist, group, width), lambda i, s: (i, 0, s, 0))

    in_specs = ([pl.BlockSpec(memory_space=pltpu.SMEM),
                 pl.BlockSpec((rows, D_MODEL), lambda i, s: (s, 0)),
                 cache_spec, cache_spec,
                 hist_spec(B_TAPS - 1, SEG), hist_spec(D_TAPS - 1, D_CONV_DIM), ssm_spec]
                + [wspec(a) for a in weights])
    out_shape = (jax.ShapeDtypeStruct((n_rows, D_MODEL), F32),
                 jax.ShapeDtypeStruct((depth, n_rows, KV_W), F32),
                 jax.ShapeDtypeStruct((depth, n_rows, KV_W), F32),
                 jax.ShapeDtypeStruct((depth, n_rows, SEG), F32),
                 jax.ShapeDtypeStruct((depth, n_rows, D_CONV_DIM), F32),
                 jax.ShapeDtypeStruct((depth, n_seq, N_HEADS, HEAD, D_STATE), F32),
                 jax.ShapeDtypeStruct((depth, n_rows, SEG), F32))
    out_specs = (pl.BlockSpec((n_rows, D_MODEL), lambda i, s: (0, 0)),
                 rspec(KV_W, SAMPLE_SEQ), rspec(KV_W, SAMPLE_SEQ),
                 rspec(SEG, SAMPLE_SEQ), rspec(D_CONV_DIM, SAMPLE_SEQ), ssm_spec,
                 rspec(SEG, SAMPLE_SEQ))
    scratch = [pltpu.VMEM((rows, D_MODEL), F32),
               pltpu.VMEM((rows, IN_PAD), F32),
               pltpu.VMEM((D_CONV_DIM // LANES, rows, LANES), F32),
               pltpu.VMEM(((SEG + D_CONV_DIM) // LANES, rows, LANES), F32),
               pltpu.VMEM((rows, D_MODEL), BF16)]
    return pl.pallas_call(
        functools.partial(_sample_kernel, group),
        grid=(depth, ns),
        in_specs=in_specs, out_specs=out_specs, out_shape=out_shape, scratch_shapes=scratch,
        compiler_params=pltpu.CompilerParams(dimension_semantics=("arbitrary", "arbitrary"),
                                             vmem_limit_bytes=VMEM_LIMIT_BYTES),
        name="sample_layers",
    )(p["sinks"], x, kc, vc, cbh, cdh, ssm0, *weights)


def _prepare_params(norm_pre, norm_post, w_in, w_out, a_ln_g, a_ln_b, a_ws, a_wb, b_conv_w, b_conv_b,
                    b_ln_g, b_ln_b, b_pw_w, b_pw_b, c_sinks, d_conv_w, d_conv_b, d_dt_bias, d_a_log,
                    d_skip, d_norm_g, group):
    depth = w_in.shape[0]
    win = w_in[:, :, :O_DT].astype(BF16)
    wdt = jnp.pad(w_in[:, :, O_DT:], ((0, 0), (0, 0), (0, DT_PAD - N_HEADS))).astype(BF16)
    wout = w_out.astype(BF16)

    def row(a):
        return a.reshape(depth, 1, -1)

    def per_head(a):
        return jnp.repeat(a, HEAD, axis=-1)

    reps = CH // SAMPLE_SEQ
    return {
        "sinks": c_sinks,
        "gpre": row(norm_pre), "gpost": row(norm_post), "win": win, "wdt": wdt, "wout": wout,
        "alng": row(a_ln_g), "alnb": row(a_ln_b),
        "aws": a_ws,
        "awb_p": per_head(jnp.swapaxes(a_wb, 1, 2)),
        "awb_s": per_head(jnp.tile(jnp.swapaxes(a_wb[:, :, :SAMPLE_SEQ], 1, 2), (1, reps, 1))),
        "bcw": b_conv_w, "bcb": row(b_conv_b), "blng": row(b_ln_g), "blnb": row(b_ln_b),
        "bpw": b_pw_w.astype(BF16), "bpb": row(b_pw_b),
        "dcw": d_conv_w, "dcb": row(d_conv_b),
        "dtb": jnp.pad(row(d_dt_bias), ((0, 0), (0, 0), (0, DT_PAD - N_HEADS))),
        "alog": jnp.pad(row(d_a_log), ((0, 0), (0, 0), (0, DT_PAD - N_HEADS))),
        "dskip": per_head(row(d_skip)), "dng": row(d_norm_g),
    }


PROMPT_ROWS = 256
SAMPLE_GROUP = CH // SAMPLE_SEQ


def kernel(x_prompt, x_sample, cache_win_k, cache_win_v, state_conv_b, state_conv_d, state_ssm, norm_pre, norm_post, w_in, w_out, a_ln_g, a_ln_b, a_ws, a_wb, b_conv_w, b_conv_b, b_ln_g, b_ln_b, b_pw_w, b_pw_b, c_sinks, d_conv_w, d_conv_b, d_dt_bias, d_a_log, d_skip, d_norm_g):
    depth = w_in.shape[0]
    batch, seq, _ = x_prompt.shape
    n_seq, dec_seq, _ = x_sample.shape
    assert dec_seq == SAMPLE_SEQ and cache_win_k.shape[2] == CH and seq % PROMPT_ROWS == 0
    assert n_seq % SAMPLE_GROUP == 0
    p = _prepare_params(norm_pre, norm_post, w_in, w_out, a_ln_g, a_ln_b, a_ws, a_wb, b_conv_w, b_conv_b,
                        b_ln_g, b_ln_b, b_pw_w, b_pw_b, c_sinks, d_conv_w, d_conv_b, d_dt_bias, d_a_log,
                        d_skip, d_norm_g, SAMPLE_GROUP)

    xp = x_prompt
    per_layer = []
    for i in range(depth):
        xp, *state = _prompt_layer(i, xp, p, PROMPT_ROWS)
        per_layer.append(state)
    pk, pv, pcb, pcd, pssm = (jnp.stack(list(a)) for a in zip(*per_layer))

    n_rows = n_seq * SAMPLE_SEQ
    ys, sk, sv, scb, scd, sssm, sav = _sample_layers(
        x_sample.reshape(n_rows, D_MODEL),
        jnp.transpose(cache_win_k, (0, 1, 3, 4, 2)), jnp.transpose(cache_win_v, (0, 1, 3, 4, 2)),
        jnp.swapaxes(state_conv_b, 1, 2), jnp.swapaxes(state_conv_d, 1, 2),
        state_ssm, p, SAMPLE_GROUP)

    kv_p = (depth, batch, CH, 2, HEAD)
    kv_s = (depth, n_seq, SAMPLE_SEQ, 2, HEAD)
    return (xp, ys.reshape(n_seq, SAMPLE_SEQ, D_MODEL),
            pk.reshape(kv_p), sk.reshape(kv_s), pv.reshape(kv_p), sv.reshape(kv_s),
            pcb, scb.reshape(depth, n_seq, SAMPLE_SEQ, SEG),
            pcd, scd.reshape(depth, n_seq, SAMPLE_SEQ, D_CONV_DIM),
            pssm, sssm, sav.reshape(depth, n_seq, SAMPLE_SEQ, SEG))
```

```python
import functools

import jax
import jax.numpy as jnp
from jax import lax
from jax.experimental import pallas as pl
from jax.experimental.pallas import tpu as pltpu

F32 = jnp.float32
BF16 = jnp.bfloat16
EPS = 1e-6

D_MODEL = 1024
LANES = 128
CH = 128
HEAD = 64
N_HEADS = 4
SEG = 256
KV_W = 128
B_TAPS = 31
D_TAPS = 4
D_CONV_DIM = 768
D_STATE = 128
DT_PAD = 128
SAMPLE_SEQ = 8
ATT_SUB = 8

O_AU, O_AV, O_AZ = 0, 256, 512
O_BV, O_BG, O_BZ = 768, 1024, 1280
O_CQ, O_CK, O_CV, O_CZ = 1536, 1792, 1920, 2048
O_DZ, O_DX, O_DT = 2304, 2560, 3328
IN_RAW = 3332
IN_PAD = 3456
PROJ_TILE = 1152

ALIBI = tuple(float(2.0 ** (-8.0 * (i + 1) / N_HEADS)) for i in range(N_HEADS))
VMEM_LIMIT_BYTES = 56 * 1024 * 1024


def _iota(shape, dim):
    return lax.broadcasted_iota(jnp.int32, shape, dim)


def _dot(a, b):
    return jnp.dot(a, b, preferred_element_type=F32)


def _dot_nt(a, b):
    return lax.dot_general(a, b, (((1,), (1,)), ((), ())), preferred_element_type=F32)


def _rms(x, g):
    return x * lax.rsqrt(jnp.mean(x * x, axis=-1, keepdims=True) + EPS) * g


def _ln(x, g, b):
    mu = jnp.mean(x, axis=-1, keepdims=True)
    xc = x - mu
    return xc * lax.rsqrt(jnp.mean(xc * xc, axis=-1, keepdims=True) + EPS) * g + b


def _silu(x):
    return x * jax.nn.sigmoid(x)


def _softplus(x):
    return jnp.maximum(x, 0.0) + jnp.log1p(jnp.exp(-jnp.abs(x)))


def _hi_lo(x):
    hi = x.astype(BF16)
    return hi, (x - hi.astype(F32)).astype(BF16)


def _head_of_lane(width):
    return _iota((CH, width), 1) >> 6


def _lane_bcast_heads(cols, width=SEG):
    head = _head_of_lane(width)
    out = jnp.zeros((CH, width), F32)
    for h in range(N_HEADS):
        out = jnp.where(head == h, cols[:, h:h + 1], out)
    return out


def _chunk_masks(seq_rows):
    shift = seq_rows.bit_length() - 1
    r = _iota((CH, CH), 0)
    c = _iota((CH, CH), 1)
    same = (r >> shift) == (c >> shift)
    causal = same & (c <= r)
    return same, causal


def _project(x, gpre, win_ref, wdt_ref, proj):
    h = _rms(x, gpre).astype(BF16)
    for n0 in range(0, O_DT, PROJ_TILE):
        n1 = min(n0 + PROJ_TILE, O_DT)
        proj[:, n0:n1] = _dot(h, win_ref[0, :, n0:n1])
    proj[:, O_DT:O_DT + DT_PAD] = _dot(h, wdt_ref[0])


def _masked_ws(ws_ref, causal, seq_rows):
    if seq_rows < CH:
        r = _iota((CH, CH), 0)
        c = _iota((CH, CH), 1)
        lane_tile = jnp.where((r < seq_rows) & ((c & (seq_rows - 1)) == r), 1.0, 0.0).astype(BF16)
    out = []
    for h in range(N_HEADS):
        if seq_rows == CH:
            w = ws_ref[0, h]
        else:
            top = _dot(ws_ref[0, h, 0:seq_rows, :].astype(BF16), lane_tile)
            w = jnp.concatenate([top] * (CH // seq_rows), axis=0)
        out.append(jnp.where(causal, w, 0.0).astype(BF16))
    return out


def _mixer_a(u_raw, v_raw, z, ln_g, ln_b, ws_masked, wb):
    u = jax.nn.gelu(u_raw)
    v = _ln(jax.nn.gelu(v_raw), ln_g, ln_b)
    vb = v.astype(BF16)
    head = _head_of_lane(SEG)
    s = jnp.zeros((CH, SEG), F32)
    for h in range(N_HEADS):
        s = jnp.where(head == h, _dot(ws_masked[h], vb), s)
    return u * (s + wb) * _silu(z), v


def _mixer_b_post(c, z, ln_g, ln_b, pw_ref, pw_b):
    y = _dot(_silu(_ln(c, ln_g, ln_b)).astype(BF16), pw_ref[0]) + pw_b
    return y * _silu(z)


def _attn_bias(rel, ok, slope):
    return jnp.where(ok, -slope * rel.astype(F32), -jnp.inf)


def _attn(parts, sink):
    scores = [qk + bias for qk, bias, _ in parts]
    same_shape = all(s.shape == scores[0].shape for s in scores)

    def row_reduce(xs, combine, reduce):
        if same_shape:
            xs = [functools.reduce(combine, xs)]
        return functools.reduce(combine, [reduce(x, axis=-1, keepdims=True) for x in xs])

    m = jnp.maximum(sink, row_reduce(scores, jnp.maximum, jnp.max))
    probs = [jnp.exp(s - m) for s in scores]
    den = jnp.exp(sink - m) + row_reduce(probs, jnp.add, jnp.sum)
    out = functools.reduce(jnp.add, [pv(p.astype(BF16)) for p, (_, _, pv) in zip(probs, parts)])
    return out / den


def _swap_inner_heads(x):
    t0, t1 = x[:, 0:KV_W], x[:, KV_W:2 * KV_W]
    lo = _iota(t0.shape, 1) < HEAD
    return jnp.concatenate([jnp.where(lo, t0, pltpu.roll(t1, HEAD, 1)),
                            jnp.where(lo, pltpu.roll(t0, HEAD, 1), t1)], axis=1)


def _attn_q_rows(q, kvh):
    lane = _iota((q.shape[0], KV_W), 1)
    keep = (lane < HEAD) if kvh == 0 else (lane >= HEAD)
    t0 = jnp.where(keep, q[:, 0:KV_W], 0.0)
    t1 = jnp.where(keep, q[:, KV_W:2 * KV_W], 0.0)
    return jnp.concatenate([t0, t1], axis=0).astype(BF16)


def _attn_merge(o0, o1):
    half = o0.shape[0] // 2
    lo = _iota((half, KV_W), 1) < HEAD
    return jnp.concatenate([jnp.where(lo, o0[:half], o1[:half]),
                            jnp.where(lo, o0[half:], o1[half:])], axis=1)


def _head_cols(sink_ref, layer, kvh, rows):
    half = rows // 2
    first = _iota((rows, 1), 0) < half
    slope = jnp.where(first, ALIBI[2 * kvh], ALIBI[2 * kvh + 1])
    sink = jnp.where(first, sink_ref[layer, 2 * kvh], sink_ref[layer, 2 * kvh + 1])
    return slope, sink


def _sum_matrices(same, causal):
    return jnp.where(same, 1.0, 0.0).astype(BF16), jnp.where(causal, 1.0, 0.0).astype(BF16)


def _ssd_intra(xbc, dt_raw, dtb, alog, same_b, tri_b, causal):
    xs = xbc[:, 0:SEG]
    bm_b = xbc[:, SEG:2 * SEG].astype(BF16)
    cm_b = xbc[:, 2 * SEG:3 * SEG].astype(BF16)
    xs_b = xs.astype(BF16)
    dt = _softplus(dt_raw + dtb)
    da_hi, da_lo = _hi_lo(dt * (-jnp.exp(alog)))
    acum = _dot(tri_b, da_hi) + _dot(tri_b, da_lo)
    alast = _dot(same_b, da_hi) + _dot(same_b, da_lo)
    acum_t = acum.T
    dt_t = dt.T
    head = _head_of_lane(SEG)
    y = jnp.zeros((CH, SEG), F32)
    for g in range(2):
        sc = _dot_nt(cm_b[:, g * D_STATE:(g + 1) * D_STATE], bm_b[:, g * D_STATE:(g + 1) * D_STATE])
        for hh in range(2):
            h = 2 * g + hh
            seg = acum[:, h:h + 1] - acum_t[h:h + 1, :]
            decay = jnp.exp(jnp.where(causal, seg, -jnp.inf))
            m = (sc * decay * dt_t[h:h + 1, :]).astype(BF16)
            y = jnp.where(head == h, _dot(m, xs_b), y)
    return xs, xbc[:, SEG:2 * SEG], cm_b, dt, acum, alast, y


def _ssd_finish(y, xs, z, dskip, dng):
    y = (y + dskip * xs) * _silu(z)
    parts = []
    for g in range(2):
        yg = y[:, g * D_STATE:(g + 1) * D_STATE]
        parts.append(_rms(yg, dng[:, g * D_STATE:(g + 1) * D_STATE]))
    return jnp.concatenate(parts, axis=1)


def _finish_layer(x, mix_s, wout_ref, gpost):
    return x + _rms(_dot(mix_s[...], wout_ref[0]), gpost)


def _prompt_kernel(layer, rows,
                   sink_ref, x_ref, gpre, gpost, win, wdt, wout, alng, alnb, aws, awb, bcw, bcb, blng, blnb,
                   bpw, bpb, dcw, dcb, dtb, alog, dskip, dng,
                   y_ref, kn_ref, vn_ref, cbn_ref, cdn_ref, ssm_ref,
                   proj_s, glu_s, xbc_s, conv_s, kprev_s, vprev_s, h_s, mix_s, bias_s):
    j = pl.program_id(1)
    n_chunks = rows // CH
    b_hist, d_hist = 32, 8

    @pl.when(j == 0)
    def _():
        glu_s[:, 0:b_hist, :] = jnp.zeros((SEG // LANES, b_hist, LANES), F32)
        xbc_s[:, 0:d_hist, :] = jnp.zeros((D_CONV_DIM // LANES, d_hist, LANES), F32)
        kprev_s[...] = jnp.zeros((CH, KV_W), BF16)
        vprev_s[...] = jnp.zeros((CH, KV_W), BF16)
        h_s[...] = jnp.zeros((N_HEADS * HEAD, D_STATE), F32)
        row2 = _iota((2 * CH, CH), 0) & (CH - 1)
        col2 = _iota((2 * CH, CH), 1)
        for kvh in range(2):
            slope, _ = _head_cols(sink_ref, layer, kvh, 2 * CH)
            bias_s[2 * kvh] = _attn_bias(row2 + CH - col2, col2 >= row2, slope)
            bias_s[2 * kvh + 1] = _attn_bias(row2 - col2, col2 <= row2, slope)

    @pl.when(j > 0)
    def _():
        glu_s[:, 0:b_hist, :] = glu_s[:, rows:rows + b_hist, :]
        xbc_s[:, 0:d_hist, :] = xbc_s[:, rows:rows + d_hist, :]

    def project(lo, hi):
        _project(x_ref[0, lo:hi], gpre[0], win, wdt, proj_s.at[lo:hi])

    def conv_inputs(lo, hi):
        for c in range(SEG // LANES):
            glu_s[c, b_hist + lo:b_hist + hi, :] = (
                proj_s[lo:hi, O_BV + c * LANES:O_BV + (c + 1) * LANES]
                * jax.nn.sigmoid(proj_s[lo:hi, O_BG + c * LANES:O_BG + (c + 1) * LANES]))
        for c in range(D_CONV_DIM // LANES):
            xbc_s[c, d_hist + lo:d_hist + hi, :] = proj_s[lo:hi, O_DX + c * LANES:O_DX + (c + 1) * LANES]

    def conv(buf, hist, n_tiles, w_ref, b_ref, n_taps, out_col, act, lo, hi):
        tile = 32
        for c in range(n_tiles):
            cs = slice(c * LANES, (c + 1) * LANES)
            for r0 in range(lo, hi, tile):
                first = r0 + hist - (n_taps - 1)
                acc = buf[c, pl.ds(first, tile, stride=1), :] * w_ref[0, 0:1, cs]
                for k in range(1, n_taps):
                    acc = acc + buf[c, pl.ds(first + k, tile, stride=1), :] * w_ref[0, k:k + 1, cs]
                conv_s[r0:r0 + tile, out_col + c * LANES:out_col + (c + 1) * LANES] = act(acc + b_ref[0, :, cs])

    def convs(lo, hi):
        conv(glu_s, b_hist, SEG // LANES, bcw, bcb, B_TAPS, 0, lambda a: a, lo, hi)
        conv(xbc_s, d_hist, D_CONV_DIM // LANES, dcw, dcb, D_TAPS, SEG, _silu, lo, hi)

    same, causal = _chunk_masks(CH)
    same_b, tri_b = _sum_matrices(same, causal)
    ws_masked = _masked_ws(aws, causal, CH)
    kv_carry = [kprev_s[...], vprev_s[...]]

    def mixers(c):
        rs = slice(c * CH, (c + 1) * CH)
        k_prev, v_prev = kv_carry

        ya, _ = _mixer_a(proj_s[rs, O_AU:O_AU + SEG], proj_s[rs, O_AV:O_AV + SEG], proj_s[rs, O_AZ:O_AZ + SEG],
                         alng[0], alnb[0], ws_masked, awb[0])
        mix_s[rs, 0:SEG] = ya.astype(BF16)

        yb = _mixer_b_post(conv_s[rs, 0:SEG], proj_s[rs, O_BZ:O_BZ + SEG], blng[0], blnb[0], bpw, bpb[0])
        mix_s[rs, SEG:2 * SEG] = yb.astype(BF16)

        q = _swap_inner_heads(proj_s[rs, O_CQ:O_CQ + SEG] * (HEAD ** -0.5))
        k_cur = proj_s[rs, O_CK:O_CK + KV_W].astype(BF16)
        v_cur = proj_s[rs, O_CV:O_CV + KV_W].astype(BF16)
        outs = []
        for kvh in range(2):
            _, sink = _head_cols(sink_ref, layer, kvh, 2 * CH)
            q_rows = _attn_q_rows(q, kvh)
            bias_prev = bias_s[2 * kvh]
            if c == 0:
                bias_prev = jnp.where(j > 0, bias_prev, -jnp.inf)
            outs.append(_attn([(_dot_nt(q_rows, k_prev), bias_prev, functools.partial(_dot, b=v_prev)),
                               (_dot_nt(q_rows, k_cur), bias_s[2 * kvh + 1], functools.partial(_dot, b=v_cur))],
                              sink))
        yc = _swap_inner_heads(_attn_merge(outs[0], outs[1])) * _silu(proj_s[rs, O_CZ:O_CZ + SEG])
        mix_s[rs, 2 * SEG:3 * SEG] = yc.astype(BF16)
        kv_carry[:] = [k_cur, v_cur]

        xs, bm, cm_b, dt, acum, alast, y = _ssd_intra(conv_s[rs, SEG:SEG + D_CONV_DIM],
                                                      proj_s[rs, O_DT:O_DT + DT_PAD], dtb[0], alog[0],
                                                      same_b, tri_b, causal)
        h_prev = h_s[...]
        h_prev_b = h_prev.astype(BF16)
        y_inter = jnp.concatenate(
            [_dot_nt(cm_b[:, g * D_STATE:(g + 1) * D_STATE], h_prev_b[g * 2 * HEAD:(g + 1) * 2 * HEAD])
             for g in range(2)], axis=1)
        y = y + y_inter * _lane_bcast_heads(jnp.exp(acum))
        wcol = jnp.exp(alast - acum) * dt
        carry = jnp.exp(alast)
        xs_t = xs.T.astype(BF16)
        for h in range(N_HEADS):
            g = h // 2
            wb_h = (bm[:, g * D_STATE:(g + 1) * D_STATE] * wcol[:, h:h + 1]).astype(BF16)
            hs = slice(h * HEAD, (h + 1) * HEAD)
            h_s[hs, :] = h_prev[hs] * carry[0:HEAD, h:h + 1] + _dot(xs_t[hs], wb_h)
        mix_s[rs, 3 * SEG:4 * SEG] = _ssd_finish(y, xs, proj_s[rs, O_DZ:O_DZ + SEG], dskip[0], dng[0]).astype(BF16)

    def finish(lo, hi):
        y_ref[0, lo:hi] = x_ref[0, lo:hi] + _rms(_dot(mix_s[lo:hi], wout[0]), gpost[0])

    half = rows // 2
    project(0, half)
    conv_inputs(0, half)
    project(half, rows)
    convs(0, half)
    conv_inputs(half, rows)
    for c in range(n_chunks // 2):
        mixers(c)
    convs(half, rows)
    finish(0, half)
    for c in range(n_chunks // 2, n_chunks):
        mixers(c)
    finish(half, rows)
    kprev_s[...] = kv_carry[0]
    vprev_s[...] = kv_carry[1]

    @pl.when(j == pl.num_programs(1) - 1)
    def _():
        kn_ref[0] = proj_s[rows - CH:rows, O_CK:O_CK + KV_W]
        vn_ref[0] = proj_s[rows - CH:rows, O_CV:O_CV + KV_W]
        for c in range(SEG // LANES):
            cbn_ref[0, :, c * LANES:(c + 1) * LANES] = glu_s[c, b_hist + rows - (B_TAPS - 1):b_hist + rows, :]
        for c in range(D_CONV_DIM // LANES):
            cdn_ref[0, :, c * LANES:(c + 1) * LANES] = xbc_s[c, d_hist + rows - (D_TAPS - 1):d_hist + rows, :]
        ssm_ref[0] = h_s[...].reshape(N_HEADS, HEAD, D_STATE)


def _prompt_layer(layer, x, p, rows):
    batch, seq, _ = x.shape
    nb = seq // rows

    def wspec(a):
        blk = (1,) + a.shape[1:]
        zeros = (0,) * (a.ndim - 1)
        return pl.BlockSpec(blk, lambda b, j: (layer,) + zeros)

    weights = [p[k] for k in ("gpre", "gpost", "win", "wdt", "wout", "alng", "alnb", "aws", "awb_p", "bcw", "bcb",
                              "blng", "blnb", "bpw", "bpb", "dcw", "dcb", "dtb", "alog", "dskip", "dng")]
    x_spec = pl.BlockSpec((1, rows, D_MODEL), lambda b, j: (b, j, 0))
    in_specs = [pl.BlockSpec(memory_space=pltpu.SMEM), x_spec] + [wspec(a) for a in weights]
    out_shape = (jax.ShapeDtypeStruct((batch, seq, D_MODEL), F32),
                 jax.ShapeDtypeStruct((batch, CH, KV_W), F32),
                 jax.ShapeDtypeStruct((batch, CH, KV_W), F32),
                 jax.ShapeDtypeStruct((batch, B_TAPS - 1, SEG), F32),
                 jax.ShapeDtypeStruct((batch, D_TAPS - 1, D_CONV_DIM), F32),
                 jax.ShapeDtypeStruct((batch, N_HEADS, HEAD, D_STATE), F32))
    out_specs = (x_spec,
                 pl.BlockSpec((1, CH, KV_W), lambda b, j: (b, 0, 0)),
                 pl.BlockSpec((1, CH, KV_W), lambda b, j: (b, 0, 0)),
                 pl.BlockSpec((1, B_TAPS - 1, SEG), lambda b, j: (b, 0, 0)),
                 pl.BlockSpec((1, D_TAPS - 1, D_CONV_DIM), lambda b, j: (b, 0, 0)),
                 pl.BlockSpec((1, N_HEADS, HEAD, D_STATE), lambda b, j: (b, 0, 0, 0)))
    scratch = [pltpu.VMEM((rows, IN_PAD), F32),
               pltpu.VMEM((SEG // LANES, rows + 32, LANES), F32),
               pltpu.VMEM((D_CONV_DIM // LANES, rows + 8, LANES), F32),
               pltpu.VMEM((rows, SEG + D_CONV_DIM), F32),
               pltpu.VMEM((CH, KV_W), BF16),
               pltpu.VMEM((CH, KV_W), BF16),
               pltpu.VMEM((N_HEADS * HEAD, D_STATE), F32),
               pltpu.VMEM((rows, D_MODEL), BF16),
               pltpu.VMEM((4, 2 * CH, CH), F32)]
    return pl.pallas_call(
        functools.partial(_prompt_kernel, layer, rows),
        grid=(batch, nb),
        in_specs=in_specs, out_specs=out_specs, out_shape=out_shape, scratch_shapes=scratch,
        compiler_params=pltpu.CompilerParams(dimension_semantics=("arbitrary", "arbitrary"),
                                             vmem_limit_bytes=VMEM_LIMIT_BYTES),
        name=f"prompt_layer{layer}",
    )(p["sinks"], x, *weights)


def _sample_kernel(group,
                   sink_ref, x_ref, kc_ref, vc_ref, cbh_ref, cdh_ref, ssm0_ref,
                   gpre, gpost, win, wdt, wout, alng, alnb, aws, awb, bcw, bcb, blng, blnb,
                   bpw, bpb, dcw, dcb, dtb, alog, dskip, dng,
                   y_ref, kn_ref, vn_ref, cbn_ref, cdn_ref, ssm_ref, avn_ref,
                   xin_s, proj_s, new_s, conv_s, mix_s, bias_a_s, bias_b_s):
    layer = pl.program_id(0)
    s = pl.program_id(1)
    rows = group * SAMPLE_SEQ
    row0 = pl.multiple_of(s * rows, rows)

    @pl.when(layer == 0)
    def _():
        xin_s[...] = x_ref[...]

    @pl.when(layer > 0)
    def _():
        xin_s[...] = y_ref[pl.ds(row0, rows), :]

    x = xin_s[...]
    _project(x, gpre[0], win, wdt, proj_s)
    same, causal = _chunk_masks(SAMPLE_SEQ)
    same_b, tri_b = _sum_matrices(same, causal)

    ya, v_rows = _mixer_a(proj_s[:, O_AU:O_AU + SEG], proj_s[:, O_AV:O_AV + SEG], proj_s[:, O_AZ:O_AZ + SEG],
                          alng[0], alnb[0], _masked_ws(aws, causal, SAMPLE_SEQ), awb[0])
    mix_s[:, 0:SEG] = ya.astype(BF16)
    avn_ref[0] = v_rows

    glu = proj_s[:, O_BV:O_BV + SEG] * jax.nn.sigmoid(proj_s[:, O_BG:O_BG + SEG])
    cbn_ref[0] = glu
    cdn_ref[0] = proj_s[:, O_DX:O_DX + D_CONV_DIM]

    def conv_tiles(hist_ref, new, w_ref, b_ref, n_taps, tile0, act):
        n_hist = n_taps - 1
        n_tiles = new.shape[1] // LANES
        for c in range(n_tiles):
            cs = slice(c * LANES, (c + 1) * LANES)
            new_s[c] = new[:, cs]
            slabs = [hist_ref[0, t, :, cs] for t in range(n_hist)]
            slabs += [new_s[c, pl.ds(t, group, stride=SAMPLE_SEQ), :] for t in range(SAMPLE_SEQ)]
            for t in range(SAMPLE_SEQ):
                acc = slabs[t] * w_ref[0, 0:1, cs]
                for k in range(1, n_taps):
                    acc = acc + slabs[t + k] * w_ref[0, k:k + 1, cs]
                conv_s[tile0 + c, pl.ds(t, group, stride=SAMPLE_SEQ), :] = act(acc + b_ref[0, :, cs])

    conv_tiles(cbh_ref, glu, bcw, bcb, B_TAPS, 0, lambda a: a)
    conv_tiles(cdh_ref, proj_s[:, O_DX:O_DX + D_CONV_DIM], dcw, dcb, D_TAPS, SEG // LANES, _silu)
    conv_b = jnp.concatenate([conv_s[c] for c in range(SEG // LANES)], axis=1)
    conv_d = jnp.concatenate([conv_s[c] for c in range(SEG // LANES, (SEG + D_CONV_DIM) // LANES)], axis=1)

    yb = _mixer_b_post(conv_b, proj_s[:, O_BZ:O_BZ + SEG], blng[0], blnb[0], bpw, bpb[0])
    mix_s[:, SEG:2 * SEG] = yb.astype(BF16)

    kn_ref[0] = proj_s[:, O_CK:O_CK + KV_W]
    vn_ref[0] = proj_s[:, O_CV:O_CV + KV_W]
    sub_rows = ATT_SUB * SAMPLE_SEQ
    n_past = ATT_SUB * CH
    row = _iota((2 * sub_rows, n_past), 0) & (sub_rows - 1)
    col = _iota((2 * sub_rows, n_past), 1)
    q_t = row & (SAMPLE_SEQ - 1)
    key_pos = col & (CH - 1)
    rel_a = q_t + CH - key_pos
    ok_a = ((col >> 7) == (row >> 3)) & (key_pos >= q_t)
    row_n = _iota((2 * sub_rows, sub_rows), 0) & (sub_rows - 1)
    col_n = _iota((2 * sub_rows, sub_rows), 1)
    rel_b = (row_n & (SAMPLE_SEQ - 1)) - (col_n & (SAMPLE_SEQ - 1))
    ok_b = ((col_n >> 3) == (row_n >> 3)) & (rel_b >= 0)

    @pl.when((layer == 0) & (s == 0))
    def _():
        for kvh in range(2):
            slope, _ = _head_cols(sink_ref, layer, kvh, 2 * sub_rows)
            bias_a_s[kvh] = _attn_bias(rel_a, ok_a, slope)
            bias_b_s[kvh] = _attn_bias(rel_b, ok_b, slope)

    for u in range(group // ATT_SUB):
        us = slice(u * sub_rows, (u + 1) * sub_rows)
        q = proj_s[us, O_CQ:O_CQ + SEG] * (HEAD ** -0.5)
        outs = []
        for kvh in range(2):
            hs = [slice((2 * kvh + i) * HEAD, (2 * kvh + i + 1) * HEAD) for i in range(2)]
            q_rows = jnp.concatenate([q[:, hs[0]], q[:, hs[1]]], axis=0).astype(BF16)
            k_t = jnp.concatenate([kc_ref[0, u * ATT_SUB + i, kvh] for i in range(ATT_SUB)], axis=1).astype(BF16)
            v_t = jnp.concatenate([vc_ref[0, u * ATT_SUB + i, kvh] for i in range(ATT_SUB)], axis=1).astype(BF16)
            k_new = proj_s[us, O_CK + kvh * HEAD:O_CK + (kvh + 1) * HEAD].astype(BF16)
            v_new = proj_s[us, O_CV + kvh * HEAD:O_CV + (kvh + 1) * HEAD].astype(BF16)
            _, sink = _head_cols(sink_ref, layer, kvh, 2 * sub_rows)
            o = _attn([(_dot(q_rows, k_t), bias_a_s[kvh], functools.partial(_dot_nt, b=v_t)),
                       (_dot_nt(q_rows, k_new), bias_b_s[kvh], functools.partial(_dot, b=v_new))], sink)
            outs += [o[:sub_rows], o[sub_rows:]]
        yc = jnp.concatenate(outs, axis=1)
        mix_s[us, 2 * SEG:3 * SEG] = (yc * _silu(proj_s[us, O_CZ:O_CZ + SEG])).astype(BF16)

    xs, bm, cm_b, dt, acum, alast, y = _ssd_intra(conv_d, proj_s[:, O_DT:O_DT + DT_PAD], dtb[0], alog[0],
                                                  same_b, tri_b, causal)
    state_rows = group * HEAD
    own = (_iota((state_rows, CH), 0) >> 6) == (_iota((state_rows, CH), 1) >> 3)
    wcol = jnp.exp(alast - acum) * dt
    carry = jnp.exp(alast)
    xs_t = xs.T
    y_inter_t = []
    for h in range(N_HEADS):
        g = h // 2
        gs = slice(g * D_STATE, (g + 1) * D_STATE)
        h0 = ssm0_ref[0, :, h].reshape(state_rows, D_STATE)
        full = jnp.where(own, _dot_nt(h0.astype(BF16), cm_b[:, gs]), 0.0)
        y_inter_t.append(jnp.sum(full.reshape(group, HEAD, CH), axis=0))
        xt_h = xs_t[h * HEAD:(h + 1) * HEAD]
        lhs = jnp.where(own, jnp.concatenate([xt_h] * group, axis=0), 0.0).astype(BF16)
        wb_h = (bm[:, gs] * wcol[:, h:h + 1]).astype(BF16)
        carry_h = jnp.broadcast_to(carry[:, h:h + 1], (CH, D_STATE)).reshape(group, SAMPLE_SEQ, D_STATE)
        carry_h = jnp.concatenate([carry_h] * (HEAD // SAMPLE_SEQ), axis=1).reshape(state_rows, D_STATE)
        ssm_ref[0, :, h] = (h0 * carry_h + _dot(lhs, wb_h)).reshape(group, HEAD, D_STATE)
    y_inter = jnp.concatenate(y_inter_t, axis=0).T
    y = y + y_inter * _lane_bcast_heads(jnp.exp(acum))
    mix_s[:, 3 * SEG:4 * SEG] = _ssd_finish(y, xs, proj_s[:, O_DZ:O_DZ + SEG], dskip[0], dng[0]).astype(BF16)

    y_ref[pl.ds(row0, rows), :] = _finish_layer(x, mix_s, wout, gpost[0])


def _sample_layers(x, kc, vc, cbh, cdh, ssm0, p, group):
    depth = p["win"].shape[0]
    n_seq = ssm0.shape[1]
    n_rows = n_seq * SAMPLE_SEQ
    rows = group * SAMPLE_SEQ
    ns = n_seq // group

    def wspec(a):
        blk = (1,) + a.shape[1:]
        zeros = (0,) * (a.ndim - 1)
        return pl.BlockSpec(blk, lambda i, s: (i,) + zeros)

    def rspec(width, per_seq):
        return pl.BlockSpec((1, group * per_seq, width), lambda i, s: (i, s, 0))

    weights = [p[k] for k in ("gpre", "gpost", "win", "wdt", "wout", "alng", "alnb", "aws", "awb_s", "bcw", "bcb",
                              "blng", "blnb", "bpw", "bpb", "dcw", "dcb", "dtb", "alog", "dskip", "dng")]
    ssm_spec = pl.BlockSpec((1, group, N_HEADS, HEAD, D_STATE), lambda i, s: (i, s, 0, 0, 0))
    cache_spec = pl.BlockSpec((1, group, 2, HEAD, CH), lambda i, s: (i, s, 0, 0, 0))

    def hist_spec(n_hist, width):
        return pl.BlockSpec((1, n_hist, group, width), lambda i, s: (i, 0, s, 0))

    in_specs = ([pl.BlockSpec(memory_space=pltpu.SMEM),
                 pl.BlockSpec((rows, D_MODEL), lambda i, s: (s, 0)),
                 cache_spec, cache_spec,
                 hist_spec(B_TAPS - 1, SEG), hist_spec(D_TAPS - 1, D_CONV_DIM), ssm_spec]
                + [wspec(a) for a in weights])
    out_shape = (jax.ShapeDtypeStruct((n_rows, D_MODEL), F32),
                 jax.ShapeDtypeStruct((depth, n_rows, KV_W), F32),
                 jax.ShapeDtypeStruct((depth, n_rows, KV_W), F32),
                 jax.ShapeDtypeStruct((depth, n_rows, SEG), F32),
                 jax.ShapeDtypeStruct((depth, n_rows, D_CONV_DIM), F32),
                 jax.ShapeDtypeStruct((depth, n_seq, N_HEADS, HEAD, D_STATE), F32),
                 jax.ShapeDtypeStruct((depth, n_rows, SEG), F32))
    out_specs = (pl.BlockSpec((n_rows, D_MODEL), lambda i, s: (0, 0)),
                 rspec(KV_W, SAMPLE_SEQ), rspec(KV_W, SAMPLE_SEQ),
                 rspec(SEG, SAMPLE_SEQ), rspec(D_CONV_DIM, SAMPLE_SEQ), ssm_spec,
                 rspec(SEG, SAMPLE_SEQ))
    scratch = [pltpu.VMEM((rows, D_MODEL), F32),
               pltpu.VMEM((rows, IN_PAD), F32),
               pltpu.VMEM((D_CONV_DIM // LANES, rows, LANES), F32),
               pltpu.VMEM(((SEG + D_CONV_DIM) // LANES, rows, LANES), F32),
               pltpu.VMEM((rows, D_MODEL), BF16),
               pltpu.VMEM((2, 2 * ATT_SUB * SAMPLE_SEQ, ATT_SUB * CH), F32),
               pltpu.VMEM((2, 2 * ATT_SUB * SAMPLE_SEQ, ATT_SUB * SAMPLE_SEQ), F32)]
    return pl.pallas_call(
        functools.partial(_sample_kernel, group),
        grid=(depth, ns),
        in_specs=in_specs, out_specs=out_specs, out_shape=out_shape, scratch_shapes=scratch,
        compiler_params=pltpu.CompilerParams(dimension_semantics=("arbitrary", "arbitrary"),
                                             vmem_limit_bytes=VMEM_LIMIT_BYTES),
        name="sample_layers",
    )(p["sinks"], x, kc, vc, cbh, cdh, ssm0, *weights)


def _prepare_params(norm_pre, norm_post, w_in, w_out, a_ln_g, a_ln_b, a_ws, a_wb, b_conv_w, b_conv_b,
                    b_ln_g, b_ln_b, b_pw_w, b_pw_b, c_sinks, d_conv_w, d_conv_b, d_dt_bias, d_a_log,
                    d_skip, d_norm_g, group):
    depth = w_in.shape[0]
    win = w_in[:, :, :O_DT].astype(BF16)
    wdt = jnp.pad(w_in[:, :, O_DT:], ((0, 0), (0, 0), (0, DT_PAD - N_HEADS))).astype(BF16)
    wout = w_out.astype(BF16)

    def row(a):
        return a.reshape(depth, 1, -1)

    def per_head(a):
        return jnp.repeat(a, HEAD, axis=-1)

    reps = CH // SAMPLE_SEQ
    return {
        "sinks": c_sinks,
        "gpre": row(norm_pre), "gpost": row(norm_post), "win": win, "wdt": wdt, "wout": wout,
        "alng": row(a_ln_g), "alnb": row(a_ln_b),
        "aws": a_ws,
        "awb_p": per_head(jnp.swapaxes(a_wb, 1, 2)),
        "awb_s": per_head(jnp.tile(jnp.swapaxes(a_wb[:, :, :SAMPLE_SEQ], 1, 2), (1, reps, 1))),
        "bcw": b_conv_w, "bcb": row(b_conv_b), "blng": row(b_ln_g), "blnb": row(b_ln_b),
        "bpw": b_pw_w.astype(BF16), "bpb": row(b_pw_b),
        "dcw": d_conv_w, "dcb": row(d_conv_b),
        "dtb": jnp.pad(row(d_dt_bias), ((0, 0), (0, 0), (0, DT_PAD - N_HEADS))),
        "alog": jnp.pad(row(d_a_log), ((0, 0), (0, 0), (0, DT_PAD - N_HEADS))),
        "dskip": per_head(row(d_skip)), "dng": row(d_norm_g),
    }


PROMPT_ROWS = 512
SAMPLE_GROUP = CH // SAMPLE_SEQ


def kernel(x_prompt, x_sample, cache_win_k, cache_win_v, state_conv_b, state_conv_d, state_ssm, norm_pre, norm_post, w_in, w_out, a_ln_g, a_ln_b, a_ws, a_wb, b_conv_w, b_conv_b, b_ln_g, b_ln_b, b_pw_w, b_pw_b, c_sinks, d_conv_w, d_conv_b, d_dt_bias, d_a_log, d_skip, d_norm_g):
    depth = w_in.shape[0]
    batch, seq, _ = x_prompt.shape
    n_seq, dec_seq, _ = x_sample.shape
    assert dec_seq == SAMPLE_SEQ and cache_win_k.shape[2] == CH and seq % PROMPT_ROWS == 0
    assert n_seq % SAMPLE_GROUP == 0
    p = _prepare_params(norm_pre, norm_post, w_in, w_out, a_ln_g, a_ln_b, a_ws, a_wb, b_conv_w, b_conv_b,
                        b_ln_g, b_ln_b, b_pw_w, b_pw_b, c_sinks, d_conv_w, d_conv_b, d_dt_bias, d_a_log,
                        d_skip, d_norm_g, SAMPLE_GROUP)

    xp = x_prompt
    per_layer = []
    for i in range(depth):
        xp, *state = _prompt_layer(i, xp, p, PROMPT_ROWS)
        per_layer.append(state)
    pk, pv, pcb, pcd, pssm = (jnp.stack(list(a)) for a in zip(*per_layer))

    n_rows = n_seq * SAMPLE_SEQ
    ys, sk, sv, scb, scd, sssm, sav = _sample_layers(
        x_sample.reshape(n_rows, D_MODEL),
        jnp.transpose(cache_win_k, (0, 1, 3, 4, 2)), jnp.transpose(cache_win_v, (0, 1, 3, 4, 2)),
        jnp.swapaxes(state_conv_b, 1, 2), jnp.swapaxes(state_conv_d, 1, 2),
        state_ssm, p, SAMPLE_GROUP)

    kv_p = (depth, batch, CH, 2, HEAD)
    kv_s = (depth, n_seq, SAMPLE_SEQ, 2, HEAD)
    return (xp, ys.reshape(n_seq, SAMPLE_SEQ, D_MODEL),
            pk.reshape(kv_p), sk.reshape(kv_s), pv.reshape(kv_p), sv.reshape(kv_s),
            pcb, scb.reshape(depth, n_seq, SAMPLE_SEQ, SEG),
            pcd, scd.reshape(depth, n_seq, SAMPLE_SEQ, D_CONV_DIM),
            pssm, sssm, sav.reshape(depth, n_seq, SAMPLE_SEQ, SEG))
```

```python
import functools

import jax
import jax.numpy as jnp
from jax import lax
from jax.experimental import pallas as pl
from jax.experimental.pallas import tpu as pltpu

F32 = jnp.float32
BF16 = jnp.bfloat16
EPS = 1e-6

D_MODEL = 1024
LANES = 128
CH = 128
HEAD = 64
N_HEADS = 4
SEG = 256
KV_W = 128
B_TAPS = 31
D_TAPS = 4
D_CONV_DIM = 768
D_STATE = 128
DT_PAD = 128
SAMPLE_SEQ = 8
ATT_SUB = 8

O_AU, O_AV, O_AZ = 0, 256, 512
O_BV, O_BG, O_BZ = 768, 1024, 1280
O_CQ, O_CK, O_CV, O_CZ = 1536, 1792, 1920, 2048
O_DZ, O_DX, O_DT = 2304, 2560, 3328
IN_RAW = 3332
IN_PAD = 3456
PROJ_TILE = 1152

ALIBI = tuple(float(2.0 ** (-8.0 * (i + 1) / N_HEADS)) for i in range(N_HEADS))
VMEM_LIMIT_BYTES = 56 * 1024 * 1024


VEC_NAMES = ("gpre", "gpost", "alng", "alnb", "bcb", "blng", "blnb", "bpb", "dcb", "dtb", "alog", "dskip", "dng")


def _layer_rows(layer, refs):
    return [r[pl.ds(layer, 1), :][None] for r in refs]


def _iota(shape, dim):
    return lax.broadcasted_iota(jnp.int32, shape, dim)


def _dot(a, b):
    return jnp.dot(a, b, preferred_element_type=F32)


def _dot_nt(a, b):
    return lax.dot_general(a, b, (((1,), (1,)), ((), ())), preferred_element_type=F32)


def _rms(x, g):
    return x * lax.rsqrt(jnp.mean(x * x, axis=-1, keepdims=True) + EPS) * g


def _ln(x, g, b):
    mu = jnp.mean(x, axis=-1, keepdims=True)
    xc = x - mu
    return xc * lax.rsqrt(jnp.mean(xc * xc, axis=-1, keepdims=True) + EPS) * g + b


def _silu(x):
    return x * jax.nn.sigmoid(x)


def _softplus(x):
    return jnp.maximum(x, 0.0) + jnp.log1p(jnp.exp(-jnp.abs(x)))


def _hi_lo(x):
    hi = x.astype(BF16)
    return hi, (x - hi.astype(F32)).astype(BF16)


def _head_of_lane(width):
    return _iota((CH, width), 1) >> 6


def _lane_bcast_heads(cols, width=SEG):
    head = _head_of_lane(width)
    out = jnp.zeros((CH, width), F32)
    for h in range(N_HEADS):
        out = jnp.where(head == h, cols[:, h:h + 1], out)
    return out


def _chunk_masks(seq_rows):
    shift = seq_rows.bit_length() - 1
    r = _iota((CH, CH), 0)
    c = _iota((CH, CH), 1)
    same = (r >> shift) == (c >> shift)
    causal = same & (c <= r)
    return same, causal


def _project(x, gpre, win_ref, wdt_ref, proj):
    h = _rms(x, gpre).astype(BF16)
    for n0 in range(0, O_DT, PROJ_TILE):
        n1 = min(n0 + PROJ_TILE, O_DT)
        proj[:, n0:n1] = _dot(h, win_ref[0, :, n0:n1])
    proj[:, O_DT:O_DT + DT_PAD] = _dot(h, wdt_ref[0])


def _masked_ws(ws_ref, causal, seq_rows):
    if seq_rows < CH:
        r = _iota((CH, CH), 0)
        c = _iota((CH, CH), 1)
        lane_tile = jnp.where((r < seq_rows) & ((c & (seq_rows - 1)) == r), 1.0, 0.0).astype(BF16)
    out = []
    for h in range(N_HEADS):
        if seq_rows == CH:
            w = ws_ref[0, h]
        else:
            top = _dot(ws_ref[0, h, 0:seq_rows, :].astype(BF16), lane_tile)
            w = jnp.concatenate([top] * (CH // seq_rows), axis=0)
        out.append(jnp.where(causal, w, 0.0).astype(BF16))
    return out


def _mixer_a(u_raw, v_raw, z, ln_g, ln_b, ws_masked, wb):
    u = jax.nn.gelu(u_raw)
    v = _ln(jax.nn.gelu(v_raw), ln_g, ln_b)
    vb = v.astype(BF16)
    head = _head_of_lane(SEG)
    s = jnp.zeros((CH, SEG), F32)
    for h in range(N_HEADS):
        s = jnp.where(head == h, _dot(ws_masked[h], vb), s)
    return u * (s + wb) * _silu(z), v


def _mixer_b_post(c, z, ln_g, ln_b, pw_ref, pw_b):
    y = _dot(_silu(_ln(c, ln_g, ln_b)).astype(BF16), pw_ref[0]) + pw_b
    return y * _silu(z)


def _attn_bias(rel, ok, slope):
    return jnp.where(ok, -slope * rel.astype(F32), -jnp.inf)


def _attn(parts, sink):
    scores = [qk + bias for qk, bias, _ in parts]
    same_shape = all(s.shape == scores[0].shape for s in scores)

    def row_reduce(xs, combine, reduce):
        if same_shape:
            xs = [functools.reduce(combine, xs)]
        return functools.reduce(combine, [reduce(x, axis=-1, keepdims=True) for x in xs])

    m = jnp.maximum(sink, row_reduce(scores, jnp.maximum, jnp.max))
    probs = [jnp.exp(s - m) for s in scores]
    den = jnp.exp(sink - m) + row_reduce(probs, jnp.add, jnp.sum)
    out = functools.reduce(jnp.add, [pv(p.astype(BF16)) for p, (_, _, pv) in zip(probs, parts)])
    return out / den


def _swap_inner_heads(x):
    t0, t1 = x[:, 0:KV_W], x[:, KV_W:2 * KV_W]
    lo = _iota(t0.shape, 1) < HEAD
    return jnp.concatenate([jnp.where(lo, t0, pltpu.roll(t1, HEAD, 1)),
                            jnp.where(lo, pltpu.roll(t0, HEAD, 1), t1)], axis=1)


def _attn_q_rows(q, kvh):
    lane = _iota((q.shape[0], KV_W), 1)
    keep = (lane < HEAD) if kvh == 0 else (lane >= HEAD)
    t0 = jnp.where(keep, q[:, 0:KV_W], 0.0)
    t1 = jnp.where(keep, q[:, KV_W:2 * KV_W], 0.0)
    return jnp.concatenate([t0, t1], axis=0).astype(BF16)


def _attn_merge(o0, o1):
    half = o0.shape[0] // 2
    lo = _iota((half, KV_W), 1) < HEAD
    return jnp.concatenate([jnp.where(lo, o0[:half], o1[:half]),
                            jnp.where(lo, o0[half:], o1[half:])], axis=1)


def _head_cols(sink_ref, layer, kvh, rows):
    half = rows // 2
    first = _iota((rows, 1), 0) < half
    slope = jnp.where(first, ALIBI[2 * kvh], ALIBI[2 * kvh + 1])
    sink = jnp.where(first, sink_ref[layer, 2 * kvh], sink_ref[layer, 2 * kvh + 1])
    return slope, sink


def _sum_matrices(same, causal):
    return jnp.where(same, 1.0, 0.0).astype(BF16), jnp.where(causal, 1.0, 0.0).astype(BF16)


def _ssd_intra(xbc, dt_raw, dtb, alog, same_b, tri_b, causal):
    xs = xbc[:, 0:SEG]
    bm_b = xbc[:, SEG:2 * SEG].astype(BF16)
    cm_b = xbc[:, 2 * SEG:3 * SEG].astype(BF16)
    xs_b = xs.astype(BF16)
    dt = _softplus(dt_raw + dtb)
    da_hi, da_lo = _hi_lo(dt * (-jnp.exp(alog)))
    acum = _dot(tri_b, da_hi) + _dot(tri_b, da_lo)
    alast = _dot(same_b, da_hi) + _dot(same_b, da_lo)
    acum_t = acum.T
    dt_t = dt.T
    head = _head_of_lane(SEG)
    y = jnp.zeros((CH, SEG), F32)
    for g in range(2):
        sc = _dot_nt(cm_b[:, g * D_STATE:(g + 1) * D_STATE], bm_b[:, g * D_STATE:(g + 1) * D_STATE])
        for hh in range(2):
            h = 2 * g + hh
            seg = acum[:, h:h + 1] - acum_t[h:h + 1, :]
            decay = jnp.exp(jnp.where(causal, seg, -jnp.inf))
            m = (sc * decay * dt_t[h:h + 1, :]).astype(BF16)
            y = jnp.where(head == h, _dot(m, xs_b), y)
    return xs, xbc[:, SEG:2 * SEG], cm_b, dt, acum, alast, y


def _ssd_finish(y, xs, z, dskip, dng):
    y = (y + dskip * xs) * _silu(z)
    parts = []
    for g in range(2):
        yg = y[:, g * D_STATE:(g + 1) * D_STATE]
        parts.append(_rms(yg, dng[:, g * D_STATE:(g + 1) * D_STATE]))
    return jnp.concatenate(parts, axis=1)


def _finish_layer(x, mix_s, wout_ref, gpost):
    return x + _rms(_dot(mix_s[...], wout_ref[0]), gpost)


def _prompt_kernel(layer, rows,
                   sink_ref, x_ref, *refs):
    vec_refs, refs = refs[:len(VEC_NAMES)], refs[len(VEC_NAMES):]
    (win, wdt, wout, aws, awb, bcw, bpw, dcw,
     y_ref, kn_ref, vn_ref, cbn_ref, cdn_ref, ssm_ref,
     proj_s, glu_s, xbc_s, conv_s, kprev_s, vprev_s, h_s, mix_s, bias_s) = refs
    gpre, gpost, alng, alnb, bcb, blng, blnb, bpb, dcb, dtb, alog, dskip, dng = _layer_rows(layer, vec_refs)
    j = pl.program_id(1)
    n_chunks = rows // CH
    b_hist, d_hist = 32, 8

    @pl.when(j == 0)
    def _():
        glu_s[:, 0:b_hist, :] = jnp.zeros((SEG // LANES, b_hist, LANES), F32)
        xbc_s[:, 0:d_hist, :] = jnp.zeros((D_CONV_DIM // LANES, d_hist, LANES), F32)
        kprev_s[...] = jnp.zeros((CH, KV_W), BF16)
        vprev_s[...] = jnp.zeros((CH, KV_W), BF16)
        h_s[...] = jnp.zeros((N_HEADS * HEAD, D_STATE), F32)
        row2 = _iota((2 * CH, CH), 0) & (CH - 1)
        col2 = _iota((2 * CH, CH), 1)
        for kvh in range(2):
            slope, _ = _head_cols(sink_ref, layer, kvh, 2 * CH)
            bias_s[2 * kvh] = _attn_bias(row2 + CH - col2, col2 >= row2, slope)
            bias_s[2 * kvh + 1] = _attn_bias(row2 - col2, col2 <= row2, slope)

    @pl.when(j > 0)
    def _():
        glu_s[:, 0:b_hist, :] = glu_s[:, rows:rows + b_hist, :]
        xbc_s[:, 0:d_hist, :] = xbc_s[:, rows:rows + d_hist, :]

    def project(lo, hi):
        _project(x_ref[0, lo:hi], gpre[0], win, wdt, proj_s.at[lo:hi])

    def conv_inputs(lo, hi):
        for c in range(SEG // LANES):
            glu_s[c, b_hist + lo:b_hist + hi, :] = (
                proj_s[lo:hi, O_BV + c * LANES:O_BV + (c + 1) * LANES]
                * jax.nn.sigmoid(proj_s[lo:hi, O_BG + c * LANES:O_BG + (c + 1) * LANES]))
        for c in range(D_CONV_DIM // LANES):
            xbc_s[c, d_hist + lo:d_hist + hi, :] = proj_s[lo:hi, O_DX + c * LANES:O_DX + (c + 1) * LANES]

    def conv(buf, hist, n_tiles, w_ref, b_ref, n_taps, out_col, act, lo, hi):
        tile = 32
        for c in range(n_tiles):
            cs = slice(c * LANES, (c + 1) * LANES)
            for r0 in range(lo, hi, tile):
                first = r0 + hist - (n_taps - 1)
                acc = buf[c, pl.ds(first, tile, stride=1), :] * w_ref[0, 0:1, cs]
                for k in range(1, n_taps):
                    acc = acc + buf[c, pl.ds(first + k, tile, stride=1), :] * w_ref[0, k:k + 1, cs]
                conv_s[r0:r0 + tile, out_col + c * LANES:out_col + (c + 1) * LANES] = act(acc + b_ref[0, :, cs])

    def convs(lo, hi):
        conv(glu_s, b_hist, SEG // LANES, bcw, bcb, B_TAPS, 0, lambda a: a, lo, hi)
        conv(xbc_s, d_hist, D_CONV_DIM // LANES, dcw, dcb, D_TAPS, SEG, _silu, lo, hi)

    same, causal = _chunk_masks(CH)
    same_b, tri_b = _sum_matrices(same, causal)
    ws_masked = _masked_ws(aws, causal, CH)
    kv_carry = [kprev_s[...], vprev_s[...]]

    def mixers(c):
        rs = slice(c * CH, (c + 1) * CH)
        k_prev, v_prev = kv_carry

        ya, _ = _mixer_a(proj_s[rs, O_AU:O_AU + SEG], proj_s[rs, O_AV:O_AV + SEG], proj_s[rs, O_AZ:O_AZ + SEG],
                         alng[0], alnb[0], ws_masked, awb[0])
        mix_s[rs, 0:SEG] = ya.astype(BF16)

        yb = _mixer_b_post(conv_s[rs, 0:SEG], proj_s[rs, O_BZ:O_BZ + SEG], blng[0], blnb[0], bpw, bpb[0])
        mix_s[rs, SEG:2 * SEG] = yb.astype(BF16)

        q = _swap_inner_heads(proj_s[rs, O_CQ:O_CQ + SEG] * (HEAD ** -0.5))
        k_cur = proj_s[rs, O_CK:O_CK + KV_W].astype(BF16)
        v_cur = proj_s[rs, O_CV:O_CV + KV_W].astype(BF16)
        outs = []
        for kvh in range(2):
            _, sink = _head_cols(sink_ref, layer, kvh, 2 * CH)
            q_rows = _attn_q_rows(q, kvh)
            bias_prev = bias_s[2 * kvh]
            if c == 0:
                bias_prev = jnp.where(j > 0, bias_prev, -jnp.inf)
            outs.append(_attn([(_dot_nt(q_rows, k_prev), bias_prev, functools.partial(_dot, b=v_prev)),
                               (_dot_nt(q_rows, k_cur), bias_s[2 * kvh + 1], functools.partial(_dot, b=v_cur))],
                              sink))
        yc = _swap_inner_heads(_attn_merge(outs[0], outs[1])) * _silu(proj_s[rs, O_CZ:O_CZ + SEG])
        mix_s[rs, 2 * SEG:3 * SEG] = yc.astype(BF16)
        kv_carry[:] = [k_cur, v_cur]

        xs, bm, cm_b, dt, acum, alast, y = _ssd_intra(conv_s[rs, SEG:SEG + D_CONV_DIM],
                                                      proj_s[rs, O_DT:O_DT + DT_PAD], dtb[0], alog[0],
                                                      same_b, tri_b, causal)
        h_prev = h_s[...]
        h_prev_b = h_prev.astype(BF16)
        y_inter = jnp.concatenate(
            [_dot_nt(cm_b[:, g * D_STATE:(g + 1) * D_STATE], h_prev_b[g * 2 * HEAD:(g + 1) * 2 * HEAD])
             for g in range(2)], axis=1)
        y = y + y_inter * _lane_bcast_heads(jnp.exp(acum))
        wcol = jnp.exp(alast - acum) * dt
        carry = jnp.exp(alast)
        xs_t = xs.T.astype(BF16)
        for h in range(N_HEADS):
            g = h // 2
            wb_h = (bm[:, g * D_STATE:(g + 1) * D_STATE] * wcol[:, h:h + 1]).astype(BF16)
            hs = slice(h * HEAD, (h + 1) * HEAD)
            h_s[hs, :] = h_prev[hs] * carry[0:HEAD, h:h + 1] + _dot(xs_t[hs], wb_h)
        mix_s[rs, 3 * SEG:4 * SEG] = _ssd_finish(y, xs, proj_s[rs, O_DZ:O_DZ + SEG], dskip[0], dng[0]).astype(BF16)

    def finish(lo, hi):
        y_ref[0, lo:hi] = x_ref[0, lo:hi] + _rms(_dot(mix_s[lo:hi], wout[0]), gpost[0])

    blk = rows // PROMPT_SUBBLOCKS
    project(0, blk)
    for lo in range(0, rows, blk):
        hi = lo + blk
        conv_inputs(lo, hi)
        if hi < rows:
            project(hi, hi + blk)
        convs(lo, hi)
        for c in range(lo // CH, hi // CH):
            mixers(c)
        finish(lo, hi)
    kprev_s[...] = kv_carry[0]
    vprev_s[...] = kv_carry[1]

    @pl.when(j == pl.num_programs(1) - 1)
    def _():
        kn_ref[0] = proj_s[rows - CH:rows, O_CK:O_CK + KV_W]
        vn_ref[0] = proj_s[rows - CH:rows, O_CV:O_CV + KV_W]
        for c in range(SEG // LANES):
            cbn_ref[0, :, c * LANES:(c + 1) * LANES] = glu_s[c, b_hist + rows - (B_TAPS - 1):b_hist + rows, :]
        for c in range(D_CONV_DIM // LANES):
            cdn_ref[0, :, c * LANES:(c + 1) * LANES] = xbc_s[c, d_hist + rows - (D_TAPS - 1):d_hist + rows, :]
        ssm_ref[0] = h_s[...].reshape(N_HEADS, HEAD, D_STATE)


def _prompt_layer(layer, x, p, rows):
    batch, seq, _ = x.shape
    nb = seq // rows

    def wspec(a):
        blk = (1,) + a.shape[1:]
        zeros = (0,) * (a.ndim - 1)
        return pl.BlockSpec(blk, lambda b, j: (layer,) + zeros)

    vectors = [p[k] for k in VEC_NAMES]
    weights = [p[k] for k in ("win", "wdt", "wout", "aws", "awb_p", "bcw", "bpw", "dcw")]
    x_spec = pl.BlockSpec((1, rows, D_MODEL), lambda b, j: (b, j, 0))
    in_specs = ([pl.BlockSpec(memory_space=pltpu.SMEM), x_spec]
                + [pl.BlockSpec(a.shape, lambda b, j: (0, 0)) for a in vectors] + [wspec(a) for a in weights])
    out_shape = (jax.ShapeDtypeStruct((batch, seq, D_MODEL), F32),
                 jax.ShapeDtypeStruct((batch, CH, KV_W), F32),
                 jax.ShapeDtypeStruct((batch, CH, KV_W), F32),
                 jax.ShapeDtypeStruct((batch, B_TAPS - 1, SEG), F32),
                 jax.ShapeDtypeStruct((batch, D_TAPS - 1, D_CONV_DIM), F32),
                 jax.ShapeDtypeStruct((batch, N_HEADS, HEAD, D_STATE), F32))
    out_specs = (x_spec,
                 pl.BlockSpec((1, CH, KV_W), lambda b, j: (b, 0, 0)),
                 pl.BlockSpec((1, CH, KV_W), lambda b, j: (b, 0, 0)),
                 pl.BlockSpec((1, B_TAPS - 1, SEG), lambda b, j: (b, 0, 0)),
                 pl.BlockSpec((1, D_TAPS - 1, D_CONV_DIM), lambda b, j: (b, 0, 0)),
                 pl.BlockSpec((1, N_HEADS, HEAD, D_STATE), lambda b, j: (b, 0, 0, 0)))
    scratch = [pltpu.VMEM((rows, IN_PAD), F32),
               pltpu.VMEM((SEG // LANES, rows + 32, LANES), F32),
               pltpu.VMEM((D_CONV_DIM // LANES, rows + 8, LANES), F32),
               pltpu.VMEM((rows, SEG + D_CONV_DIM), F32),
               pltpu.VMEM((CH, KV_W), BF16),
               pltpu.VMEM((CH, KV_W), BF16),
               pltpu.VMEM((N_HEADS * HEAD, D_STATE), F32),
               pltpu.VMEM((rows, D_MODEL), BF16),
               pltpu.VMEM((4, 2 * CH, CH), F32)]
    return pl.pallas_call(
        functools.partial(_prompt_kernel, layer, rows),
        grid=(batch, nb),
        in_specs=in_specs, out_specs=out_specs, out_shape=out_shape, scratch_shapes=scratch,
        compiler_params=pltpu.CompilerParams(dimension_semantics=("arbitrary", "arbitrary"),
                                             vmem_limit_bytes=VMEM_LIMIT_BYTES),
        name=f"prompt_layer{layer}",
    )(p["sinks"], x, *vectors, *weights)


def _sample_kernel(group,
                   sink_ref, x_ref, kc_ref, vc_ref, cbh_ref, cdh_ref, ssm0_ref, *refs):
    vec_refs, refs = refs[:len(VEC_NAMES)], refs[len(VEC_NAMES):]
    (win, wdt, wout, aws, awb, bcw, bpw, dcw,
     y_ref, kn_ref, vn_ref, cbn_ref, cdn_ref, ssm_ref, avn_ref,
     xin_s, proj_s, new_s, conv_s, mix_s, bias_a_s, bias_b_s) = refs
    layer = pl.program_id(0)
    gpre, gpost, alng, alnb, bcb, blng, blnb, bpb, dcb, dtb, alog, dskip, dng = _layer_rows(layer, vec_refs)
    s = pl.program_id(1)
    rows = group * SAMPLE_SEQ
    row0 = pl.multiple_of(s * rows, rows)

    @pl.when(layer == 0)
    def _():
        xin_s[...] = x_ref[...]

    @pl.when(layer > 0)
    def _():
        xin_s[...] = y_ref[pl.ds(row0, rows), :]

    x = xin_s[...]
    _project(x, gpre[0], win, wdt, proj_s)
    same, causal = _chunk_masks(SAMPLE_SEQ)
    same_b, tri_b = _sum_matrices(same, causal)

    ya, v_rows = _mixer_a(proj_s[:, O_AU:O_AU + SEG], proj_s[:, O_AV:O_AV + SEG], proj_s[:, O_AZ:O_AZ + SEG],
                          alng[0], alnb[0], _masked_ws(aws, causal, SAMPLE_SEQ), awb[0])
    mix_s[:, 0:SEG] = ya.astype(BF16)
    avn_ref[0] = v_rows

    glu = proj_s[:, O_BV:O_BV + SEG] * jax.nn.sigmoid(proj_s[:, O_BG:O_BG + SEG])
    cbn_ref[0] = glu
    cdn_ref[0] = proj_s[:, O_DX:O_DX + D_CONV_DIM]

    def conv_tiles(hist_ref, new, w_ref, b_ref, n_taps, tile0, act):
        n_hist = n_taps - 1
        n_tiles = new.shape[1] // LANES
        for c in range(n_tiles):
            cs = slice(c * LANES, (c + 1) * LANES)
            new_s[c] = new[:, cs]
            slabs = [hist_ref[0, t, :, cs] for t in range(n_hist)]
            slabs += [new_s[c, pl.ds(t, group, stride=SAMPLE_SEQ), :] for t in range(SAMPLE_SEQ)]
            for t in range(SAMPLE_SEQ):
                acc = slabs[t] * w_ref[0, 0:1, cs]
                for k in range(1, n_taps):
                    acc = acc + slabs[t + k] * w_ref[0, k:k + 1, cs]
                conv_s[tile0 + c, pl.ds(t, group, stride=SAMPLE_SEQ), :] = act(acc + b_ref[0, :, cs])

    conv_tiles(cbh_ref, glu, bcw, bcb, B_TAPS, 0, lambda a: a)
    conv_tiles(cdh_ref, proj_s[:, O_DX:O_DX + D_CONV_DIM], dcw, dcb, D_TAPS, SEG // LANES, _silu)
    conv_b = jnp.concatenate([conv_s[c] for c in range(SEG // LANES)], axis=1)
    conv_d = jnp.concatenate([conv_s[c] for c in range(SEG // LANES, (SEG + D_CONV_DIM) // LANES)], axis=1)

    yb = _mixer_b_post(conv_b, proj_s[:, O_BZ:O_BZ + SEG], blng[0], blnb[0], bpw, bpb[0])
    mix_s[:, SEG:2 * SEG] = yb.astype(BF16)

    kn_ref[0] = proj_s[:, O_CK:O_CK + KV_W]
    vn_ref[0] = proj_s[:, O_CV:O_CV + KV_W]
    sub_rows = ATT_SUB * SAMPLE_SEQ
    n_past = ATT_SUB * CH

    @pl.when((layer == 0) & (s == 0))
    def _():
        row = _iota((2 * sub_rows, n_past), 0) & (sub_rows - 1)
        col = _iota((2 * sub_rows, n_past), 1)
        q_t = row & (SAMPLE_SEQ - 1)
        key_pos = col & (CH - 1)
        rel_a = q_t + CH - key_pos
        ok_a = ((col >> 7) == (row >> 3)) & (key_pos >= q_t)
        row_n = _iota((2 * sub_rows, sub_rows), 0) & (sub_rows - 1)
        col_n = _iota((2 * sub_rows, sub_rows), 1)
        rel_b = (row_n & (SAMPLE_SEQ - 1)) - (col_n & (SAMPLE_SEQ - 1))
        ok_b = ((col_n >> 3) == (row_n >> 3)) & (rel_b >= 0)
        for kvh in range(2):
            slope, _ = _head_cols(sink_ref, layer, kvh, 2 * sub_rows)
            bias_a_s[kvh] = _attn_bias(rel_a, ok_a, slope)
            bias_b_s[kvh] = _attn_bias(rel_b, ok_b, slope)

    for u in range(group // ATT_SUB):
        us = slice(u * sub_rows, (u + 1) * sub_rows)
        q = proj_s[us, O_CQ:O_CQ + SEG] * (HEAD ** -0.5)
        outs = []
        for kvh in range(2):
            hs = [slice((2 * kvh + i) * HEAD, (2 * kvh + i + 1) * HEAD) for i in range(2)]
            q_rows = jnp.concatenate([q[:, hs[0]], q[:, hs[1]]], axis=0).astype(BF16)
            k_t = jnp.concatenate([kc_ref[0, u * ATT_SUB + i, kvh] for i in range(ATT_SUB)], axis=1).astype(BF16)
            v_t = jnp.concatenate([vc_ref[0, u * ATT_SUB + i, kvh] for i in range(ATT_SUB)], axis=1).astype(BF16)
            k_new = proj_s[us, O_CK + kvh * HEAD:O_CK + (kvh + 1) * HEAD].astype(BF16)
            v_new = proj_s[us, O_CV + kvh * HEAD:O_CV + (kvh + 1) * HEAD].astype(BF16)
            _, sink = _head_cols(sink_ref, layer, kvh, 2 * sub_rows)
            o = _attn([(_dot(q_rows, k_t), bias_a_s[kvh], functools.partial(_dot_nt, b=v_t)),
                       (_dot_nt(q_rows, k_new), bias_b_s[kvh], functools.partial(_dot, b=v_new))], sink)
            outs += [o[:sub_rows], o[sub_rows:]]
        yc = jnp.concatenate(outs, axis=1)
        mix_s[us, 2 * SEG:3 * SEG] = (yc * _silu(proj_s[us, O_CZ:O_CZ + SEG])).astype(BF16)

    xs, bm, cm_b, dt, acum, alast, y = _ssd_intra(conv_d, proj_s[:, O_DT:O_DT + DT_PAD], dtb[0], alog[0],
                                                  same_b, tri_b, causal)
    state_rows = group * HEAD
    own = (_iota((state_rows, CH), 0) >> 6) == (_iota((state_rows, CH), 1) >> 3)
    wcol = jnp.exp(alast - acum) * dt
    carry = jnp.exp(alast)
    xs_t = xs.T
    y_inter_t = []
    for h in range(N_HEADS):
        g = h // 2
        gs = slice(g * D_STATE, (g + 1) * D_STATE)
        h0 = ssm0_ref[0, :, h].reshape(state_rows, D_STATE)
        full = jnp.where(own, _dot_nt(h0.astype(BF16), cm_b[:, gs]), 0.0)
        y_inter_t.append(jnp.sum(full.reshape(group, HEAD, CH), axis=0))
        xt_h = xs_t[h * HEAD:(h + 1) * HEAD]
        lhs = jnp.where(own, jnp.concatenate([xt_h] * group, axis=0), 0.0).astype(BF16)
        wb_h = (bm[:, gs] * wcol[:, h:h + 1]).astype(BF16)
        carry_h = jnp.broadcast_to(carry[:, h:h + 1], (CH, D_STATE)).reshape(group, SAMPLE_SEQ, D_STATE)
        carry_h = jnp.concatenate([carry_h] * (HEAD // SAMPLE_SEQ), axis=1).reshape(state_rows, D_STATE)
        ssm_ref[0, :, h] = (h0 * carry_h + _dot(lhs, wb_h)).reshape(group, HEAD, D_STATE)
    y_inter = jnp.concatenate(y_inter_t, axis=0).T
    y = y + y_inter * _lane_bcast_heads(jnp.exp(acum))
    mix_s[:, 3 * SEG:4 * SEG] = _ssd_finish(y, xs, proj_s[:, O_DZ:O_DZ + SEG], dskip[0], dng[0]).astype(BF16)

    y_ref[pl.ds(row0, rows), :] = _finish_layer(x, mix_s, wout, gpost[0])


def _sample_layers(x, kc, vc, cbh, cdh, ssm0, p, group):
    depth = p["win"].shape[0]
    n_seq = ssm0.shape[1]
    n_rows = n_seq * SAMPLE_SEQ
    rows = group * SAMPLE_SEQ
    ns = n_seq // group

    def wspec(a):
        blk = (1,) + a.shape[1:]
        zeros = (0,) * (a.ndim - 1)
        return pl.BlockSpec(blk, lambda i, s: (i,) + zeros)

    def rspec(width, per_seq):
        return pl.BlockSpec((1, group * per_seq, width), lambda i, s: (i, s, 0))

    vectors = [p[k] for k in VEC_NAMES]
    weights = [p[k] for k in ("win", "wdt", "wout", "aws", "awb_s", "bcw", "bpw", "dcw")]
    ssm_spec = pl.BlockSpec((1, group, N_HEADS, HEAD, D_STATE), lambda i, s: (i, s, 0, 0, 0))
    cache_spec = pl.BlockSpec((1, group, 2, HEAD, CH), lambda i, s: (i, s, 0, 0, 0))

    def hist_spec(n_hist, width):
        return pl.BlockSpec((1, n_hist, group, width), lambda i, s: (i, 0, s, 0))

    in_specs = ([pl.BlockSpec(memory_space=pltpu.SMEM),
                 pl.BlockSpec((rows, D_MODEL), lambda i, s: (s, 0)),
                 cache_spec, cache_spec,
                 hist_spec(B_TAPS - 1, SEG), hist_spec(D_TAPS - 1, D_CONV_DIM), ssm_spec]
                + [pl.BlockSpec(a.shape, lambda i, s: (0, 0)) for a in vectors] + [wspec(a) for a in weights])
    out_shape = (jax.ShapeDtypeStruct((n_rows, D_MODEL), F32),
                 jax.ShapeDtypeStruct((depth, n_rows, KV_W), F32),
                 jax.ShapeDtypeStruct((depth, n_rows, KV_W), F32),
                 jax.ShapeDtypeStruct((depth, n_rows, SEG), F32),
                 jax.ShapeDtypeStruct((depth, n_rows, D_CONV_DIM), F32),
                 jax.ShapeDtypeStruct((depth, n_seq, N_HEADS, HEAD, D_STATE), F32),
                 jax.ShapeDtypeStruct((depth, n_rows, SEG), F32))
    out_specs = (pl.BlockSpec((n_rows, D_MODEL), lambda i, s: (0, 0)),
                 rspec(KV_W, SAMPLE_SEQ), rspec(KV_W, SAMPLE_SEQ),
                 rspec(SEG, SAMPLE_SEQ), rspec(D_CONV_DIM, SAMPLE_SEQ), ssm_spec,
                 rspec(SEG, SAMPLE_SEQ))
    scratch = [pltpu.VMEM((rows, D_MODEL), F32),
               pltpu.VMEM((rows, IN_PAD), F32),
               pltpu.VMEM((D_CONV_DIM // LANES, rows, LANES), F32),
               pltpu.VMEM(((SEG + D_CONV_DIM) // LANES, rows, LANES), F32),
               pltpu.VMEM((rows, D_MODEL), BF16),
               pltpu.VMEM((2, 2 * ATT_SUB * SAMPLE_SEQ, ATT_SUB * CH), F32),
               pltpu.VMEM((2, 2 * ATT_SUB * SAMPLE_SEQ, ATT_SUB * SAMPLE_SEQ), F32)]
    return pl.pallas_call(
        functools.partial(_sample_kernel, group),
        grid=(depth, ns),
        in_specs=in_specs, out_specs=out_specs, out_shape=out_shape, scratch_shapes=scratch,
        compiler_params=pltpu.CompilerParams(dimension_semantics=("arbitrary", "arbitrary"),
                                             vmem_limit_bytes=VMEM_LIMIT_BYTES),
        name="sample_layers",
    )(p["sinks"], x, kc, vc, cbh, cdh, ssm0, *vectors, *weights)


def _prepare_params(norm_pre, norm_post, w_in, w_out, a_ln_g, a_ln_b, a_ws, a_wb, b_conv_w, b_conv_b,
                    b_ln_g, b_ln_b, b_pw_w, b_pw_b, c_sinks, d_conv_w, d_conv_b, d_dt_bias, d_a_log,
                    d_skip, d_norm_g, group):
    depth = w_in.shape[0]
    win = w_in.astype(BF16)
    wdt = jnp.pad(w_in[:, :, O_DT:], ((0, 0), (0, 0), (0, DT_PAD - N_HEADS))).astype(BF16)
    wout = w_out.astype(BF16)

    def per_head(a):
        return jnp.repeat(a, HEAD, axis=-1)

    def lane_pad(a):
        return jnp.pad(a, ((0, 0), (0, DT_PAD - N_HEADS)))

    reps = CH // SAMPLE_SEQ
    return {
        "gpre": norm_pre, "gpost": norm_post, "alng": a_ln_g, "alnb": a_ln_b, "bcb": b_conv_b,
        "blng": b_ln_g, "blnb": b_ln_b, "bpb": b_pw_b, "dcb": d_conv_b, "dtb": lane_pad(d_dt_bias),
        "alog": lane_pad(d_a_log), "dskip": per_head(d_skip), "dng": d_norm_g,
        "sinks": c_sinks, "win": win, "wdt": wdt, "wout": wout,
        "aws": a_ws,
        "awb_p": per_head(jnp.swapaxes(a_wb, 1, 2)),
        "awb_s": per_head(jnp.tile(jnp.swapaxes(a_wb[:, :, :SAMPLE_SEQ], 1, 2), (1, reps, 1))),
        "bcw": b_conv_w, "bpw": b_pw_w.astype(BF16), "dcw": d_conv_w,
    }


PROMPT_ROWS = 512
PROMPT_SUBBLOCKS = 2
SAMPLE_GROUP = CH // SAMPLE_SEQ


def kernel(x_prompt, x_sample, cache_win_k, cache_win_v, state_conv_b, state_conv_d, state_ssm, norm_pre, norm_post, w_in, w_out, a_ln_g, a_ln_b, a_ws, a_wb, b_conv_w, b_conv_b, b_ln_g, b_ln_b, b_pw_w, b_pw_b, c_sinks, d_conv_w, d_conv_b, d_dt_bias, d_a_log, d_skip, d_norm_g):
    depth = w_in.shape[0]
    batch, seq, _ = x_prompt.shape
    n_seq, dec_seq, _ = x_sample.shape
    assert dec_seq == SAMPLE_SEQ and cache_win_k.shape[2] == CH and seq % PROMPT_ROWS == 0
    assert n_seq % SAMPLE_GROUP == 0
    p = _prepare_params(norm_pre, norm_post, w_in, w_out, a_ln_g, a_ln_b, a_ws, a_wb, b_conv_w, b_conv_b,
                        b_ln_g, b_ln_b, b_pw_w, b_pw_b, c_sinks, d_conv_w, d_conv_b, d_dt_bias, d_a_log,
                        d_skip, d_norm_g, SAMPLE_GROUP)

    xp = x_prompt
    per_layer = []
    for i in range(depth):
        xp, *state = _prompt_layer(i, xp, p, PROMPT_ROWS)
        per_layer.append(state)
    pk, pv, pcb, pcd, pssm = (jnp.stack(list(a)) for a in zip(*per_layer))

    n_rows = n_seq * SAMPLE_SEQ
    ys, sk, sv, scb, scd, sssm, sav = _sample_layers(
        x_sample.reshape(n_rows, D_MODEL),
        jnp.transpose(cache_win_k, (0, 1, 3, 4, 2)), jnp.transpose(cache_win_v, (0, 1, 3, 4, 2)),
        jnp.swapaxes(state_conv_b, 1, 2), jnp.swapaxes(state_conv_d, 1, 2),
        state_ssm, p, SAMPLE_GROUP)

    kv_p = (depth, batch, CH, 2, HEAD)
    kv_s = (depth, n_seq, SAMPLE_SEQ, 2, HEAD)
    return (xp, ys.reshape(n_seq, SAMPLE_SEQ, D_MODEL),
            pk.reshape(kv_p), sk.reshape(kv_s), pv.reshape(kv_p), sv.reshape(kv_s),
            pcb, scb.reshape(depth, n_seq, SAMPLE_SEQ, SEG),
            pcd, scd.reshape(depth, n_seq, SAMPLE_SEQ, D_CONV_DIM),
            pssm, sssm, sav.reshape(depth, n_seq, SAMPLE_SEQ, SEG))
```

```python
import functools

import jax
import jax.numpy as jnp
from jax import lax
from jax.experimental import pallas as pl
from jax.experimental.pallas import tpu as pltpu

F32 = jnp.float32
BF16 = jnp.bfloat16
EPS = 1e-6

D_MODEL = 1024
LANES = 128
CH = 128
HEAD = 64
N_HEADS = 4
SEG = 256
KV_W = 128
B_TAPS = 31
D_TAPS = 4
D_CONV_DIM = 768
D_STATE = 128
DT_PAD = 128
SAMPLE_SEQ = 8
ATT_SUB = 8

O_AU, O_AV, O_AZ = 0, 256, 512
O_BV, O_BG, O_BZ = 768, 1024, 1280
O_CQ, O_CK, O_CV, O_CZ = 1536, 1792, 1920, 2048
O_DZ, O_DX, O_DT = 2304, 2560, 3328
IN_RAW = 3332
IN_PAD = 3456
PROJ_TILE = 1152
HELD_TILE = 512

ALIBI = tuple(float(2.0 ** (-8.0 * (i + 1) / N_HEADS)) for i in range(N_HEADS))
VMEM_LIMIT_BYTES = 56 * 1024 * 1024


VEC_NAMES = ("gpre", "gpost", "alng", "alnb", "bcb", "blng", "blnb", "bpb", "dcb", "dtb", "alog", "dskip", "dng")


def _layer_rows(layer, refs):
    return [r[pl.ds(layer, 1), :][None] for r in refs]


def _iota(shape, dim):
    return lax.broadcasted_iota(jnp.int32, shape, dim)


def _dot(a, b):
    return jnp.dot(a, b, preferred_element_type=F32)


def _dot_nt(a, b):
    return lax.dot_general(a, b, (((1,), (1,)), ((), ())), preferred_element_type=F32)


def _rms(x, g):
    return x * lax.rsqrt(jnp.mean(x * x, axis=-1, keepdims=True) + EPS) * g


def _ln(x, g, b):
    mu = jnp.mean(x, axis=-1, keepdims=True)
    xc = x - mu
    return xc * lax.rsqrt(jnp.mean(xc * xc, axis=-1, keepdims=True) + EPS) * g + b


def _silu(x):
    return x * jax.nn.sigmoid(x)


def _softplus(x):
    return jnp.maximum(x, 0.0) + jnp.log1p(jnp.exp(-jnp.abs(x)))


def _hi_lo(x):
    hi = x.astype(BF16)
    return hi, (x - hi.astype(F32)).astype(BF16)


def _head_of_lane(width):
    return _iota((CH, width), 1) >> 6


def _lane_bcast_heads(cols, width=SEG):
    head = _head_of_lane(width)
    out = jnp.zeros((CH, width), F32)
    for h in range(N_HEADS):
        out = jnp.where(head == h, cols[:, h:h + 1], out)
    return out


def _chunk_masks(seq_rows):
    shift = seq_rows.bit_length() - 1
    r = _iota((CH, CH), 0)
    c = _iota((CH, CH), 1)
    same = (r >> shift) == (c >> shift)
    causal = same & (c <= r)
    return same, causal


def _after(v):
    return jnp.minimum(jnp.abs(v[0:1, 0:1]), 0.0).astype(BF16)


def _project(x, gpre, win_ref, wdt_ref, proj, after=None):
    h = _rms(x, gpre).astype(BF16)
    tile = PROJ_TILE if after is None else HELD_TILE
    bounds = [(n0, min(n0 + tile, O_DT)) for n0 in range(0, O_DT, tile)]
    if after is None:
        after = [None] * (len(bounds) + 1)
    assert len(after) == len(bounds) + 1
    for (n0, n1), zero in zip(bounds, after):
        proj[:, n0:n1] = _dot(h if zero is None else h + zero, win_ref[0, :, n0:n1])
    proj[:, O_DT:O_DT + DT_PAD] = _dot(h if after[-1] is None else h + after[-1], wdt_ref[0])


def _masked_ws(ws_ref, causal, seq_rows):
    if seq_rows < CH:
        r = _iota((CH, CH), 0)
        c = _iota((CH, CH), 1)
        lane_tile = jnp.where((r < seq_rows) & ((c & (seq_rows - 1)) == r), 1.0, 0.0).astype(BF16)
    out = []
    for h in range(N_HEADS):
        if seq_rows == CH:
            w = ws_ref[0, h]
        else:
            top = _dot(ws_ref[0, h, 0:seq_rows, :].astype(BF16), lane_tile)
            w = jnp.concatenate([top] * (CH // seq_rows), axis=0)
        out.append(jnp.where(causal, w, 0.0).astype(BF16))
    return out


def _mixer_a(u_raw, v_raw, z, ln_g, ln_b, ws_masked, wb):
    u = jax.nn.gelu(u_raw)
    v = _ln(jax.nn.gelu(v_raw), ln_g, ln_b)
    vb = v.astype(BF16)
    head = _head_of_lane(SEG)
    s = jnp.zeros((CH, SEG), F32)
    for h in range(N_HEADS):
        s = jnp.where(head == h, _dot(ws_masked[h], vb), s)
    return u * (s + wb) * _silu(z), v


def _mixer_b_post(c, z, ln_g, ln_b, pw_ref, pw_b):
    y = _dot(_silu(_ln(c, ln_g, ln_b)).astype(BF16), pw_ref[0]) + pw_b
    return y * _silu(z)


def _attn_bias(rel, ok, slope):
    return jnp.where(ok, -slope * rel.astype(F32), -jnp.inf)


def _attn(parts, sink):
    scores = [qk + bias for qk, bias, _ in parts]
    same_shape = all(s.shape == scores[0].shape for s in scores)

    def row_reduce(xs, combine, reduce):
        if same_shape:
            xs = [functools.reduce(combine, xs)]
        return functools.reduce(combine, [reduce(x, axis=-1, keepdims=True) for x in xs])

    m = jnp.maximum(sink, row_reduce(scores, jnp.maximum, jnp.max))
    probs = [jnp.exp(s - m) for s in scores]
    den = jnp.exp(sink - m) + row_reduce(probs, jnp.add, jnp.sum)
    out = functools.reduce(jnp.add, [pv(p.astype(BF16)) for p, (_, _, pv) in zip(probs, parts)])
    return out / den


def _swap_inner_heads(x):
    t0, t1 = x[:, 0:KV_W], x[:, KV_W:2 * KV_W]
    lo = _iota(t0.shape, 1) < HEAD
    return jnp.concatenate([jnp.where(lo, t0, pltpu.roll(t1, HEAD, 1)),
                            jnp.where(lo, pltpu.roll(t0, HEAD, 1), t1)], axis=1)


def _attn_q_rows(q, kvh):
    lane = _iota((q.shape[0], KV_W), 1)
    keep = (lane < HEAD) if kvh == 0 else (lane >= HEAD)
    t0 = jnp.where(keep, q[:, 0:KV_W], 0.0)
    t1 = jnp.where(keep, q[:, KV_W:2 * KV_W], 0.0)
    return jnp.concatenate([t0, t1], axis=0).astype(BF16)


def _attn_merge(o0, o1):
    half = o0.shape[0] // 2
    lo = _iota((half, KV_W), 1) < HEAD
    return jnp.concatenate([jnp.where(lo, o0[:half], o1[:half]),
                            jnp.where(lo, o0[half:], o1[half:])], axis=1)


def _head_cols(sink_ref, layer, kvh, rows):
    half = rows // 2
    first = _iota((rows, 1), 0) < half
    slope = jnp.where(first, ALIBI[2 * kvh], ALIBI[2 * kvh + 1])
    sink = jnp.where(first, sink_ref[layer, 2 * kvh], sink_ref[layer, 2 * kvh + 1])
    return slope, sink


def _sum_matrices(same, causal):
    return jnp.where(same, 1.0, 0.0).astype(BF16), jnp.where(causal, 1.0, 0.0).astype(BF16)


def _ssd_intra(xbc, dt_raw, dtb, alog, same_b, tri_b, causal):
    xs = xbc[:, 0:SEG]
    bm_b = xbc[:, SEG:2 * SEG].astype(BF16)
    cm_b = xbc[:, 2 * SEG:3 * SEG].astype(BF16)
    xs_b = xs.astype(BF16)
    dt = _softplus(dt_raw + dtb)
    da_hi, da_lo = _hi_lo(dt * (-jnp.exp(alog)))
    acum = _dot(tri_b, da_hi) + _dot(tri_b, da_lo)
    alast = _dot(same_b, da_hi) + _dot(same_b, da_lo)
    acum_t = acum.T
    dt_t = dt.T
    head = _head_of_lane(SEG)
    y = jnp.zeros((CH, SEG), F32)
    for g in range(2):
        sc = _dot_nt(cm_b[:, g * D_STATE:(g + 1) * D_STATE], bm_b[:, g * D_STATE:(g + 1) * D_STATE])
        for hh in range(2):
            h = 2 * g + hh
            seg = acum[:, h:h + 1] - acum_t[h:h + 1, :]
            decay = jnp.exp(jnp.where(causal, seg, -jnp.inf))
            m = (sc * decay * dt_t[h:h + 1, :]).astype(BF16)
            y = jnp.where(head == h, _dot(m, xs_b), y)
    return xs, xbc[:, SEG:2 * SEG], cm_b, dt, acum, alast, y


def _ssd_finish(y, xs, z, dskip, dng):
    y = (y + dskip * xs) * _silu(z)
    parts = []
    for g in range(2):
        yg = y[:, g * D_STATE:(g + 1) * D_STATE]
        parts.append(_rms(yg, dng[:, g * D_STATE:(g + 1) * D_STATE]))
    return jnp.concatenate(parts, axis=1)


def _finish_layer(x, mix_s, wout_ref, gpost):
    return x + _rms(_dot(mix_s[...], wout_ref[0]), gpost)


def _prompt_kernel(layer, rows,
                   sink_ref, x_ref, *refs):
    vec_refs, refs = refs[:len(VEC_NAMES)], refs[len(VEC_NAMES):]
    (win, wdt, wout, aws, awb, bcw, bpw, dcw,
     y_ref, kn_ref, vn_ref, cbn_ref, cdn_ref, ssm_ref,
     proj_s, glu_s, xbc_s, conv_s, kprev_s, vprev_s, h_s, mix_s, bias_s) = refs
    gpre, gpost, alng, alnb, bcb, blng, blnb, bpb, dcb, dtb, alog, dskip, dng = _layer_rows(layer, vec_refs)
    j = pl.program_id(1)
    n_chunks = rows // CH
    b_hist, d_hist = 32, 8

    @pl.when(j == 0)
    def _():
        glu_s[:, 0:b_hist, :] = jnp.zeros((SEG // LANES, b_hist, LANES), F32)
        xbc_s[:, 0:d_hist, :] = jnp.zeros((D_CONV_DIM // LANES, d_hist, LANES), F32)
        kprev_s[...] = jnp.zeros((CH, KV_W), BF16)
        vprev_s[...] = jnp.zeros((CH, KV_W), BF16)
        h_s[...] = jnp.zeros((N_HEADS * HEAD, D_STATE), F32)
        row2 = _iota((2 * CH, CH), 0) & (CH - 1)
        col2 = _iota((2 * CH, CH), 1)
        for kvh in range(2):
            slope, _ = _head_cols(sink_ref, layer, kvh, 2 * CH)
            bias_s[2 * kvh] = _attn_bias(row2 + CH - col2, col2 >= row2, slope)
            bias_s[2 * kvh + 1] = _attn_bias(row2 - col2, col2 <= row2, slope)

    @pl.when(j > 0)
    def _():
        glu_s[:, 0:b_hist, :] = glu_s[:, rows:rows + b_hist, :]
        xbc_s[:, 0:d_hist, :] = xbc_s[:, rows:rows + d_hist, :]

    def project(lo, hi, after=None):
        _project(x_ref[0, lo:hi], gpre[0], win, wdt, proj_s.at[lo:hi], after)

    def conv_inputs(lo, hi):
        for c in range(SEG // LANES):
            glu_s[c, b_hist + lo:b_hist + hi, :] = (
                proj_s[lo:hi, O_BV + c * LANES:O_BV + (c + 1) * LANES]
                * jax.nn.sigmoid(proj_s[lo:hi, O_BG + c * LANES:O_BG + (c + 1) * LANES]))
        for c in range(D_CONV_DIM // LANES):
            xbc_s[c, d_hist + lo:d_hist + hi, :] = proj_s[lo:hi, O_DX + c * LANES:O_DX + (c + 1) * LANES]

    def conv(buf, hist, n_tiles, w_ref, b_ref, n_taps, out_col, act, lo, hi):
        tile = 32
        for c in range(n_tiles):
            cs = slice(c * LANES, (c + 1) * LANES)
            for r0 in range(lo, hi, tile):
                first = r0 + hist - (n_taps - 1)
                acc = buf[c, pl.ds(first, tile, stride=1), :] * w_ref[0, 0:1, cs]
                for k in range(1, n_taps):
                    acc = acc + buf[c, pl.ds(first + k, tile, stride=1), :] * w_ref[0, k:k + 1, cs]
                conv_s[r0:r0 + tile, out_col + c * LANES:out_col + (c + 1) * LANES] = act(acc + b_ref[0, :, cs])

    def convs(lo, hi):
        conv(glu_s, b_hist, SEG // LANES, bcw, bcb, B_TAPS, 0, lambda a: a, lo, hi)
        conv(xbc_s, d_hist, D_CONV_DIM // LANES, dcw, dcb, D_TAPS, SEG, _silu, lo, hi)

    same, causal = _chunk_masks(CH)
    same_b, tri_b = _sum_matrices(same, causal)
    ws_masked = _masked_ws(aws, causal, CH)
    kv_carry = [kprev_s[...], vprev_s[...]]

    def mixers(c):
        rs = slice(c * CH, (c + 1) * CH)
        k_prev, v_prev = kv_carry

        ya, _ = _mixer_a(proj_s[rs, O_AU:O_AU + SEG], proj_s[rs, O_AV:O_AV + SEG], proj_s[rs, O_AZ:O_AZ + SEG],
                         alng[0], alnb[0], ws_masked, awb[0])
        mix_s[rs, 0:SEG] = ya.astype(BF16)

        yb = _mixer_b_post(conv_s[rs, 0:SEG], proj_s[rs, O_BZ:O_BZ + SEG], blng[0], blnb[0], bpw, bpb[0])
        mix_s[rs, SEG:2 * SEG] = yb.astype(BF16)

        q = _swap_inner_heads(proj_s[rs, O_CQ:O_CQ + SEG] * (HEAD ** -0.5))
        k_cur = proj_s[rs, O_CK:O_CK + KV_W].astype(BF16)
        v_cur = proj_s[rs, O_CV:O_CV + KV_W].astype(BF16)
        outs = []
        for kvh in range(2):
            _, sink = _head_cols(sink_ref, layer, kvh, 2 * CH)
            q_rows = _attn_q_rows(q, kvh)
            bias_prev = bias_s[2 * kvh]
            if c == 0:
                bias_prev = jnp.where(j > 0, bias_prev, -jnp.inf)
            outs.append(_attn([(_dot_nt(q_rows, k_prev), bias_prev, functools.partial(_dot, b=v_prev)),
                               (_dot_nt(q_rows, k_cur), bias_s[2 * kvh + 1], functools.partial(_dot, b=v_cur))],
                              sink))
        yc = _swap_inner_heads(_attn_merge(outs[0], outs[1])) * _silu(proj_s[rs, O_CZ:O_CZ + SEG])
        mix_s[rs, 2 * SEG:3 * SEG] = yc.astype(BF16)
        kv_carry[:] = [k_cur, v_cur]

        xs, bm, cm_b, dt, acum, alast, y = _ssd_intra(conv_s[rs, SEG:SEG + D_CONV_DIM],
                                                      proj_s[rs, O_DT:O_DT + DT_PAD], dtb[0], alog[0],
                                                      same_b, tri_b, causal)
        h_prev = h_s[...]
        h_prev_b = h_prev.astype(BF16)
        y_inter = jnp.concatenate(
            [_dot_nt(cm_b[:, g * D_STATE:(g + 1) * D_STATE], h_prev_b[g * 2 * HEAD:(g + 1) * 2 * HEAD])
             for g in range(2)], axis=1)
        y = y + y_inter * _lane_bcast_heads(jnp.exp(acum))
        wcol = jnp.exp(alast - acum) * dt
        carry = jnp.exp(alast)
        xs_t = xs.T.astype(BF16)
        for h in range(N_HEADS):
            g = h // 2
            wb_h = (bm[:, g * D_STATE:(g + 1) * D_STATE] * wcol[:, h:h + 1]).astype(BF16)
            hs = slice(h * HEAD, (h + 1) * HEAD)
            h_s[hs, :] = h_prev[hs] * carry[0:HEAD, h:h + 1] + _dot(xs_t[hs], wb_h)
        yd = _ssd_finish(y, xs, proj_s[rs, O_DZ:O_DZ + SEG], dskip[0], dng[0])
        mix_s[rs, 3 * SEG:4 * SEG] = yd.astype(BF16)
        return [_after(v) for v in (ya, yb, yc, yd)]

    def finish(lo, hi):
        y_ref[0, lo:hi] = x_ref[0, lo:hi] + _rms(_dot(mix_s[lo:hi], wout[0]), gpost[0])

    blk = rows // PROMPT_SUBBLOCKS
    assert blk == 2 * CH
    project(0, blk)
    for lo in range(0, rows, blk):
        hi = lo + blk
        conv_inputs(lo, hi)
        convs(lo, hi)
        start = _after(conv_s[lo:lo + 8, 0:LANES])
        first = mixers(lo // CH)
        second = mixers(lo // CH + 1)
        if hi < rows:
            project(hi, hi + blk, [start, start, first[0], first[1], first[2], first[3], second[0], second[1]])
        finish(lo, hi)
    kprev_s[...] = kv_carry[0]
    vprev_s[...] = kv_carry[1]

    @pl.when(j == pl.num_programs(1) - 1)
    def _():
        kn_ref[0] = proj_s[rows - CH:rows, O_CK:O_CK + KV_W]
        vn_ref[0] = proj_s[rows - CH:rows, O_CV:O_CV + KV_W]
        for c in range(SEG // LANES):
            cbn_ref[0, :, c * LANES:(c + 1) * LANES] = glu_s[c, b_hist + rows - (B_TAPS - 1):b_hist + rows, :]
        for c in range(D_CONV_DIM // LANES):
            cdn_ref[0, :, c * LANES:(c + 1) * LANES] = xbc_s[c, d_hist + rows - (D_TAPS - 1):d_hist + rows, :]
        ssm_ref[0] = h_s[...].reshape(N_HEADS, HEAD, D_STATE)


def _prompt_layer(layer, x, p, rows):
    batch, seq, _ = x.shape
    nb = seq // rows

    def wspec(a):
        blk = (1,) + a.shape[1:]
        zeros = (0,) * (a.ndim - 1)
        return pl.BlockSpec(blk, lambda b, j: (layer,) + zeros)

    vectors = [p[k] for k in VEC_NAMES]
    weights = [p[k] for k in ("win", "wdt", "wout", "aws", "awb_p", "bcw", "bpw", "dcw")]
    x_spec = pl.BlockSpec((1, rows, D_MODEL), lambda b, j: (b, j, 0))
    in_specs = ([pl.BlockSpec(memory_space=pltpu.SMEM), x_spec]
                + [pl.BlockSpec(a.shape, lambda b, j: (0, 0)) for a in vectors] + [wspec(a) for a in weights])
    out_shape = (jax.ShapeDtypeStruct((batch, seq, D_MODEL), F32),
                 jax.ShapeDtypeStruct((batch, CH, KV_W), F32),
                 jax.ShapeDtypeStruct((batch, CH, KV_W), F32),
                 jax.ShapeDtypeStruct((batch, B_TAPS - 1, SEG), F32),
                 jax.ShapeDtypeStruct((batch, D_TAPS - 1, D_CONV_DIM), F32),
                 jax.ShapeDtypeStruct((batch, N_HEADS, HEAD, D_STATE), F32))
    out_specs = (x_spec,
                 pl.BlockSpec((1, CH, KV_W), lambda b, j: (b, 0, 0)),
                 pl.BlockSpec((1, CH, KV_W), lambda b, j: (b, 0, 0)),
                 pl.BlockSpec((1, B_TAPS - 1, SEG), lambda b, j: (b, 0, 0)),
                 pl.BlockSpec((1, D_TAPS - 1, D_CONV_DIM), lambda b, j: (b, 0, 0)),
                 pl.BlockSpec((1, N_HEADS, HEAD, D_STATE), lambda b, j: (b, 0, 0, 0)))
    scratch = [pltpu.VMEM((rows, IN_PAD), F32),
               pltpu.VMEM((SEG // LANES, rows + 32, LANES), F32),
               pltpu.VMEM((D_CONV_DIM // LANES, rows + 8, LANES), F32),
               pltpu.VMEM((rows, SEG + D_CONV_DIM), F32),
               pltpu.VMEM((CH, KV_W), BF16),
               pltpu.VMEM((CH, KV_W), BF16),
               pltpu.VMEM((N_HEADS * HEAD, D_STATE), F32),
               pltpu.VMEM((rows, D_MODEL), BF16),
               pltpu.VMEM((4, 2 * CH, CH), F32)]
    return pl.pallas_call(
        functools.partial(_prompt_kernel, layer, rows),
        grid=(batch, nb),
        in_specs=in_specs, out_specs=out_specs, out_shape=out_shape, scratch_shapes=scratch,
        compiler_params=pltpu.CompilerParams(dimension_semantics=("arbitrary", "arbitrary"),
                                             vmem_limit_bytes=VMEM_LIMIT_BYTES),
        name=f"prompt_layer{layer}",
    )(p["sinks"], x, *vectors, *weights)


def _sample_kernel(group,
                   sink_ref, x_ref, kc_ref, vc_ref, cbh_ref, cdh_ref, ssm0_ref, *refs):
    vec_refs, refs = refs[:len(VEC_NAMES)], refs[len(VEC_NAMES):]
    (win, wdt, wout, aws, awb, bcw, bpw, dcw,
     y_ref, kn_ref, vn_ref, cbn_ref, cdn_ref, ssm_ref, avn_ref,
     xin_s, proj_s, new_s, conv_s, mix_s, bias_a_s, bias_b_s) = refs
    layer = pl.program_id(0)
    gpre, gpost, alng, alnb, bcb, blng, blnb, bpb, dcb, dtb, alog, dskip, dng = _layer_rows(layer, vec_refs)
    s = pl.program_id(1)
    rows = group * SAMPLE_SEQ
    row0 = pl.multiple_of(s * rows, rows)

    @pl.when(layer == 0)
    def _():
        xin_s[...] = x_ref[...]

    @pl.when(layer > 0)
    def _():
        xin_s[...] = y_ref[pl.ds(row0, rows), :]

    x = xin_s[...]
    _project(x, gpre[0], win, wdt, proj_s)
    same, causal = _chunk_masks(SAMPLE_SEQ)
    same_b, tri_b = _sum_matrices(same, causal)

    ya, v_rows = _mixer_a(proj_s[:, O_AU:O_AU + SEG], proj_s[:, O_AV:O_AV + SEG], proj_s[:, O_AZ:O_AZ + SEG],
                          alng[0], alnb[0], _masked_ws(aws, causal, SAMPLE_SEQ), awb[0])
    mix_s[:, 0:SEG] = ya.astype(BF16)
    avn_ref[0] = v_rows

    glu = proj_s[:, O_BV:O_BV + SEG] * jax.nn.sigmoid(proj_s[:, O_BG:O_BG + SEG])
    cbn_ref[0] = glu
    cdn_ref[0] = proj_s[:, O_DX:O_DX + D_CONV_DIM]

    def conv_tiles(hist_ref, new, w_ref, b_ref, n_taps, tile0, act):
        n_hist = n_taps - 1
        n_tiles = new.shape[1] // LANES
        for c in range(n_tiles):
            cs = slice(c * LANES, (c + 1) * LANES)
            new_s[c] = new[:, cs]
            slabs = [hist_ref[0, t, :, cs] for t in range(n_hist)]
            slabs += [new_s[c, pl.ds(t, group, stride=SAMPLE_SEQ), :] for t in range(SAMPLE_SEQ)]
            for t in range(SAMPLE_SEQ):
                acc = slabs[t] * w_ref[0, 0:1, cs]
                for k in range(1, n_taps):
                    acc = acc + slabs[t + k] * w_ref[0, k:k + 1, cs]
                conv_s[tile0 + c, pl.ds(t, group, stride=SAMPLE_SEQ), :] = act(acc + b_ref[0, :, cs])

    conv_tiles(cbh_ref, glu, bcw, bcb, B_TAPS, 0, lambda a: a)
    conv_tiles(cdh_ref, proj_s[:, O_DX:O_DX + D_CONV_DIM], dcw, dcb, D_TAPS, SEG // LANES, _silu)
    conv_b = jnp.concatenate([conv_s[c] for c in range(SEG // LANES)], axis=1)
    conv_d = jnp.concatenate([conv_s[c] for c in range(SEG // LANES, (SEG + D_CONV_DIM) // LANES)], axis=1)

    yb = _mixer_b_post(conv_b, proj_s[:, O_BZ:O_BZ + SEG], blng[0], blnb[0], bpw, bpb[0])
    mix_s[:, SEG:2 * SEG] = yb.astype(BF16)

    kn_ref[0] = proj_s[:, O_CK:O_CK + KV_W]
    vn_ref[0] = proj_s[:, O_CV:O_CV + KV_W]
    sub_rows = ATT_SUB * SAMPLE_SEQ
    n_past = ATT_SUB * CH

    @pl.when((layer == 0) & (s == 0))
    def _():
        row = _iota((2 * sub_rows, n_past), 0) & (sub_rows - 1)
        col = _iota((2 * sub_rows, n_past), 1)
        q_t = row & (SAMPLE_SEQ - 1)
        key_pos = col & (CH - 1)
        rel_a = q_t + CH - key_pos
        ok_a = ((col >> 7) == (row >> 3)) & (key_pos >= q_t)
        row_n = _iota((2 * sub_rows, sub_rows), 0) & (sub_rows - 1)
        col_n = _iota((2 * sub_rows, sub_rows), 1)
        rel_b = (row_n & (SAMPLE_SEQ - 1)) - (col_n & (SAMPLE_SEQ - 1))
        ok_b = ((col_n >> 3) == (row_n >> 3)) & (rel_b >= 0)
        for kvh in range(2):
            slope, _ = _head_cols(sink_ref, layer, kvh, 2 * sub_rows)
            bias_a_s[kvh] = _attn_bias(rel_a, ok_a, slope)
            bias_b_s[kvh] = _attn_bias(rel_b, ok_b, slope)

    for u in range(group // ATT_SUB):
        us = slice(u * sub_rows, (u + 1) * sub_rows)
        q = proj_s[us, O_CQ:O_CQ + SEG] * (HEAD ** -0.5)
        outs = []
        for kvh in range(2):
            hs = [slice((2 * kvh + i) * HEAD, (2 * kvh + i + 1) * HEAD) for i in range(2)]
            q_rows = jnp.concatenate([q[:, hs[0]], q[:, hs[1]]], axis=0).astype(BF16)
            k_t = jnp.concatenate([kc_ref[0, u * ATT_SUB + i, kvh] for i in range(ATT_SUB)], axis=1).astype(BF16)
            v_t = jnp.concatenate([vc_ref[0, u * ATT_SUB + i, kvh] for i in range(ATT_SUB)], axis=1).astype(BF16)
            k_new = proj_s[us, O_CK + kvh * HEAD:O_CK + (kvh + 1) * HEAD].astype(BF16)
            v_new = proj_s[us, O_CV + kvh * HEAD:O_CV + (kvh + 1) * HEAD].astype(BF16)
            _, sink = _head_cols(sink_ref, layer, kvh, 2 * sub_rows)
            o = _attn([(_dot(q_rows, k_t), bias_a_s[kvh], functools.partial(_dot_nt, b=v_t)),
                       (_dot_nt(q_rows, k_new), bias_b_s[kvh], functools.partial(_dot, b=v_new))], sink)
            outs += [o[:sub_rows], o[sub_rows:]]
        yc = jnp.concatenate(outs, axis=1)
        mix_s[us, 2 * SEG:3 * SEG] = (yc * _silu(proj_s[us, O_CZ:O_CZ + SEG])).astype(BF16)

    xs, bm, cm_b, dt, acum, alast, y = _ssd_intra(conv_d, proj_s[:, O_DT:O_DT + DT_PAD], dtb[0], alog[0],
                                                  same_b, tri_b, causal)
    state_rows = group * HEAD
    own = (_iota((state_rows, CH), 0) >> 6) == (_iota((state_rows, CH), 1) >> 3)
    wcol = jnp.exp(alast - acum) * dt
    carry = jnp.exp(alast)
    xs_t = xs.T
    y_inter_t = []
    for h in range(N_HEADS):
        g = h // 2
        gs = slice(g * D_STATE, (g + 1) * D_STATE)
        h0 = ssm0_ref[0, :, h].reshape(state_rows, D_STATE)
        full = jnp.where(own, _dot_nt(h0.astype(BF16), cm_b[:, gs]), 0.0)
        y_inter_t.append(jnp.sum(full.reshape(group, HEAD, CH), axis=0))
        xt_h = xs_t[h * HEAD:(h + 1) * HEAD]
        lhs = jnp.where(own, jnp.concatenate([xt_h] * group, axis=0), 0.0).astype(BF16)
        wb_h = (bm[:, gs] * wcol[:, h:h + 1]).astype(BF16)
        carry_h = jnp.broadcast_to(carry[:, h:h + 1], (CH, D_STATE)).reshape(group, SAMPLE_SEQ, D_STATE)
        carry_h = jnp.concatenate([carry_h] * (HEAD // SAMPLE_SEQ), axis=1).reshape(state_rows, D_STATE)
        ssm_ref[0, :, h] = (h0 * carry_h + _dot(lhs, wb_h)).reshape(group, HEAD, D_STATE)
    y_inter = jnp.concatenate(y_inter_t, axis=0).T
    y = y + y_inter * _lane_bcast_heads(jnp.exp(acum))
    mix_s[:, 3 * SEG:4 * SEG] = _ssd_finish(y, xs, proj_s[:, O_DZ:O_DZ + SEG], dskip[0], dng[0]).astype(BF16)

    y_ref[pl.ds(row0, rows), :] = _finish_layer(x, mix_s, wout, gpost[0])


def _sample_layers(x, kc, vc, cbh, cdh, ssm0, p, group):
    depth = p["win"].shape[0]
    n_seq = ssm0.shape[1]
    n_rows = n_seq * SAMPLE_SEQ
    rows = group * SAMPLE_SEQ
    ns = n_seq // group

    def wspec(a):
        blk = (1,) + a.shape[1:]
        zeros = (0,) * (a.ndim - 1)
        return pl.BlockSpec(blk, lambda i, s: (i,) + zeros)

    def rspec(width, per_seq):
        return pl.BlockSpec((1, group * per_seq, width), lambda i, s: (i, s, 0))

    vectors = [p[k] for k in VEC_NAMES]
    weights = [p[k] for k in ("win", "wdt", "wout", "aws", "awb_s", "bcw", "bpw", "dcw")]
    ssm_spec = pl.BlockSpec((1, group, N_HEADS, HEAD, D_STATE), lambda i, s: (i, s, 0, 0, 0))
    cache_spec = pl.BlockSpec((1, group, 2, HEAD, CH), lambda i, s: (i, s, 0, 0, 0))

    def hist_spec(n_hist, width):
        return pl.BlockSpec((1, n_hist, group, width), lambda i, s: (i, 0, s, 0))

    in_specs = ([pl.BlockSpec(memory_space=pltpu.SMEM),
                 pl.BlockSpec((rows, D_MODEL), lambda i, s: (s, 0)),
                 cache_spec, cache_spec,
                 hist_spec(B_TAPS - 1, SEG), hist_spec(D_TAPS - 1, D_CONV_DIM), ssm_spec]
                + [pl.BlockSpec(a.shape, lambda i, s: (0, 0)) for a in vectors] + [wspec(a) for a in weights])
    out_shape = (jax.ShapeDtypeStruct((n_rows, D_MODEL), F32),
                 jax.ShapeDtypeStruct((depth, n_rows, KV_W), F32),
                 jax.ShapeDtypeStruct((depth, n_rows, KV_W), F32),
                 jax.ShapeDtypeStruct((depth, n_rows, SEG), F32),
                 jax.ShapeDtypeStruct((depth, n_rows, D_CONV_DIM), F32),
                 jax.ShapeDtypeStruct((depth, n_seq, N_HEADS, HEAD, D_STATE), F32),
                 jax.ShapeDtypeStruct((depth, n_rows, SEG), F32))
    out_specs = (pl.BlockSpec((n_rows, D_MODEL), lambda i, s: (0, 0)),
                 rspec(KV_W, SAMPLE_SEQ), rspec(KV_W, SAMPLE_SEQ),
                 rspec(SEG, SAMPLE_SEQ), rspec(D_CONV_DIM, SAMPLE_SEQ), ssm_spec,
                 rspec(SEG, SAMPLE_SEQ))
    scratch = [pltpu.VMEM((rows, D_MODEL), F32),
               pltpu.VMEM((rows, IN_PAD), F32),
               pltpu.VMEM((D_CONV_DIM // LANES, rows, LANES), F32),
               pltpu.VMEM(((SEG + D_CONV_DIM) // LANES, rows, LANES), F32),
               pltpu.VMEM((rows, D_MODEL), BF16),
               pltpu.VMEM((2, 2 * ATT_SUB * SAMPLE_SEQ, ATT_SUB * CH), F32),
               pltpu.VMEM((2, 2 * ATT_SUB * SAMPLE_SEQ, ATT_SUB * SAMPLE_SEQ), F32)]
    return pl.pallas_call(
        functools.partial(_sample_kernel, group),
        grid=(depth, ns),
        in_specs=in_specs, out_specs=out_specs, out_shape=out_shape, scratch_shapes=scratch,
        compiler_params=pltpu.CompilerParams(dimension_semantics=("arbitrary", "arbitrary"),
                                             vmem_limit_bytes=VMEM_LIMIT_BYTES),
        name="sample_layers",
    )(p["sinks"], x, kc, vc, cbh, cdh, ssm0, *vectors, *weights)


def _prepare_params(norm_pre, norm_post, w_in, w_out, a_ln_g, a_ln_b, a_ws, a_wb, b_conv_w, b_conv_b,
                    b_ln_g, b_ln_b, b_pw_w, b_pw_b, c_sinks, d_conv_w, d_conv_b, d_dt_bias, d_a_log,
                    d_skip, d_norm_g, group):
    depth = w_in.shape[0]
    win = w_in.astype(BF16)
    wdt = jnp.pad(w_in[:, :, O_DT:], ((0, 0), (0, 0), (0, DT_PAD - N_HEADS))).astype(BF16)
    wout = w_out.astype(BF16)

    def per_head(a):
        return jnp.repeat(a, HEAD, axis=-1)

    def lane_pad(a):
        return jnp.pad(a, ((0, 0), (0, DT_PAD - N_HEADS)))

    reps = CH // SAMPLE_SEQ
    return {
        "gpre": norm_pre, "gpost": norm_post, "alng": a_ln_g, "alnb": a_ln_b, "bcb": b_conv_b,
        "blng": b_ln_g, "blnb": b_ln_b, "bpb": b_pw_b, "dcb": d_conv_b, "dtb": lane_pad(d_dt_bias),
        "alog": lane_pad(d_a_log), "dskip": per_head(d_skip), "dng": d_norm_g,
        "sinks": c_sinks, "win": win, "wdt": wdt, "wout": wout,
        "aws": a_ws,
        "awb_p": per_head(jnp.swapaxes(a_wb, 1, 2)),
        "awb_s": per_head(jnp.tile(jnp.swapaxes(a_wb[:, :, :SAMPLE_SEQ], 1, 2), (1, reps, 1))),
        "bcw": b_conv_w, "bpw": b_pw_w.astype(BF16), "dcw": d_conv_w,
    }


PROMPT_ROWS = 512
PROMPT_SUBBLOCKS = 2
SAMPLE_GROUP = CH // SAMPLE_SEQ


def kernel(x_prompt, x_sample, cache_win_k, cache_win_v, state_conv_b, state_conv_d, state_ssm, norm_pre, norm_post, w_in, w_out, a_ln_g, a_ln_b, a_ws, a_wb, b_conv_w, b_conv_b, b_ln_g, b_ln_b, b_pw_w, b_pw_b, c_sinks, d_conv_w, d_conv_b, d_dt_bias, d_a_log, d_skip, d_norm_g):
    depth = w_in.shape[0]
    batch, seq, _ = x_prompt.shape
    n_seq, dec_seq, _ = x_sample.shape
    assert dec_seq == SAMPLE_SEQ and cache_win_k.shape[2] == CH and seq % PROMPT_ROWS == 0
    assert n_seq % SAMPLE_GROUP == 0
    p = _prepare_params(norm_pre, norm_post, w_in, w_out, a_ln_g, a_ln_b, a_ws, a_wb, b_conv_w, b_conv_b,
                        b_ln_g, b_ln_b, b_pw_w, b_pw_b, c_sinks, d_conv_w, d_conv_b, d_dt_bias, d_a_log,
                        d_skip, d_norm_g, SAMPLE_GROUP)

    xp = x_prompt
    per_layer = []
    for i in range(depth):
        xp, *state = _prompt_layer(i, xp, p, PROMPT_ROWS)
        per_layer.append(state)
    pk, pv, pcb, pcd, pssm = (jnp.stack(list(a)) for a in zip(*per_layer))

    n_rows = n_seq * SAMPLE_SEQ
    ys, sk, sv, scb, scd, sssm, sav = _sample_layers(
        x_sample.reshape(n_rows, D_MODEL),
        jnp.transpose(cache_win_k, (0, 1, 3, 4, 2)), jnp.transpose(cache_win_v, (0, 1, 3, 4, 2)),
        jnp.swapaxes(state_conv_b, 1, 2), jnp.swapaxes(state_conv_d, 1, 2),
        state_ssm, p, SAMPLE_GROUP)

    kv_p = (depth, batch, CH, 2, HEAD)
    kv_s = (depth, n_seq, SAMPLE_SEQ, 2, HEAD)
    return (xp, ys.reshape(n_seq, SAMPLE_SEQ, D_MODEL),
            pk.reshape(kv_p), sk.reshape(kv_s), pv.reshape(kv_p), sv.reshape(kv_s),
            pcb, scb.reshape(depth, n_seq, SAMPLE_SEQ, SEG),
            pcd, scd.reshape(depth, n_seq, SAMPLE_SEQ, D_CONV_DIM),
            pssm, sssm, sav.reshape(depth, n_seq, SAMPLE_SEQ, SEG))
```

```python
import functools

import jax
import jax.numpy as jnp
from jax import lax
from jax.experimental import pallas as pl
from jax.experimental.pallas import tpu as pltpu

F32 = jnp.float32
BF16 = jnp.bfloat16
EPS = 1e-6

D_MODEL = 1024
LANES = 128
CH = 128
HEAD = 64
N_HEADS = 4
SEG = 256
KV_W = 128
B_TAPS = 31
D_TAPS = 4
D_CONV_DIM = 768
D_STATE = 128
DT_PAD = 128
SAMPLE_SEQ = 8
ATT_SUB = 8

O_AU, O_AV, O_AZ = 0, 256, 512
O_BV, O_BG, O_BZ = 768, 1024, 1280
O_CQ, O_CK, O_CV, O_CZ = 1536, 1792, 1920, 2048
O_DZ, O_DX, O_DT = 2304, 2560, 3328
IN_RAW = 3332
IN_PAD = 3456
PROJ_TILE = 1152
HELD_TILE = 512

ALIBI = tuple(float(2.0 ** (-8.0 * (i + 1) / N_HEADS)) for i in range(N_HEADS))
VMEM_LIMIT_BYTES = 56 * 1024 * 1024


VEC_NAMES = ("gpre", "gpost", "alng", "alnb", "bcb", "blng", "blnb", "bpb", "dcb", "dtb", "alog", "dskip", "dng")


def _layer_rows(layer, refs):
    return [r[pl.ds(layer, 1), :][None] for r in refs]


def _iota(shape, dim):
    return lax.broadcasted_iota(jnp.int32, shape, dim)


def _dot(a, b):
    return jnp.dot(a, b, preferred_element_type=F32)


def _dot_nt(a, b):
    return lax.dot_general(a, b, (((1,), (1,)), ((), ())), preferred_element_type=F32)


def _rms(x, g):
    return x * lax.rsqrt(jnp.mean(x * x, axis=-1, keepdims=True) + EPS) * g


def _ln(x, g, b):
    mu = jnp.mean(x, axis=-1, keepdims=True)
    xc = x - mu
    return xc * lax.rsqrt(jnp.mean(xc * xc, axis=-1, keepdims=True) + EPS) * g + b


def _silu(x):
    return x * jax.nn.sigmoid(x)


def _softplus(x):
    return jnp.maximum(x, 0.0) + jnp.log1p(jnp.exp(-jnp.abs(x)))


def _hi_lo(x):
    hi = x.astype(BF16)
    return hi, (x - hi.astype(F32)).astype(BF16)


def _head_of_lane(width):
    return _iota((CH, width), 1) >> 6


def _lane_bcast_heads(cols, width=SEG):
    head = _head_of_lane(width)
    out = jnp.zeros((CH, width), F32)
    for h in range(N_HEADS):
        out = jnp.where(head == h, cols[:, h:h + 1], out)
    return out


def _chunk_masks(seq_rows):
    shift = seq_rows.bit_length() - 1
    r = _iota((CH, CH), 0)
    c = _iota((CH, CH), 1)
    same = (r >> shift) == (c >> shift)
    causal = same & (c <= r)
    return same, causal


def _after(v):
    return jnp.minimum(jnp.abs(v[0:1, 0:1]), 0.0).astype(BF16)


def _project(x, gpre, win_ref, wdt_ref, proj, after=None):
    h = _rms(x, gpre).astype(BF16)
    tile = PROJ_TILE if after is None else HELD_TILE
    bounds = [(n0, min(n0 + tile, O_DT)) for n0 in range(0, O_DT, tile)]
    if after is None:
        after = [None] * (len(bounds) + 1)
    assert len(after) == len(bounds) + 1
    for (n0, n1), zero in zip(bounds, after):
        proj[:, n0:n1] = _dot(h if zero is None else h + zero, win_ref[0, :, n0:n1])
    proj[:, O_DT:O_DT + DT_PAD] = _dot(h if after[-1] is None else h + after[-1], wdt_ref[0])


def _masked_ws(ws_ref, causal, seq_rows):
    if seq_rows < CH:
        r = _iota((CH, CH), 0)
        c = _iota((CH, CH), 1)
        lane_tile = jnp.where((r < seq_rows) & ((c & (seq_rows - 1)) == r), 1.0, 0.0).astype(BF16)
    out = []
    for h in range(N_HEADS):
        if seq_rows == CH:
            w = ws_ref[0, h]
        else:
            top = _dot(ws_ref[0, h, 0:seq_rows, :].astype(BF16), lane_tile)
            w = jnp.concatenate([top] * (CH // seq_rows), axis=0)
        out.append(jnp.where(causal, w, 0.0).astype(BF16))
    return out


def _mixer_a(u_raw, v_raw, z, ln_g, ln_b, ws_masked, wb):
    u = jax.nn.gelu(u_raw)
    v = _ln(jax.nn.gelu(v_raw), ln_g, ln_b)
    vb = v.astype(BF16)
    head = _head_of_lane(SEG)
    s = jnp.zeros((CH, SEG), F32)
    for h in range(N_HEADS):
        s = jnp.where(head == h, _dot(ws_masked[h], vb), s)
    return u * (s + wb) * _silu(z), v


def _mixer_b_post(c, z, ln_g, ln_b, pw_ref, pw_b):
    y = _dot(_silu(_ln(c, ln_g, ln_b)).astype(BF16), pw_ref[0]) + pw_b
    return y * _silu(z)


def _attn_bias(rel, ok, slope):
    return jnp.where(ok, -slope * rel.astype(F32), -jnp.inf)


def _attn(parts, sink):
    scores = [qk + bias for qk, bias, _ in parts]
    same_shape = all(s.shape == scores[0].shape for s in scores)

    def row_reduce(xs, combine, reduce):
        if same_shape:
            xs = [functools.reduce(combine, xs)]
        return functools.reduce(combine, [reduce(x, axis=-1, keepdims=True) for x in xs])

    m = jnp.maximum(sink, row_reduce(scores, jnp.maximum, jnp.max))
    probs = [jnp.exp(s - m) for s in scores]
    den = jnp.exp(sink - m) + row_reduce(probs, jnp.add, jnp.sum)
    out = functools.reduce(jnp.add, [pv(p.astype(BF16)) for p, (_, _, pv) in zip(probs, parts)])
    return out / den


def _swap_inner_heads(x):
    t0, t1 = x[:, 0:KV_W], x[:, KV_W:2 * KV_W]
    lo = _iota(t0.shape, 1) < HEAD
    return jnp.concatenate([jnp.where(lo, t0, pltpu.roll(t1, HEAD, 1)),
                            jnp.where(lo, pltpu.roll(t0, HEAD, 1), t1)], axis=1)


def _attn_q_rows(q, kvh):
    lane = _iota((q.shape[0], KV_W), 1)
    keep = (lane < HEAD) if kvh == 0 else (lane >= HEAD)
    t0 = jnp.where(keep, q[:, 0:KV_W], 0.0)
    t1 = jnp.where(keep, q[:, KV_W:2 * KV_W], 0.0)
    return jnp.concatenate([t0, t1], axis=0).astype(BF16)


def _attn_merge(o0, o1):
    half = o0.shape[0] // 2
    lo = _iota((half, KV_W), 1) < HEAD
    return jnp.concatenate([jnp.where(lo, o0[:half], o1[:half]),
                            jnp.where(lo, o0[half:], o1[half:])], axis=1)


def _head_cols(sink_ref, layer, kvh, rows):
    half = rows // 2
    first = _iota((rows, 1), 0) < half
    slope = jnp.where(first, ALIBI[2 * kvh], ALIBI[2 * kvh + 1])
    sink = jnp.where(first, sink_ref[layer, 2 * kvh], sink_ref[layer, 2 * kvh + 1])
    return slope, sink


def _sum_matrices(same, causal):
    return jnp.where(same, 1.0, 0.0).astype(BF16), jnp.where(causal, 1.0, 0.0).astype(BF16)


def _ssd_intra(xbc, dt_raw, dtb, alog, same_b, tri_b, causal):
    xs = xbc[:, 0:SEG]
    bm_b = xbc[:, SEG:2 * SEG].astype(BF16)
    cm_b = xbc[:, 2 * SEG:3 * SEG].astype(BF16)
    xs_b = xs.astype(BF16)
    dt = _softplus(dt_raw + dtb)
    da_hi, da_lo = _hi_lo(dt * (-jnp.exp(alog)))
    acum = _dot(tri_b, da_hi) + _dot(tri_b, da_lo)
    alast = _dot(same_b, da_hi) + _dot(same_b, da_lo)
    acum_t = acum.T
    dt_t = dt.T
    head = _head_of_lane(SEG)
    y = jnp.zeros((CH, SEG), F32)
    for g in range(2):
        sc = _dot_nt(cm_b[:, g * D_STATE:(g + 1) * D_STATE], bm_b[:, g * D_STATE:(g + 1) * D_STATE])
        for hh in range(2):
            h = 2 * g + hh
            seg = acum[:, h:h + 1] - acum_t[h:h + 1, :]
            decay = jnp.exp(jnp.where(causal, seg, -jnp.inf))
            m = (sc * decay * dt_t[h:h + 1, :]).astype(BF16)
            y = jnp.where(head == h, _dot(m, xs_b), y)
    return xs, xbc[:, SEG:2 * SEG], cm_b, dt, acum, alast, y


def _ssd_finish(y, xs, z, dskip, dng):
    y = (y + dskip * xs) * _silu(z)
    parts = []
    for g in range(2):
        yg = y[:, g * D_STATE:(g + 1) * D_STATE]
        parts.append(_rms(yg, dng[:, g * D_STATE:(g + 1) * D_STATE]))
    return jnp.concatenate(parts, axis=1)


def _finish_layer(x, mix_s, wout_ref, gpost):
    return x + _rms(_dot(mix_s[...], wout_ref[0]), gpost)


def _prompt_kernel(layer, rows,
                   sink_ref, x_ref, *refs):
    vec_refs, refs = refs[:len(VEC_NAMES)], refs[len(VEC_NAMES):]
    (win, wdt, wout, aws, awb, bcw, bpw, dcw,
     y_ref, kn_ref, vn_ref, cbn_ref, cdn_ref, ssm_ref,
     proj_s, glu_s, xbc_s, conv_s, kprev_s, vprev_s, h_s, mix_s, bias_s) = refs
    gpre, gpost, alng, alnb, bcb, blng, blnb, bpb, dcb, dtb, alog, dskip, dng = _layer_rows(layer, vec_refs)
    j = pl.program_id(1)
    n_chunks = rows // CH
    b_hist, d_hist = 32, 8

    @pl.when(j == 0)
    def _():
        glu_s[:, 0:b_hist, :] = jnp.zeros((SEG // LANES, b_hist, LANES), F32)
        xbc_s[:, 0:d_hist, :] = jnp.zeros((D_CONV_DIM // LANES, d_hist, LANES), F32)
        kprev_s[...] = jnp.zeros((CH, KV_W), BF16)
        vprev_s[...] = jnp.zeros((CH, KV_W), BF16)
        h_s[...] = jnp.zeros((N_HEADS * HEAD, D_STATE), F32)
        row2 = _iota((2 * CH, CH), 0) & (CH - 1)
        col2 = _iota((2 * CH, CH), 1)
        for kvh in range(2):
            slope, _ = _head_cols(sink_ref, layer, kvh, 2 * CH)
            bias_s[2 * kvh] = _attn_bias(row2 + CH - col2, col2 >= row2, slope)
            bias_s[2 * kvh + 1] = _attn_bias(row2 - col2, col2 <= row2, slope)

    @pl.when(j > 0)
    def _():
        glu_s[:, 0:b_hist, :] = glu_s[:, rows:rows + b_hist, :]
        xbc_s[:, 0:d_hist, :] = xbc_s[:, rows:rows + d_hist, :]

    def project(lo, hi, after=None):
        _project(x_ref[0, lo:hi], gpre[0], win, wdt, proj_s.at[lo:hi], after)

    def conv_inputs(lo, hi):
        for c in range(SEG // LANES):
            glu_s[c, b_hist + lo:b_hist + hi, :] = (
                proj_s[lo:hi, O_BV + c * LANES:O_BV + (c + 1) * LANES]
                * jax.nn.sigmoid(proj_s[lo:hi, O_BG + c * LANES:O_BG + (c + 1) * LANES]))
        for c in range(D_CONV_DIM // LANES):
            xbc_s[c, d_hist + lo:d_hist + hi, :] = proj_s[lo:hi, O_DX + c * LANES:O_DX + (c + 1) * LANES]

    def conv(buf, hist, n_tiles, w_ref, b_ref, n_taps, out_col, act, lo, hi):
        tile = 32
        for c in range(n_tiles):
            cs = slice(c * LANES, (c + 1) * LANES)
            for r0 in range(lo, hi, tile):
                first = r0 + hist - (n_taps - 1)
                acc = buf[c, pl.ds(first, tile, stride=1), :] * w_ref[0, 0:1, cs]
                for k in range(1, n_taps):
                    acc = acc + buf[c, pl.ds(first + k, tile, stride=1), :] * w_ref[0, k:k + 1, cs]
                conv_s[r0:r0 + tile, out_col + c * LANES:out_col + (c + 1) * LANES] = act(acc + b_ref[0, :, cs])

    def convs(lo, hi):
        conv(glu_s, b_hist, SEG // LANES, bcw, bcb, B_TAPS, 0, lambda a: a, lo, hi)
        conv(xbc_s, d_hist, D_CONV_DIM // LANES, dcw, dcb, D_TAPS, SEG, _silu, lo, hi)

    same, causal = _chunk_masks(CH)
    same_b, tri_b = _sum_matrices(same, causal)
    ws_masked = _masked_ws(aws, causal, CH)
    kv_carry = [kprev_s[...], vprev_s[...]]

    def mixers(c):
        rs = slice(c * CH, (c + 1) * CH)
        k_prev, v_prev = kv_carry

        ya, _ = _mixer_a(proj_s[rs, O_AU:O_AU + SEG], proj_s[rs, O_AV:O_AV + SEG], proj_s[rs, O_AZ:O_AZ + SEG],
                         alng[0], alnb[0], ws_masked, awb[0])
        mix_s[rs, 0:SEG] = ya.astype(BF16)

        yb = _mixer_b_post(conv_s[rs, 0:SEG], proj_s[rs, O_BZ:O_BZ + SEG], blng[0], blnb[0], bpw, bpb[0])
        mix_s[rs, SEG:2 * SEG] = yb.astype(BF16)

        q = _swap_inner_heads(proj_s[rs, O_CQ:O_CQ + SEG] * (HEAD ** -0.5))
        k_cur = proj_s[rs, O_CK:O_CK + KV_W].astype(BF16)
        v_cur = proj_s[rs, O_CV:O_CV + KV_W].astype(BF16)
        outs = []
        for kvh in range(2):
            _, sink = _head_cols(sink_ref, layer, kvh, 2 * CH)
            q_rows = _attn_q_rows(q, kvh)
            bias_prev = bias_s[2 * kvh]
            if c == 0:
                bias_prev = jnp.where(j > 0, bias_prev, -jnp.inf)
            outs.append(_attn([(_dot_nt(q_rows, k_prev), bias_prev, functools.partial(_dot, b=v_prev)),
                               (_dot_nt(q_rows, k_cur), bias_s[2 * kvh + 1], functools.partial(_dot, b=v_cur))],
                              sink))
        yc = _swap_inner_heads(_attn_merge(outs[0], outs[1])) * _silu(proj_s[rs, O_CZ:O_CZ + SEG])
        mix_s[rs, 2 * SEG:3 * SEG] = yc.astype(BF16)
        kv_carry[:] = [k_cur, v_cur]

        xs, bm, cm_b, dt, acum, alast, y = _ssd_intra(conv_s[rs, SEG:SEG + D_CONV_DIM],
                                                      proj_s[rs, O_DT:O_DT + DT_PAD], dtb[0], alog[0],
                                                      same_b, tri_b, causal)
        h_prev = h_s[...]
        h_prev_b = h_prev.astype(BF16)
        y_inter = jnp.concatenate(
            [_dot_nt(cm_b[:, g * D_STATE:(g + 1) * D_STATE], h_prev_b[g * 2 * HEAD:(g + 1) * 2 * HEAD])
             for g in range(2)], axis=1)
        y = y + y_inter * _lane_bcast_heads(jnp.exp(acum))
        wcol = jnp.exp(alast - acum) * dt
        carry = jnp.exp(alast)
        xs_t = xs.T.astype(BF16)
        for h in range(N_HEADS):
            g = h // 2
            wb_h = (bm[:, g * D_STATE:(g + 1) * D_STATE] * wcol[:, h:h + 1]).astype(BF16)
            hs = slice(h * HEAD, (h + 1) * HEAD)
            h_s[hs, :] = h_prev[hs] * carry[0:HEAD, h:h + 1] + _dot(xs_t[hs], wb_h)
        yd = _ssd_finish(y, xs, proj_s[rs, O_DZ:O_DZ + SEG], dskip[0], dng[0])
        mix_s[rs, 3 * SEG:4 * SEG] = yd.astype(BF16)
        return [_after(v) for v in (ya, yb, yc, yd)]

    def finish(lo, hi):
        y_ref[0, lo:hi] = x_ref[0, lo:hi] + _rms(_dot(mix_s[lo:hi], wout[0]), gpost[0])

    blk = rows // PROMPT_SUBBLOCKS
    assert blk == 2 * CH
    project(0, blk)
    for lo in range(0, rows, blk):
        hi = lo + blk
        conv_inputs(lo, hi)
        convs(lo, hi)
        start = _after(conv_s[lo:lo + 8, 0:LANES])
        first = mixers(lo // CH)
        second = mixers(lo // CH + 1)
        if hi < rows:
            project(hi, hi + blk, [start, start, first[0], first[1], first[2], first[3], second[0], second[1]])
        finish(lo, hi)
    kprev_s[...] = kv_carry[0]
    vprev_s[...] = kv_carry[1]

    @pl.when(j == pl.num_programs(1) - 1)
    def _():
        kn_ref[0] = proj_s[rows - CH:rows, O_CK:O_CK + KV_W]
        vn_ref[0] = proj_s[rows - CH:rows, O_CV:O_CV + KV_W]
        for c in range(SEG // LANES):
            cbn_ref[0, :, c * LANES:(c + 1) * LANES] = glu_s[c, b_hist + rows - (B_TAPS - 1):b_hist + rows, :]
        for c in range(D_CONV_DIM // LANES):
            cdn_ref[0, :, c * LANES:(c + 1) * LANES] = xbc_s[c, d_hist + rows - (D_TAPS - 1):d_hist + rows, :]
        ssm_ref[0] = h_s[...].reshape(N_HEADS, HEAD, D_STATE)


def _prompt_layer(layer, x, p, rows):
    batch, seq, _ = x.shape
    nb = seq // rows

    def wspec(a):
        blk = (1,) + a.shape[1:]
        zeros = (0,) * (a.ndim - 1)
        return pl.BlockSpec(blk, lambda b, j: (layer,) + zeros)

    vectors = [p[k] for k in VEC_NAMES]
    weights = [p[k] for k in ("win", "wdt", "wout", "aws", "awb_p", "bcw", "bpw", "dcw")]
    x_spec = pl.BlockSpec((1, rows, D_MODEL), lambda b, j: (b, j, 0))
    in_specs = ([pl.BlockSpec(memory_space=pltpu.SMEM), x_spec]
                + [pl.BlockSpec(a.shape, lambda b, j: (0, 0)) for a in vectors] + [wspec(a) for a in weights])
    out_shape = (jax.ShapeDtypeStruct((batch, seq, D_MODEL), F32),
                 jax.ShapeDtypeStruct((batch, CH, KV_W), F32),
                 jax.ShapeDtypeStruct((batch, CH, KV_W), F32),
                 jax.ShapeDtypeStruct((batch, B_TAPS - 1, SEG), F32),
                 jax.ShapeDtypeStruct((batch, D_TAPS - 1, D_CONV_DIM), F32),
                 jax.ShapeDtypeStruct((batch, N_HEADS, HEAD, D_STATE), F32))
    out_specs = (x_spec,
                 pl.BlockSpec((1, CH, KV_W), lambda b, j: (b, 0, 0)),
                 pl.BlockSpec((1, CH, KV_W), lambda b, j: (b, 0, 0)),
                 pl.BlockSpec((1, B_TAPS - 1, SEG), lambda b, j: (b, 0, 0)),
                 pl.BlockSpec((1, D_TAPS - 1, D_CONV_DIM), lambda b, j: (b, 0, 0)),
                 pl.BlockSpec((1, N_HEADS, HEAD, D_STATE), lambda b, j: (b, 0, 0, 0)))
    scratch = [pltpu.VMEM((rows, IN_PAD), F32),
               pltpu.VMEM((SEG // LANES, rows + 32, LANES), F32),
               pltpu.VMEM((D_CONV_DIM // LANES, rows + 8, LANES), F32),
               pltpu.VMEM((rows, SEG + D_CONV_DIM), F32),
               pltpu.VMEM((CH, KV_W), BF16),
               pltpu.VMEM((CH, KV_W), BF16),
               pltpu.VMEM((N_HEADS * HEAD, D_STATE), F32),
               pltpu.VMEM((rows, D_MODEL), BF16),
               pltpu.VMEM((4, 2 * CH, CH), F32)]
    return pl.pallas_call(
        functools.partial(_prompt_kernel, layer, rows),
        grid=(batch, nb),
        in_specs=in_specs, out_specs=out_specs, out_shape=out_shape, scratch_shapes=scratch,
        compiler_params=pltpu.CompilerParams(dimension_semantics=("arbitrary", "arbitrary"),
                                             vmem_limit_bytes=VMEM_LIMIT_BYTES),
        name=f"prompt_layer{layer}",
    )(p["sinks"], x, *vectors, *weights)


def _sample_kernel(group,
                   sink_ref, x_ref, kc_ref, vc_ref, cbh_ref, cdh_ref, ssm0_ref, *refs):
    vec_refs, refs = refs[:len(VEC_NAMES)], refs[len(VEC_NAMES):]
    (win, wdt, wout, aws, awb, bcw, bpw, dcw,
     y_ref, kn_ref, vn_ref, cbn_ref, cdn_ref, ssm_ref, avn_ref,
     xin_s, proj_s, new_s, conv_s, mix_s, bias_a_s, bias_b_s) = refs
    layer = pl.program_id(0)
    gpre, gpost, alng, alnb, bcb, blng, blnb, bpb, dcb, dtb, alog, dskip, dng = _layer_rows(layer, vec_refs)
    s = pl.program_id(1)
    rows = group * SAMPLE_SEQ
    row0 = pl.multiple_of(s * rows, rows)

    @pl.when(layer == 0)
    def _():
        xin_s[...] = x_ref[...]

    @pl.when(layer > 0)
    def _():
        xin_s[...] = y_ref[pl.ds(row0, rows), :]

    x = xin_s[...]
    _project(x, gpre[0], win, wdt, proj_s)
    same, causal = _chunk_masks(SAMPLE_SEQ)
    same_b, tri_b = _sum_matrices(same, causal)

    ya, v_rows = _mixer_a(proj_s[:, O_AU:O_AU + SEG], proj_s[:, O_AV:O_AV + SEG], proj_s[:, O_AZ:O_AZ + SEG],
                          alng[0], alnb[0], _masked_ws(aws, causal, SAMPLE_SEQ), awb[0])
    mix_s[:, 0:SEG] = ya.astype(BF16)
    avn_ref[0] = v_rows

    glu = proj_s[:, O_BV:O_BV + SEG] * jax.nn.sigmoid(proj_s[:, O_BG:O_BG + SEG])
    cbn_ref[0] = glu
    cdn_ref[0] = proj_s[:, O_DX:O_DX + D_CONV_DIM]

    def conv_tiles(hist_ref, new, w_ref, b_ref, n_taps, tile0, act):
        n_hist = n_taps - 1
        n_tiles = new.shape[1] // LANES
        for c in range(n_tiles):
            cs = slice(c * LANES, (c + 1) * LANES)
            new_s[c] = new[:, cs]
            slabs = [hist_ref[0, t, :, cs] for t in range(n_hist)]
            slabs += [new_s[c, pl.ds(t, group, stride=SAMPLE_SEQ), :] for t in range(SAMPLE_SEQ)]
            for t in range(SAMPLE_SEQ):
                acc = slabs[t] * w_ref[0, 0:1, cs]
                for k in range(1, n_taps):
                    acc = acc + slabs[t + k] * w_ref[0, k:k + 1, cs]
                conv_s[tile0 + c, pl.ds(t, group, stride=SAMPLE_SEQ), :] = act(acc + b_ref[0, :, cs])

    conv_tiles(cbh_ref, glu, bcw, bcb, B_TAPS, 0, lambda a: a)
    conv_tiles(cdh_ref, proj_s[:, O_DX:O_DX + D_CONV_DIM], dcw, dcb, D_TAPS, SEG // LANES, _silu)
    conv_b = jnp.concatenate([conv_s[c] for c in range(SEG // LANES)], axis=1)
    conv_d = jnp.concatenate([conv_s[c] for c in range(SEG // LANES, (SEG + D_CONV_DIM) // LANES)], axis=1)

    yb = _mixer_b_post(conv_b, proj_s[:, O_BZ:O_BZ + SEG], blng[0], blnb[0], bpw, bpb[0])
    mix_s[:, SEG:2 * SEG] = yb.astype(BF16)

    kn_ref[0] = proj_s[:, O_CK:O_CK + KV_W]
    vn_ref[0] = proj_s[:, O_CV:O_CV + KV_W]
    sub_rows = ATT_SUB * SAMPLE_SEQ
    n_past = ATT_SUB * CH

    @pl.when((layer == 0) & (s == 0))
    def _():
        row = _iota((2 * sub_rows, n_past), 0) & (sub_rows - 1)
        col = _iota((2 * sub_rows, n_past), 1)
        q_t = row & (SAMPLE_SEQ - 1)
        key_pos = col & (CH - 1)
        rel_a = q_t + CH - key_pos
        ok_a = ((col >> 7) == (row >> 3)) & (key_pos >= q_t)
        row_n = _iota((2 * sub_rows, sub_rows), 0) & (sub_rows - 1)
        col_n = _iota((2 * sub_rows, sub_rows), 1)
        rel_b = (row_n & (SAMPLE_SEQ - 1)) - (col_n & (SAMPLE_SEQ - 1))
        ok_b = ((col_n >> 3) == (row_n >> 3)) & (rel_b >= 0)
        for kvh in range(2):
            slope, _ = _head_cols(sink_ref, layer, kvh, 2 * sub_rows)
            bias_a_s[kvh] = _attn_bias(rel_a, ok_a, slope)
            bias_b_s[kvh] = _attn_bias(rel_b, ok_b, slope)

    for u in range(group // ATT_SUB):
        us = slice(u * sub_rows, (u + 1) * sub_rows)
        q = proj_s[us, O_CQ:O_CQ + SEG] * (HEAD ** -0.5)
        outs = []
        for kvh in range(2):
            hs = [slice((2 * kvh + i) * HEAD, (2 * kvh + i + 1) * HEAD) for i in range(2)]
            q_rows = jnp.concatenate([q[:, hs[0]], q[:, hs[1]]], axis=0).astype(BF16)
            k_t = jnp.concatenate([kc_ref[0, u * ATT_SUB + i, kvh] for i in range(ATT_SUB)], axis=1).astype(BF16)
            v_t = jnp.concatenate([vc_ref[0, u * ATT_SUB + i, kvh] for i in range(ATT_SUB)], axis=1).astype(BF16)
            k_new = proj_s[us, O_CK + kvh * HEAD:O_CK + (kvh + 1) * HEAD].astype(BF16)
            v_new = proj_s[us, O_CV + kvh * HEAD:O_CV + (kvh + 1) * HEAD].astype(BF16)
            _, sink = _head_cols(sink_ref, layer, kvh, 2 * sub_rows)
            o = _attn([(_dot(q_rows, k_t), bias_a_s[kvh], functools.partial(_dot_nt, b=v_t)),
                       (_dot_nt(q_rows, k_new), bias_b_s[kvh], functools.partial(_dot, b=v_new))], sink)
            outs += [o[:sub_rows], o[sub_rows:]]
        yc = jnp.concatenate(outs, axis=1)
        mix_s[us, 2 * SEG:3 * SEG] = (yc * _silu(proj_s[us, O_CZ:O_CZ + SEG])).astype(BF16)

    xs, bm, cm_b, dt, acum, alast, y = _ssd_intra(conv_d, proj_s[:, O_DT:O_DT + DT_PAD], dtb[0], alog[0],
                                                  same_b, tri_b, causal)
    state_rows = group * HEAD
    own = (_iota((state_rows, CH), 0) >> 6) == (_iota((state_rows, CH), 1) >> 3)
    wcol = jnp.exp(alast - acum) * dt
    carry = jnp.exp(alast)
    xs_t = xs.T
    y_inter_t = []
    for h in range(N_HEADS):
        g = h // 2
        gs = slice(g * D_STATE, (g + 1) * D_STATE)
        h0 = ssm0_ref[0, :, h].reshape(state_rows, D_STATE)
        full = jnp.where(own, _dot_nt(h0.astype(BF16), cm_b[:, gs]), 0.0)
        y_inter_t.append(jnp.sum(full.reshape(group, HEAD, CH), axis=0))
        xt_h = xs_t[h * HEAD:(h + 1) * HEAD]
        lhs = jnp.where(own, jnp.concatenate([xt_h] * group, axis=0), 0.0).astype(BF16)
        wb_h = (bm[:, gs] * wcol[:, h:h + 1]).astype(BF16)
        carry_h = jnp.broadcast_to(carry[:, h:h + 1], (CH, D_STATE)).reshape(group, SAMPLE_SEQ, D_STATE)
        carry_h = jnp.concatenate([carry_h] * (HEAD // SAMPLE_SEQ), axis=1).reshape(state_rows, D_STATE)
        ssm_ref[0, :, h] = (h0 * carry_h + _dot(lhs, wb_h)).reshape(group, HEAD, D_STATE)
    y_inter = jnp.concatenate(y_inter_t, axis=0).T
    y = y + y_inter * _lane_bcast_heads(jnp.exp(acum))
    mix_s[:, 3 * SEG:4 * SEG] = _ssd_finish(y, xs, proj_s[:, O_DZ:O_DZ + SEG], dskip[0], dng[0]).astype(BF16)

    y_ref[pl.ds(row0, rows), :] = _finish_layer(x, mix_s, wout, gpost[0])


def _sample_layers(x, kc, vc, cbh, cdh, ssm0, p, group):
    depth = p["win"].shape[0]
    n_seq = ssm0.shape[1]
    n_rows = n_seq * SAMPLE_SEQ
    rows = group * SAMPLE_SEQ
    ns = n_seq // group

    def wspec(a):
        blk = (1,) + a.shape[1:]
        zeros = (0,) * (a.ndim - 1)
        return pl.BlockSpec(blk, lambda i, s: (i,) + zeros)

    def rspec(width, per_seq):
        return pl.BlockSpec((1, group * per_seq, width), lambda i, s: (i, s, 0))

    vectors = [p[k] for k in VEC_NAMES]
    weights = [p[k] for k in ("win", "wdt", "wout", "aws", "awb_s", "bcw", "bpw", "dcw")]
    ssm_spec = pl.BlockSpec((1, group, N_HEADS, HEAD, D_STATE), lambda i, s: (i, s, 0, 0, 0))
    cache_spec = pl.BlockSpec((1, group, 2, HEAD, CH), lambda i, s: (i, s, 0, 0, 0))

    def hist_spec(n_hist, width):
        return pl.BlockSpec((1, n_hist, group, width), lambda i, s: (i, 0, s, 0))

    in_specs = ([pl.BlockSpec(memory_space=pltpu.SMEM),
                 pl.BlockSpec((rows, D_MODEL), lambda i, s: (s, 0)),
                 cache_spec, cache_spec,
                 hist_spec(B_TAPS - 1, SEG), hist_spec(D_TAPS - 1, D_CONV_DIM), ssm_spec]
                + [pl.BlockSpec(a.shape, lambda i, s: (0, 0)) for a in vectors] + [wspec(a) for a in weights])
    out_shape = (jax.ShapeDtypeStruct((n_rows, D_MODEL), F32),
                 jax.ShapeDtypeStruct((depth, n_rows, KV_W), F32),
                 jax.ShapeDtypeStruct((depth, n_rows, KV_W), F32),
                 jax.ShapeDtypeStruct((depth, n_rows, SEG), F32),
                 jax.ShapeDtypeStruct((depth, n_rows, D_CONV_DIM), F32),
                 jax.ShapeDtypeStruct((depth, n_seq, N_HEADS, HEAD, D_STATE), F32),
                 jax.ShapeDtypeStruct((depth, n_rows, SEG), F32))
    out_specs = (pl.BlockSpec((n_rows, D_MODEL), lambda i, s: (0, 0)),
                 rspec(KV_W, SAMPLE_SEQ), rspec(KV_W, SAMPLE_SEQ),
                 rspec(SEG, SAMPLE_SEQ), rspec(D_CONV_DIM, SAMPLE_SEQ), ssm_spec,
                 rspec(SEG, SAMPLE_SEQ))
    scratch = [pltpu.VMEM((rows, D_MODEL), F32),
               pltpu.VMEM((rows, IN_PAD), F32),
               pltpu.VMEM((D_CONV_DIM // LANES, rows, LANES), F32),
               pltpu.VMEM(((SEG + D_CONV_DIM) // LANES, rows, LANES), F32),
               pltpu.VMEM((rows, D_MODEL), BF16),
               pltpu.VMEM((2, 2 * ATT_SUB * SAMPLE_SEQ, ATT_SUB * CH), F32),
               pltpu.VMEM((2, 2 * ATT_SUB * SAMPLE_SEQ, ATT_SUB * SAMPLE_SEQ), F32)]
    return pl.pallas_call(
        functools.partial(_sample_kernel, group),
        grid=(depth, ns),
        in_specs=in_specs, out_specs=out_specs, out_shape=out_shape, scratch_shapes=scratch,
        compiler_params=pltpu.CompilerParams(dimension_semantics=("arbitrary", "arbitrary"),
                                             vmem_limit_bytes=VMEM_LIMIT_BYTES),
        name="sample_layers",
    )(p["sinks"], x, kc, vc, cbh, cdh, ssm0, *vectors, *weights)


def _prepare_params(norm_pre, norm_post, w_in, w_out, a_ln_g, a_ln_b, a_ws, a_wb, b_conv_w, b_conv_b,
                    b_ln_g, b_ln_b, b_pw_w, b_pw_b, c_sinks, d_conv_w, d_conv_b, d_dt_bias, d_a_log,
                    d_skip, d_norm_g, group):
    depth = w_in.shape[0]
    win = w_in.astype(BF16)
    wdt = jnp.pad(w_in[:, :, O_DT:], ((0, 0), (0, 0), (0, DT_PAD - N_HEADS))).astype(BF16)
    wout = w_out.astype(BF16)

    def per_head(a):
        return jnp.repeat(a, HEAD, axis=-1)

    def lane_pad(a):
        return jnp.pad(a, ((0, 0), (0, DT_PAD - N_HEADS)))

    reps = CH // SAMPLE_SEQ
    return {
        "gpre": norm_pre, "gpost": norm_post, "alng": a_ln_g, "alnb": a_ln_b, "bcb": b_conv_b,
        "blng": b_ln_g, "blnb": b_ln_b, "bpb": b_pw_b, "dcb": d_conv_b, "dtb": lane_pad(d_dt_bias),
        "alog": lane_pad(d_a_log), "dskip": per_head(d_skip), "dng": d_norm_g,
        "sinks": c_sinks, "win": win, "wdt": wdt, "wout": wout,
        "aws": a_ws,
        "awb_p": per_head(jnp.swapaxes(a_wb, 1, 2)),
        "awb_s": per_head(jnp.tile(jnp.swapaxes(a_wb[:, :, :SAMPLE_SEQ], 1, 2), (1, reps, 1))),
        "bcw": b_conv_w, "bpw": b_pw_w.astype(BF16), "dcw": d_conv_w,
    }


PROMPT_ROWS = 1024
PROMPT_SUBBLOCKS = 4
SAMPLE_GROUP = CH // SAMPLE_SEQ


def kernel(x_prompt, x_sample, cache_win_k, cache_win_v, state_conv_b, state_conv_d, state_ssm, norm_pre, norm_post, w_in, w_out, a_ln_g, a_ln_b, a_ws, a_wb, b_conv_w, b_conv_b, b_ln_g, b_ln_b, b_pw_w, b_pw_b, c_sinks, d_conv_w, d_conv_b, d_dt_bias, d_a_log, d_skip, d_norm_g):
    depth = w_in.shape[0]
    batch, seq, _ = x_prompt.shape
    n_seq, dec_seq, _ = x_sample.shape
    assert dec_seq == SAMPLE_SEQ and cache_win_k.shape[2] == CH and seq % PROMPT_ROWS == 0
    assert n_seq % SAMPLE_GROUP == 0
    p = _prepare_params(norm_pre, norm_post, w_in, w_out, a_ln_g, a_ln_b, a_ws, a_wb, b_conv_w, b_conv_b,
                        b_ln_g, b_ln_b, b_pw_w, b_pw_b, c_sinks, d_conv_w, d_conv_b, d_dt_bias, d_a_log,
                        d_skip, d_norm_g, SAMPLE_GROUP)

    xp = x_prompt
    per_layer = []
    for i in range(depth):
        xp, *state = _prompt_layer(i, xp, p, PROMPT_ROWS)
        per_layer.append(state)
    pk, pv, pcb, pcd, pssm = (jnp.stack(list(a)) for a in zip(*per_layer))

    n_rows = n_seq * SAMPLE_SEQ
    ys, sk, sv, scb, scd, sssm, sav = _sample_layers(
        x_sample.reshape(n_rows, D_MODEL),
        jnp.transpose(cache_win_k, (0, 1, 3, 4, 2)), jnp.transpose(cache_win_v, (0, 1, 3, 4, 2)),
        jnp.swapaxes(state_conv_b, 1, 2), jnp.swapaxes(state_conv_d, 1, 2),
        state_ssm, p, SAMPLE_GROUP)

    kv_p = (depth, batch, CH, 2, HEAD)
    kv_s = (depth, n_seq, SAMPLE_SEQ, 2, HEAD)
    return (xp, ys.reshape(n_seq, SAMPLE_SEQ, D_MODEL),
            pk.reshape(kv_p), sk.reshape(kv_s), pv.reshape(kv_p), sv.reshape(kv_s),
            pcb, scb.reshape(depth, n_seq, SAMPLE_SEQ, SEG),
            pcd, scd.reshape(depth, n_seq, SAMPLE_SEQ, D_CONV_DIM),
            pssm, sssm, sav.reshape(depth, n_seq, SAMPLE_SEQ, SEG))
```

```python
import functools

import jax
import jax.numpy as jnp
from jax import lax
from jax.experimental import pallas as pl
from jax.experimental.pallas import tpu as pltpu

F32 = jnp.float32
BF16 = jnp.bfloat16
EPS = 1e-6

D_MODEL = 1024
LANES = 128
CH = 128
HEAD = 64
N_HEADS = 4
SEG = 256
KV_W = 128
B_TAPS = 31
D_TAPS = 4
D_CONV_DIM = 768
D_STATE = 128
DT_PAD = 128
SAMPLE_SEQ = 8
ATT_SUB = 8
HEAD_SHIFT = HEAD.bit_length() - 1
CH_SHIFT = CH.bit_length() - 1
SEQ_SHIFT = SAMPLE_SEQ.bit_length() - 1
SUBLANES = 8
B_HIST = -(-(B_TAPS - 1) // SUBLANES) * SUBLANES
D_HIST = -(-(D_TAPS - 1) // SUBLANES) * SUBLANES
CONV_ROW_TILE = 32

O_AU, O_AV, O_AZ = 0, 256, 512
O_BV, O_BG, O_BZ = 768, 1024, 1280
O_CQ, O_CK, O_CV, O_CZ = 1536, 1792, 1920, 2048
O_DZ, O_DX, O_DT = 2304, 2560, 3328
IN_RAW = 3332
IN_PAD = 3456
PROJ_TILE = 1152
HELD_TILE = 512

ALIBI = tuple(float(2.0 ** (-8.0 * (i + 1) / N_HEADS)) for i in range(N_HEADS))
VMEM_LIMIT_BYTES = 56 * 1024 * 1024


VEC_NAMES = ("gpre", "gpost", "alng", "alnb", "bcb", "blng", "blnb", "bpb", "dcb", "dtb", "alog", "dskip", "dng")


def _layer_rows(layer, refs):
    return [r[pl.ds(layer, 1), :][None] for r in refs]


def _iota(shape, dim):
    return lax.broadcasted_iota(jnp.int32, shape, dim)


def _dot(a, b):
    return jnp.dot(a, b, preferred_element_type=F32)


def _dot_nt(a, b):
    return lax.dot_general(a, b, (((1,), (1,)), ((), ())), preferred_element_type=F32)


def _rms(x, g):
    return x * lax.rsqrt(jnp.mean(x * x, axis=-1, keepdims=True) + EPS) * g


def _ln(x, g, b):
    mu = jnp.mean(x, axis=-1, keepdims=True)
    xc = x - mu
    return xc * lax.rsqrt(jnp.mean(xc * xc, axis=-1, keepdims=True) + EPS) * g + b


def _silu(x):
    return x * jax.nn.sigmoid(x)


def _softplus(x):
    return jnp.maximum(x, 0.0) + jnp.log1p(jnp.exp(-jnp.abs(x)))


def _hi_lo(x):
    hi = x.astype(BF16)
    return hi, (x - hi.astype(F32)).astype(BF16)


def _head_of_lane(width):
    return _iota((CH, width), 1) >> HEAD_SHIFT


def _lane_bcast_heads(cols, width=SEG):
    head = _head_of_lane(width)
    out = jnp.zeros((CH, width), F32)
    for h in range(N_HEADS):
        out = jnp.where(head == h, cols[:, h:h + 1], out)
    return out


def _chunk_masks(seq_rows):
    shift = seq_rows.bit_length() - 1
    r = _iota((CH, CH), 0)
    c = _iota((CH, CH), 1)
    same = (r >> shift) == (c >> shift)
    causal = same & (c <= r)
    return same, causal


def _after(v):
    return jnp.minimum(jnp.abs(v[0:1, 0:1]), 0.0).astype(BF16)


def _project(x, gpre, win_ref, wdt_ref, proj, after=None):
    h = _rms(x, gpre).astype(BF16)
    tile = PROJ_TILE if after is None else HELD_TILE
    bounds = [(n0, min(n0 + tile, O_DT)) for n0 in range(0, O_DT, tile)]
    if after is None:
        after = [None] * (len(bounds) + 1)
    assert len(after) == len(bounds) + 1
    for (n0, n1), zero in zip(bounds, after):
        proj[:, n0:n1] = _dot(h if zero is None else h + zero, win_ref[0, :, n0:n1])
    proj[:, O_DT:O_DT + DT_PAD] = _dot(h if after[-1] is None else h + after[-1], wdt_ref[0])


def _masked_ws(ws_ref, causal, seq_rows):
    if seq_rows < CH:
        r = _iota((CH, CH), 0)
        c = _iota((CH, CH), 1)
        lane_tile = jnp.where((r < seq_rows) & ((c & (seq_rows - 1)) == r), 1.0, 0.0).astype(BF16)
    out = []
    for h in range(N_HEADS):
        if seq_rows == CH:
            w = ws_ref[0, h]
        else:
            top = _dot(ws_ref[0, h, 0:seq_rows, :].astype(BF16), lane_tile)
            w = jnp.concatenate([top] * (CH // seq_rows), axis=0)
        out.append(jnp.where(causal, w, 0.0).astype(BF16))
    return out


def _mixer_a(u_raw, v_raw, z, ln_g, ln_b, ws_masked, wb):
    u = jax.nn.gelu(u_raw)
    v = _ln(jax.nn.gelu(v_raw), ln_g, ln_b)
    vb = v.astype(BF16)
    head = _head_of_lane(SEG)
    s = jnp.zeros((CH, SEG), F32)
    for h in range(N_HEADS):
        s = jnp.where(head == h, _dot(ws_masked[h], vb), s)
    return u * (s + wb) * _silu(z), v


def _mixer_b_post(c, z, ln_g, ln_b, pw_ref, pw_b):
    y = _dot(_silu(_ln(c, ln_g, ln_b)).astype(BF16), pw_ref[0]) + pw_b
    return y * _silu(z)


def _attn_bias(rel, ok, slope):
    return jnp.where(ok, -slope * rel.astype(F32), -jnp.inf)


def _attn(parts, sink):
    scores = [qk + bias for qk, bias, _ in parts]
    same_shape = all(s.shape == scores[0].shape for s in scores)

    def row_reduce(xs, combine, reduce):
        if same_shape:
            xs = [functools.reduce(combine, xs)]
        return functools.reduce(combine, [reduce(x, axis=-1, keepdims=True) for x in xs])

    m = jnp.maximum(sink, row_reduce(scores, jnp.maximum, jnp.max))
    probs = [jnp.exp(s - m) for s in scores]
    den = jnp.exp(sink - m) + row_reduce(probs, jnp.add, jnp.sum)
    out = functools.reduce(jnp.add, [pv(p.astype(BF16)) for p, (_, _, pv) in zip(probs, parts)])
    return out / den


def _swap_inner_heads(x):
    t0, t1 = x[:, 0:KV_W], x[:, KV_W:2 * KV_W]
    lo = _iota(t0.shape, 1) < HEAD
    return jnp.concatenate([jnp.where(lo, t0, pltpu.roll(t1, HEAD, 1)),
                            jnp.where(lo, pltpu.roll(t0, HEAD, 1), t1)], axis=1)


def _attn_q_rows(q, kvh):
    lane = _iota((q.shape[0], KV_W), 1)
    keep = (lane < HEAD) if kvh == 0 else (lane >= HEAD)
    t0 = jnp.where(keep, q[:, 0:KV_W], 0.0)
    t1 = jnp.where(keep, q[:, KV_W:2 * KV_W], 0.0)
    return jnp.concatenate([t0, t1], axis=0).astype(BF16)


def _attn_merge(o0, o1):
    half = o0.shape[0] // 2
    lo = _iota((half, KV_W), 1) < HEAD
    return jnp.concatenate([jnp.where(lo, o0[:half], o1[:half]),
                            jnp.where(lo, o0[half:], o1[half:])], axis=1)


def _head_cols(sink_ref, layer, kvh, rows):
    half = rows // 2
    first = _iota((rows, 1), 0) < half
    slope = jnp.where(first, ALIBI[2 * kvh], ALIBI[2 * kvh + 1])
    sink = jnp.where(first, sink_ref[layer, 2 * kvh], sink_ref[layer, 2 * kvh + 1])
    return slope, sink


def _sum_matrices(same, causal):
    return jnp.where(same, 1.0, 0.0).astype(BF16), jnp.where(causal, 1.0, 0.0).astype(BF16)


def _ssd_intra(xbc, dt_raw, dtb, alog, same_b, tri_b, causal):
    xs = xbc[:, 0:SEG]
    bm_b = xbc[:, SEG:2 * SEG].astype(BF16)
    cm_b = xbc[:, 2 * SEG:3 * SEG].astype(BF16)
    xs_b = xs.astype(BF16)
    dt = _softplus(dt_raw + dtb)
    da_hi, da_lo = _hi_lo(dt * (-jnp.exp(alog)))
    acum = _dot(tri_b, da_hi) + _dot(tri_b, da_lo)
    alast = _dot(same_b, da_hi) + _dot(same_b, da_lo)
    acum_t = acum.T
    dt_t = dt.T
    head = _head_of_lane(SEG)
    y = jnp.zeros((CH, SEG), F32)
    for g in range(2):
        sc = _dot_nt(cm_b[:, g * D_STATE:(g + 1) * D_STATE], bm_b[:, g * D_STATE:(g + 1) * D_STATE])
        for hh in range(2):
            h = 2 * g + hh
            seg = acum[:, h:h + 1] - acum_t[h:h + 1, :]
            decay = jnp.exp(jnp.where(causal, seg, -jnp.inf))
            m = (sc * decay * dt_t[h:h + 1, :]).astype(BF16)
            y = jnp.where(head == h, _dot(m, xs_b), y)
    return xs, xbc[:, SEG:2 * SEG], cm_b, dt, acum, alast, y


def _ssd_finish(y, xs, z, dskip, dng):
    y = (y + dskip * xs) * _silu(z)
    parts = []
    for g in range(2):
        yg = y[:, g * D_STATE:(g + 1) * D_STATE]
        parts.append(_rms(yg, dng[:, g * D_STATE:(g + 1) * D_STATE]))
    return jnp.concatenate(parts, axis=1)


def _finish_layer(x, mix_s, wout_ref, gpost):
    return x + _rms(_dot(mix_s[...], wout_ref[0]), gpost)


def _prompt_kernel(layer, rows,
                   sink_ref, x_ref, *refs):
    vec_refs, refs = refs[:len(VEC_NAMES)], refs[len(VEC_NAMES):]
    (win, wdt, wout, aws, awb, bcw, bpw, dcw,
     y_ref, kn_ref, vn_ref, cbn_ref, cdn_ref, ssm_ref,
     proj_s, glu_s, xbc_s, conv_s, kprev_s, vprev_s, h_s, mix_s, bias_s) = refs
    gpre, gpost, alng, alnb, bcb, blng, blnb, bpb, dcb, dtb, alog, dskip, dng = _layer_rows(layer, vec_refs)
    j = pl.program_id(1)
    n_chunks = rows // CH
    b_hist, d_hist = B_HIST, D_HIST

    @pl.when(j == 0)
    def _():
        glu_s[:, 0:b_hist, :] = jnp.zeros((SEG // LANES, b_hist, LANES), F32)
        xbc_s[:, 0:d_hist, :] = jnp.zeros((D_CONV_DIM // LANES, d_hist, LANES), F32)
        kprev_s[...] = jnp.zeros((CH, KV_W), BF16)
        vprev_s[...] = jnp.zeros((CH, KV_W), BF16)
        h_s[...] = jnp.zeros((N_HEADS * HEAD, D_STATE), F32)
        row2 = _iota((2 * CH, CH), 0) & (CH - 1)
        col2 = _iota((2 * CH, CH), 1)
        for kvh in range(2):
            slope, _ = _head_cols(sink_ref, layer, kvh, 2 * CH)
            bias_s[2 * kvh] = _attn_bias(row2 + CH - col2, col2 >= row2, slope)
            bias_s[2 * kvh + 1] = _attn_bias(row2 - col2, col2 <= row2, slope)

    @pl.when(j > 0)
    def _():
        glu_s[:, 0:b_hist, :] = glu_s[:, rows:rows + b_hist, :]
        xbc_s[:, 0:d_hist, :] = xbc_s[:, rows:rows + d_hist, :]

    def project(lo, hi, after=None):
        _project(x_ref[0, lo:hi], gpre[0], win, wdt, proj_s.at[lo:hi], after)

    def conv_inputs(lo, hi):
        for c in range(SEG // LANES):
            glu_s[c, b_hist + lo:b_hist + hi, :] = (
                proj_s[lo:hi, O_BV + c * LANES:O_BV + (c + 1) * LANES]
                * jax.nn.sigmoid(proj_s[lo:hi, O_BG + c * LANES:O_BG + (c + 1) * LANES]))
        for c in range(D_CONV_DIM // LANES):
            xbc_s[c, d_hist + lo:d_hist + hi, :] = proj_s[lo:hi, O_DX + c * LANES:O_DX + (c + 1) * LANES]

    def conv(buf, hist, n_tiles, w_ref, b_ref, n_taps, out_col, act, lo, hi):
        tile = CONV_ROW_TILE
        for c in range(n_tiles):
            cs = slice(c * LANES, (c + 1) * LANES)
            for r0 in range(lo, hi, tile):
                first = r0 + hist - (n_taps - 1)
                acc = buf[c, pl.ds(first, tile, stride=1), :] * w_ref[0, 0:1, cs]
                for k in range(1, n_taps):
                    acc = acc + buf[c, pl.ds(first + k, tile, stride=1), :] * w_ref[0, k:k + 1, cs]
                conv_s[r0:r0 + tile, out_col + c * LANES:out_col + (c + 1) * LANES] = act(acc + b_ref[0, :, cs])

    def convs(lo, hi):
        conv(glu_s, b_hist, SEG // LANES, bcw, bcb, B_TAPS, 0, lambda a: a, lo, hi)
        conv(xbc_s, d_hist, D_CONV_DIM // LANES, dcw, dcb, D_TAPS, SEG, _silu, lo, hi)

    same, causal = _chunk_masks(CH)
    same_b, tri_b = _sum_matrices(same, causal)
    ws_masked = _masked_ws(aws, causal, CH)
    kv_carry = [kprev_s[...], vprev_s[...]]

    def mixers(c):
        rs = slice(c * CH, (c + 1) * CH)
        k_prev, v_prev = kv_carry

        ya, _ = _mixer_a(proj_s[rs, O_AU:O_AU + SEG], proj_s[rs, O_AV:O_AV + SEG], proj_s[rs, O_AZ:O_AZ + SEG],
                         alng[0], alnb[0], ws_masked, awb[0])
        mix_s[rs, 0:SEG] = ya.astype(BF16)

        yb = _mixer_b_post(conv_s[rs, 0:SEG], proj_s[rs, O_BZ:O_BZ + SEG], blng[0], blnb[0], bpw, bpb[0])
        mix_s[rs, SEG:2 * SEG] = yb.astype(BF16)

        q = _swap_inner_heads(proj_s[rs, O_CQ:O_CQ + SEG] * (HEAD ** -0.5))
        k_cur = proj_s[rs, O_CK:O_CK + KV_W].astype(BF16)
        v_cur = proj_s[rs, O_CV:O_CV + KV_W].astype(BF16)
        outs = []
        for kvh in range(2):
            _, sink = _head_cols(sink_ref, layer, kvh, 2 * CH)
            q_rows = _attn_q_rows(q, kvh)
            bias_prev = bias_s[2 * kvh]
            if c == 0:
                bias_prev = jnp.where(j > 0, bias_prev, -jnp.inf)
            outs.append(_attn([(_dot_nt(q_rows, k_prev), bias_prev, functools.partial(_dot, b=v_prev)),
                               (_dot_nt(q_rows, k_cur), bias_s[2 * kvh + 1], functools.partial(_dot, b=v_cur))],
                              sink))
        yc = _swap_inner_heads(_attn_merge(outs[0], outs[1])) * _silu(proj_s[rs, O_CZ:O_CZ + SEG])
        mix_s[rs, 2 * SEG:3 * SEG] = yc.astype(BF16)
        kv_carry[:] = [k_cur, v_cur]

        xs, bm, cm_b, dt, acum, alast, y = _ssd_intra(conv_s[rs, SEG:SEG + D_CONV_DIM],
                                                      proj_s[rs, O_DT:O_DT + DT_PAD], dtb[0], alog[0],
                                                      same_b, tri_b, causal)
        h_prev = h_s[...]
        h_prev_b = h_prev.astype(BF16)
        y_inter = jnp.concatenate(
            [_dot_nt(cm_b[:, g * D_STATE:(g + 1) * D_STATE], h_prev_b[g * 2 * HEAD:(g + 1) * 2 * HEAD])
             for g in range(2)], axis=1)
        y = y + y_inter * _lane_bcast_heads(jnp.exp(acum))
        wcol = jnp.exp(alast - acum) * dt
        carry = jnp.exp(alast)
        xs_t = xs.T.astype(BF16)
        for h in range(N_HEADS):
            g = h // 2
            wb_h = (bm[:, g * D_STATE:(g + 1) * D_STATE] * wcol[:, h:h + 1]).astype(BF16)
            hs = slice(h * HEAD, (h + 1) * HEAD)
            h_s[hs, :] = h_prev[hs] * carry[0:HEAD, h:h + 1] + _dot(xs_t[hs], wb_h)
        yd = _ssd_finish(y, xs, proj_s[rs, O_DZ:O_DZ + SEG], dskip[0], dng[0])
        mix_s[rs, 3 * SEG:4 * SEG] = yd.astype(BF16)
        return [_after(v) for v in (ya, yb, yc, yd)]

    def finish(lo, hi):
        y_ref[0, lo:hi] = x_ref[0, lo:hi] + _rms(_dot(mix_s[lo:hi], wout[0]), gpost[0])

    blk = rows // PROMPT_SUBBLOCKS
    assert blk == 2 * CH
    project(0, blk)
    for lo in range(0, rows, blk):
        hi = lo + blk
        conv_inputs(lo, hi)
        convs(lo, hi)
        start = _after(conv_s[lo:lo + SUBLANES, 0:LANES])
        first = mixers(lo // CH)
        second = mixers(lo // CH + 1)
        if hi < rows:
            project(hi, hi + blk, [start, start, first[0], first[1], first[2], first[3], second[0], second[1]])
        finish(lo, hi)
    kprev_s[...] = kv_carry[0]
    vprev_s[...] = kv_carry[1]

    @pl.when(j == pl.num_programs(1) - 1)
    def _():
        kn_ref[0] = proj_s[rows - CH:rows, O_CK:O_CK + KV_W]
        vn_ref[0] = proj_s[rows - CH:rows, O_CV:O_CV + KV_W]
        for c in range(SEG // LANES):
            cbn_ref[0, :, c * LANES:(c + 1) * LANES] = glu_s[c, b_hist + rows - (B_TAPS - 1):b_hist + rows, :]
        for c in range(D_CONV_DIM // LANES):
            cdn_ref[0, :, c * LANES:(c + 1) * LANES] = xbc_s[c, d_hist + rows - (D_TAPS - 1):d_hist + rows, :]
        ssm_ref[0] = h_s[...].reshape(N_HEADS, HEAD, D_STATE)


def _prompt_layer(layer, x, p, rows):
    batch, seq, _ = x.shape
    nb = seq // rows

    def wspec(a):
        blk = (1,) + a.shape[1:]
        zeros = (0,) * (a.ndim - 1)
        return pl.BlockSpec(blk, lambda b, j: (layer,) + zeros)

    vectors = [p[k] for k in VEC_NAMES]
    weights = [p[k] for k in ("win", "wdt", "wout", "aws", "awb_p", "bcw", "bpw", "dcw")]
    x_spec = pl.BlockSpec((1, rows, D_MODEL), lambda b, j: (b, j, 0))
    in_specs = ([pl.BlockSpec(memory_space=pltpu.SMEM), x_spec]
                + [pl.BlockSpec(a.shape, lambda b, j: (0, 0)) for a in vectors] + [wspec(a) for a in weights])
    out_shape = (jax.ShapeDtypeStruct((batch, seq, D_MODEL), F32),
                 jax.ShapeDtypeStruct((batch, CH, KV_W), F32),
                 jax.ShapeDtypeStruct((batch, CH, KV_W), F32),
                 jax.ShapeDtypeStruct((batch, B_TAPS - 1, SEG), F32),
                 jax.ShapeDtypeStruct((batch, D_TAPS - 1, D_CONV_DIM), F32),
                 jax.ShapeDtypeStruct((batch, N_HEADS, HEAD, D_STATE), F32))
    out_specs = (x_spec,
                 pl.BlockSpec((1, CH, KV_W), lambda b, j: (b, 0, 0)),
                 pl.BlockSpec((1, CH, KV_W), lambda b, j: (b, 0, 0)),
                 pl.BlockSpec((1, B_TAPS - 1, SEG), lambda b, j: (b, 0, 0)),
                 pl.BlockSpec((1, D_TAPS - 1, D_CONV_DIM), lambda b, j: (b, 0, 0)),
                 pl.BlockSpec((1, N_HEADS, HEAD, D_STATE), lambda b, j: (b, 0, 0, 0)))
    scratch = [pltpu.VMEM((rows, IN_PAD), F32),
               pltpu.VMEM((SEG // LANES, rows + B_HIST, LANES), F32),
               pltpu.VMEM((D_CONV_DIM // LANES, rows + D_HIST, LANES), F32),
               pltpu.VMEM((rows, SEG + D_CONV_DIM), F32),
               pltpu.VMEM((CH, KV_W), BF16),
               pltpu.VMEM((CH, KV_W), BF16),
               pltpu.VMEM((N_HEADS * HEAD, D_STATE), F32),
               pltpu.VMEM((rows, D_MODEL), BF16),
               pltpu.VMEM((4, 2 * CH, CH), F32)]
    return pl.pallas_call(
        functools.partial(_prompt_kernel, layer, rows),
        grid=(batch, nb),
        in_specs=in_specs, out_specs=out_specs, out_shape=out_shape, scratch_shapes=scratch,
        compiler_params=pltpu.CompilerParams(dimension_semantics=("arbitrary", "arbitrary"),
                                             vmem_limit_bytes=VMEM_LIMIT_BYTES),
        name=f"prompt_layer{layer}",
    )(p["sinks"], x, *vectors, *weights)


def _sample_kernel(group,
                   sink_ref, x_ref, kc_ref, vc_ref, cbh_ref, cdh_ref, ssm0_ref, *refs):
    vec_refs, refs = refs[:len(VEC_NAMES)], refs[len(VEC_NAMES):]
    (win, wdt, wout, aws, awb, bcw, bpw, dcw,
     y_ref, kn_ref, vn_ref, cbn_ref, cdn_ref, ssm_ref, avn_ref,
     xin_s, proj_s, new_s, conv_s, mix_s, bias_a_s, bias_b_s) = refs
    layer = pl.program_id(0)
    gpre, gpost, alng, alnb, bcb, blng, blnb, bpb, dcb, dtb, alog, dskip, dng = _layer_rows(layer, vec_refs)
    s = pl.program_id(1)
    rows = group * SAMPLE_SEQ
    row0 = pl.multiple_of(s * rows, rows)

    @pl.when(layer == 0)
    def _():
        xin_s[...] = x_ref[...]

    @pl.when(layer > 0)
    def _():
        xin_s[...] = y_ref[pl.ds(row0, rows), :]

    x = xin_s[...]
    _project(x, gpre[0], win, wdt, proj_s)
    same, causal = _chunk_masks(SAMPLE_SEQ)
    same_b, tri_b = _sum_matrices(same, causal)

    ya, v_rows = _mixer_a(proj_s[:, O_AU:O_AU + SEG], proj_s[:, O_AV:O_AV + SEG], proj_s[:, O_AZ:O_AZ + SEG],
                          alng[0], alnb[0], _masked_ws(aws, causal, SAMPLE_SEQ), awb[0])
    mix_s[:, 0:SEG] = ya.astype(BF16)
    avn_ref[0] = v_rows

    glu = proj_s[:, O_BV:O_BV + SEG] * jax.nn.sigmoid(proj_s[:, O_BG:O_BG + SEG])
    cbn_ref[0] = glu
    cdn_ref[0] = proj_s[:, O_DX:O_DX + D_CONV_DIM]

    def conv_tiles(hist_ref, new, w_ref, b_ref, n_taps, tile0, act):
        n_hist = n_taps - 1
        n_tiles = new.shape[1] // LANES
        for c in range(n_tiles):
            cs = slice(c * LANES, (c + 1) * LANES)
            new_s[c] = new[:, cs]
            slabs = [hist_ref[0, t, :, cs] for t in range(n_hist)]
            slabs += [new_s[c, pl.ds(t, group, stride=SAMPLE_SEQ), :] for t in range(SAMPLE_SEQ)]
            for t in range(SAMPLE_SEQ):
                acc = slabs[t] * w_ref[0, 0:1, cs]
                for k in range(1, n_taps):
                    acc = acc + slabs[t + k] * w_ref[0, k:k + 1, cs]
                conv_s[tile0 + c, pl.ds(t, group, stride=SAMPLE_SEQ), :] = act(acc + b_ref[0, :, cs])

    conv_tiles(cbh_ref, glu, bcw, bcb, B_TAPS, 0, lambda a: a)
    conv_tiles(cdh_ref, proj_s[:, O_DX:O_DX + D_CONV_DIM], dcw, dcb, D_TAPS, SEG // LANES, _silu)
    conv_b = jnp.concatenate([conv_s[c] for c in range(SEG // LANES)], axis=1)
    conv_d = jnp.concatenate([conv_s[c] for c in range(SEG // LANES, (SEG + D_CONV_DIM) // LANES)], axis=1)

    yb = _mixer_b_post(conv_b, proj_s[:, O_BZ:O_BZ + SEG], blng[0], blnb[0], bpw, bpb[0])
    mix_s[:, SEG:2 * SEG] = yb.astype(BF16)

    kn_ref[0] = proj_s[:, O_CK:O_CK + KV_W]
    vn_ref[0] = proj_s[:, O_CV:O_CV + KV_W]
    sub_rows = ATT_SUB * SAMPLE_SEQ
    n_past = ATT_SUB * CH

    @pl.when((layer == 0) & (s == 0))
    def _():
        row = _iota((2 * sub_rows, n_past), 0) & (sub_rows - 1)
        col = _iota((2 * sub_rows, n_past), 1)
        q_t = row & (SAMPLE_SEQ - 1)
        key_pos = col & (CH - 1)
        rel_a = q_t + CH - key_pos
        ok_a = ((col >> CH_SHIFT) == (row >> SEQ_SHIFT)) & (key_pos >= q_t)
        row_n = _iota((2 * sub_rows, sub_rows), 0) & (sub_rows - 1)
        col_n = _iota((2 * sub_rows, sub_rows), 1)
        rel_b = (row_n & (SAMPLE_SEQ - 1)) - (col_n & (SAMPLE_SEQ - 1))
        ok_b = ((col_n >> SEQ_SHIFT) == (row_n >> SEQ_SHIFT)) & (rel_b >= 0)
        for kvh in range(2):
            slope, _ = _head_cols(sink_ref, layer, kvh, 2 * sub_rows)
            bias_a_s[kvh] = _attn_bias(rel_a, ok_a, slope)
            bias_b_s[kvh] = _attn_bias(rel_b, ok_b, slope)

    for u in range(group // ATT_SUB):
        us = slice(u * sub_rows, (u + 1) * sub_rows)
        q = proj_s[us, O_CQ:O_CQ + SEG] * (HEAD ** -0.5)
        outs = []
        for kvh in range(2):
            hs = [slice((2 * kvh + i) * HEAD, (2 * kvh + i + 1) * HEAD) for i in range(2)]
            q_rows = jnp.concatenate([q[:, hs[0]], q[:, hs[1]]], axis=0).astype(BF16)
            k_t = jnp.concatenate([kc_ref[0, u * ATT_SUB + i, kvh] for i in range(ATT_SUB)], axis=1).astype(BF16)
            v_t = jnp.concatenate([vc_ref[0, u * ATT_SUB + i, kvh] for i in range(ATT_SUB)], axis=1).astype(BF16)
            k_new = proj_s[us, O_CK + kvh * HEAD:O_CK + (kvh + 1) * HEAD].astype(BF16)
            v_new = proj_s[us, O_CV + kvh * HEAD:O_CV + (kvh + 1) * HEAD].astype(BF16)
            _, sink = _head_cols(sink_ref, layer, kvh, 2 * sub_rows)
            o = _attn([(_dot(q_rows, k_t), bias_a_s[kvh], functools.partial(_dot_nt, b=v_t)),
                       (_dot_nt(q_rows, k_new), bias_b_s[kvh], functools.partial(_dot, b=v_new))], sink)
            outs += [o[:sub_rows], o[sub_rows:]]
        yc = jnp.concatenate(outs, axis=1)
        mix_s[us, 2 * SEG:3 * SEG] = (yc * _silu(proj_s[us, O_CZ:O_CZ + SEG])).astype(BF16)

    xs, bm, cm_b, dt, acum, alast, y = _ssd_intra(conv_d, proj_s[:, O_DT:O_DT + DT_PAD], dtb[0], alog[0],
                                                  same_b, tri_b, causal)
    state_rows = group * HEAD
    own = (_iota((state_rows, CH), 0) >> HEAD_SHIFT) == (_iota((state_rows, CH), 1) >> SEQ_SHIFT)
    wcol = jnp.exp(alast - acum) * dt
    carry = jnp.exp(alast)
    xs_t = xs.T
    y_inter_t = []
    for h in range(N_HEADS):
        g = h // 2
        gs = slice(g * D_STATE, (g + 1) * D_STATE)
        h0 = ssm0_ref[0, :, h].reshape(state_rows, D_STATE)
        full = jnp.where(own, _dot_nt(h0.astype(BF16), cm_b[:, gs]), 0.0)
        y_inter_t.append(jnp.sum(full.reshape(group, HEAD, CH), axis=0))
        xt_h = xs_t[h * HEAD:(h + 1) * HEAD]
        lhs = jnp.where(own, jnp.concatenate([xt_h] * group, axis=0), 0.0).astype(BF16)
        wb_h = (bm[:, gs] * wcol[:, h:h + 1]).astype(BF16)
        carry_h = jnp.broadcast_to(carry[:, h:h + 1], (CH, D_STATE)).reshape(group, SAMPLE_SEQ, D_STATE)
        carry_h = jnp.concatenate([carry_h] * (HEAD // SAMPLE_SEQ), axis=1).reshape(state_rows, D_STATE)
        ssm_ref[0, :, h] = (h0 * carry_h + _dot(lhs, wb_h)).reshape(group, HEAD, D_STATE)
    y_inter = jnp.concatenate(y_inter_t, axis=0).T
    y = y + y_inter * _lane_bcast_heads(jnp.exp(acum))
    mix_s[:, 3 * SEG:4 * SEG] = _ssd_finish(y, xs, proj_s[:, O_DZ:O_DZ + SEG], dskip[0], dng[0]).astype(BF16)

    y_ref[pl.ds(row0, rows), :] = _finish_layer(x, mix_s, wout, gpost[0])


def _sample_layers(x, kc, vc, cbh, cdh, ssm0, p, group):
    depth = p["win"].shape[0]
    n_seq = ssm0.shape[1]
    n_rows = n_seq * SAMPLE_SEQ
    rows = group * SAMPLE_SEQ
    ns = n_seq // group

    def wspec(a):
        blk = (1,) + a.shape[1:]
        zeros = (0,) * (a.ndim - 1)
        return pl.BlockSpec(blk, lambda i, s: (i,) + zeros)

    def rspec(width, per_seq):
        return pl.BlockSpec((1, group * per_seq, width), lambda i, s: (i, s, 0))

    vectors = [p[k] for k in VEC_NAMES]
    weights = [p[k] for k in ("win", "wdt", "wout", "aws", "awb_s", "bcw", "bpw", "dcw")]
    ssm_spec = pl.BlockSpec((1, group, N_HEADS, HEAD, D_STATE), lambda i, s: (i, s, 0, 0, 0))
    cache_spec = pl.BlockSpec((1, group, 2, HEAD, CH), lambda i, s: (i, s, 0, 0, 0))

    def hist_spec(n_hist, width):
        return pl.BlockSpec((1, n_hist, group, width), lambda i, s: (i, 0, s, 0))

    in_specs = ([pl.BlockSpec(memory_space=pltpu.SMEM),
                 pl.BlockSpec((rows, D_MODEL), lambda i, s: (s, 0)),
                 cache_spec, cache_spec,
                 hist_spec(B_TAPS - 1, SEG), hist_spec(D_TAPS - 1, D_CONV_DIM), ssm_spec]
                + [pl.BlockSpec(a.shape, lambda i, s: (0, 0)) for a in vectors] + [wspec(a) for a in weights])
    out_shape = (jax.ShapeDtypeStruct((n_rows, D_MODEL), F32),
                 jax.ShapeDtypeStruct((depth, n_rows, KV_W), F32),
                 jax.ShapeDtypeStruct((depth, n_rows, KV_W), F32),
                 jax.ShapeDtypeStruct((depth, n_rows, SEG), F32),
                 jax.ShapeDtypeStruct((depth, n_rows, D_CONV_DIM), F32),
                 jax.ShapeDtypeStruct((depth, n_seq, N_HEADS, HEAD, D_STATE), F32),
                 jax.ShapeDtypeStruct((depth, n_rows, SEG), F32))
    out_specs = (pl.BlockSpec((n_rows, D_MODEL), lambda i, s: (0, 0)),
                 rspec(KV_W, SAMPLE_SEQ), rspec(KV_W, SAMPLE_SEQ),
                 rspec(SEG, SAMPLE_SEQ), rspec(D_CONV_DIM, SAMPLE_SEQ), ssm_spec,
                 rspec(SEG, SAMPLE_SEQ))
    scratch = [pltpu.VMEM((rows, D_MODEL), F32),
               pltpu.VMEM((rows, IN_PAD), F32),
               pltpu.VMEM((D_CONV_DIM // LANES, rows, LANES), F32),
               pltpu.VMEM(((SEG + D_CONV_DIM) // LANES, rows, LANES), F32),
               pltpu.VMEM((rows, D_MODEL), BF16),
               pltpu.VMEM((2, 2 * ATT_SUB * SAMPLE_SEQ, ATT_SUB * CH), F32),
               pltpu.VMEM((2, 2 * ATT_SUB * SAMPLE_SEQ, ATT_SUB * SAMPLE_SEQ), F32)]
    return pl.pallas_call(
        functools.partial(_sample_kernel, group),
        grid=(depth, ns),
        in_specs=in_specs, out_specs=out_specs, out_shape=out_shape, scratch_shapes=scratch,
        compiler_params=pltpu.CompilerParams(dimension_semantics=("arbitrary", "arbitrary"),
                                             vmem_limit_bytes=VMEM_LIMIT_BYTES),
        name="sample_layers",
    )(p["sinks"], x, kc, vc, cbh, cdh, ssm0, *vectors, *weights)


def _prepare_params(norm_pre, norm_post, w_in, w_out, a_ln_g, a_ln_b, a_ws, a_wb, b_conv_w, b_conv_b,
                    b_ln_g, b_ln_b, b_pw_w, b_pw_b, c_sinks, d_conv_w, d_conv_b, d_dt_bias, d_a_log,
                    d_skip, d_norm_g, group):
    depth = w_in.shape[0]
    win = w_in.astype(BF16)
    wdt = jnp.pad(w_in[:, :, O_DT:], ((0, 0), (0, 0), (0, DT_PAD - N_HEADS))).astype(BF16)
    wout = w_out.astype(BF16)

    def per_head(a):
        return jnp.repeat(a, HEAD, axis=-1)

    def lane_pad(a):
        return jnp.pad(a, ((0, 0), (0, DT_PAD - N_HEADS)))

    reps = CH // SAMPLE_SEQ
    return {
        "gpre": norm_pre, "gpost": norm_post, "alng": a_ln_g, "alnb": a_ln_b, "bcb": b_conv_b,
        "blng": b_ln_g, "blnb": b_ln_b, "bpb": b_pw_b, "dcb": d_conv_b, "dtb": lane_pad(d_dt_bias),
        "alog": lane_pad(d_a_log), "dskip": per_head(d_skip), "dng": d_norm_g,
        "sinks": c_sinks, "win": win, "wdt": wdt, "wout": wout,
        "aws": a_ws,
        "awb_p": per_head(jnp.swapaxes(a_wb, 1, 2)),
        "awb_s": per_head(jnp.tile(jnp.swapaxes(a_wb[:, :, :SAMPLE_SEQ], 1, 2), (1, reps, 1))),
        "bcw": b_conv_w, "bpw": b_pw_w.astype(BF16), "dcw": d_conv_w,
    }


PROMPT_ROWS = 512
PROMPT_SUBBLOCKS = 2
SAMPLE_GROUP = CH // SAMPLE_SEQ


def kernel(x_prompt, x_sample, cache_win_k, cache_win_v, state_conv_b, state_conv_d, state_ssm, norm_pre, norm_post, w_in, w_out, a_ln_g, a_ln_b, a_ws, a_wb, b_conv_w, b_conv_b, b_ln_g, b_ln_b, b_pw_w, b_pw_b, c_sinks, d_conv_w, d_conv_b, d_dt_bias, d_a_log, d_skip, d_norm_g):
    depth = w_in.shape[0]
    batch, seq, _ = x_prompt.shape
    n_seq, dec_seq, _ = x_sample.shape
    assert dec_seq == SAMPLE_SEQ and cache_win_k.shape[2] == CH and seq % PROMPT_ROWS == 0
    assert n_seq % SAMPLE_GROUP == 0
    p = _prepare_params(norm_pre, norm_post, w_in, w_out, a_ln_g, a_ln_b, a_ws, a_wb, b_conv_w, b_conv_b,
                        b_ln_g, b_ln_b, b_pw_w, b_pw_b, c_sinks, d_conv_w, d_conv_b, d_dt_bias, d_a_log,
                        d_skip, d_norm_g, SAMPLE_GROUP)

    xp = x_prompt
    per_layer = []
    for i in range(depth):
        xp, *state = _prompt_layer(i, xp, p, PROMPT_ROWS)
        per_layer.append(state)
    pk, pv, pcb, pcd, pssm = (jnp.stack(list(a)) for a in zip(*per_layer))

    n_rows = n_seq * SAMPLE_SEQ
    ys, sk, sv, scb, scd, sssm, sav = _sample_layers(
        x_sample.reshape(n_rows, D_MODEL),
        jnp.transpose(cache_win_k, (0, 1, 3, 4, 2)), jnp.transpose(cache_win_v, (0, 1, 3, 4, 2)),
        jnp.swapaxes(state_conv_b, 1, 2), jnp.swapaxes(state_conv_d, 1, 2),
        state_ssm, p, SAMPLE_GROUP)

    kv_p = (depth, batch, CH, 2, HEAD)
    kv_s = (depth, n_seq, SAMPLE_SEQ, 2, HEAD)
    return (xp, ys.reshape(n_seq, SAMPLE_SEQ, D_MODEL),
            pk.reshape(kv_p), sk.reshape(kv_s), pv.reshape(kv_p), sv.reshape(kv_s),
            pcb, scb.reshape(depth, n_seq, SAMPLE_SEQ, SEG),
            pcd, scd.reshape(depth, n_seq, SAMPLE_SEQ, D_CONV_DIM),
            pssm, sssm, sav.reshape(depth, n_seq, SAMPLE_SEQ, SEG))
```

```python
import functools

import jax
import jax.numpy as jnp
from jax import lax
from jax.experimental import pallas as pl
from jax.experimental.pallas import tpu as pltpu

F32 = jnp.float32
BF16 = jnp.bfloat16
EPS = 1e-6

D_MODEL = 1024
LANES = 128
CH = 128
HEAD = 64
N_HEADS = 4
SEG = 256
KV_W = 128
B_TAPS = 31
D_TAPS = 4
D_CONV_DIM = 768
D_STATE = 128
DT_PAD = 128
SAMPLE_SEQ = 8
ATT_SUB = 8
HEAD_SHIFT = HEAD.bit_length() - 1
CH_SHIFT = CH.bit_length() - 1
SEQ_SHIFT = SAMPLE_SEQ.bit_length() - 1
SUBLANES = 8
B_HIST = -(-(B_TAPS - 1) // SUBLANES) * SUBLANES
D_HIST = -(-(D_TAPS - 1) // SUBLANES) * SUBLANES
CONV_ROW_TILE = 32

O_AU, O_AV, O_AZ = 0, 256, 512
O_BV, O_BG, O_BZ = 768, 1024, 1280
O_CQ, O_CK, O_CV, O_CZ = 1536, 1792, 1920, 2048
O_DZ, O_DX, O_DT = 2304, 2560, 3328
IN_RAW = 3332
IN_PAD = 3456
PROJ_TILE = 1152
HELD_TILE = 512

ALIBI = tuple(float(2.0 ** (-8.0 * (i + 1) / N_HEADS)) for i in range(N_HEADS))
VMEM_LIMIT_BYTES = 56 * 1024 * 1024


VEC_NAMES = ("gpre", "gpost", "alng", "alnb", "bcb", "blng", "blnb", "bpb", "dcb", "dtb", "alog", "dskip", "dng")


def _layer_rows(layer, refs):
    return [r[pl.ds(layer, 1), :][None] for r in refs]


def _iota(shape, dim):
    return lax.broadcasted_iota(jnp.int32, shape, dim)


def _dot(a, b):
    return jnp.dot(a, b, preferred_element_type=F32)


def _dot_nt(a, b):
    return lax.dot_general(a, b, (((1,), (1,)), ((), ())), preferred_element_type=F32)


def _rms(x, g):
    return x * lax.rsqrt(jnp.mean(x * x, axis=-1, keepdims=True) + EPS) * g


def _ln(x, g, b):
    mu = jnp.mean(x, axis=-1, keepdims=True)
    xc = x - mu
    return xc * lax.rsqrt(jnp.mean(xc * xc, axis=-1, keepdims=True) + EPS) * g + b


def _silu(x):
    return x * jax.nn.sigmoid(x)


def _softplus(x):
    return jnp.maximum(x, 0.0) + jnp.log1p(jnp.exp(-jnp.abs(x)))


def _hi_lo(x):
    hi = x.astype(BF16)
    return hi, (x - hi.astype(F32)).astype(BF16)


def _head_of_lane(width):
    return _iota((CH, width), 1) >> HEAD_SHIFT


def _lane_bcast_heads(cols, width=SEG):
    head = _head_of_lane(width)
    out = jnp.zeros((CH, width), F32)
    for h in range(N_HEADS):
        out = jnp.where(head == h, cols[:, h:h + 1], out)
    return out


def _chunk_masks(seq_rows):
    shift = seq_rows.bit_length() - 1
    r = _iota((CH, CH), 0)
    c = _iota((CH, CH), 1)
    same = (r >> shift) == (c >> shift)
    causal = same & (c <= r)
    return same, causal


def _after(v):
    return jnp.minimum(jnp.abs(v[0:1, 0:1]), 0.0).astype(BF16)


def _project(x, gpre, win_ref, wdt_ref, proj, after=None):
    h = _rms(x, gpre).astype(BF16)
    tile = PROJ_TILE if after is None else HELD_TILE
    bounds = [(n0, min(n0 + tile, O_DT)) for n0 in range(0, O_DT, tile)]
    if after is None:
        after = [None] * (len(bounds) + 1)
    assert len(after) == len(bounds) + 1
    for (n0, n1), zero in zip(bounds, after):
        proj[:, n0:n1] = _dot(h if zero is None else h + zero, win_ref[0, :, n0:n1])
    proj[:, O_DT:O_DT + DT_PAD] = _dot(h if after[-1] is None else h + after[-1], wdt_ref[0])


def _masked_ws(ws_ref, causal, seq_rows):
    if seq_rows < CH:
        r = _iota((CH, CH), 0)
        c = _iota((CH, CH), 1)
        lane_tile = jnp.where((r < seq_rows) & ((c & (seq_rows - 1)) == r), 1.0, 0.0).astype(BF16)
    out = []
    for h in range(N_HEADS):
        if seq_rows == CH:
            w = ws_ref[0, h]
        else:
            top = _dot(ws_ref[0, h, 0:seq_rows, :].astype(BF16), lane_tile)
            w = jnp.concatenate([top] * (CH // seq_rows), axis=0)
        out.append(jnp.where(causal, w, 0.0).astype(BF16))
    return out


def _mixer_a(u_raw, v_raw, z, ln_g, ln_b, ws_masked, wb):
    u = jax.nn.gelu(u_raw)
    v = _ln(jax.nn.gelu(v_raw), ln_g, ln_b)
    vb = v.astype(BF16)
    head = _head_of_lane(SEG)
    s = jnp.zeros((CH, SEG), F32)
    for h in range(N_HEADS):
        s = jnp.where(head == h, _dot(ws_masked[h], vb), s)
    return u * (s + wb) * _silu(z), v


def _mixer_b_post(c, z, ln_g, ln_b, pw_ref, pw_b):
    y = _dot(_silu(_ln(c, ln_g, ln_b)).astype(BF16), pw_ref[0]) + pw_b
    return y * _silu(z)


def _attn_bias(rel, ok, slope):
    return jnp.where(ok, -slope * rel.astype(F32), -jnp.inf)


def _attn(parts, sink):
    scores = [qk + bias for qk, bias, _ in parts]
    same_shape = all(s.shape == scores[0].shape for s in scores)

    def row_reduce(xs, combine, reduce):
        if same_shape:
            xs = [functools.reduce(combine, xs)]
        return functools.reduce(combine, [reduce(x, axis=-1, keepdims=True) for x in xs])

    m = jnp.maximum(sink, row_reduce(scores, jnp.maximum, jnp.max))
    probs = [jnp.exp(s - m) for s in scores]
    den = jnp.exp(sink - m) + row_reduce(probs, jnp.add, jnp.sum)
    out = functools.reduce(jnp.add, [pv(p) for p, (_, _, pv) in zip(probs, parts)])
    return out / den


def _swap_inner_heads(x):
    t0, t1 = x[:, 0:KV_W], x[:, KV_W:2 * KV_W]
    lo = _iota(t0.shape, 1) < HEAD
    return jnp.concatenate([jnp.where(lo, t0, pltpu.roll(t1, HEAD, 1)),
                            jnp.where(lo, pltpu.roll(t0, HEAD, 1), t1)], axis=1)


def _attn_q_rows(q, kvh):
    lane = _iota((q.shape[0], KV_W), 1)
    keep = (lane < HEAD) if kvh == 0 else (lane >= HEAD)
    t0 = jnp.where(keep, q[:, 0:KV_W], 0.0)
    t1 = jnp.where(keep, q[:, KV_W:2 * KV_W], 0.0)
    return jnp.concatenate([t0, t1], axis=0).astype(BF16)


def _attn_merge(o0, o1):
    half = o0.shape[0] // 2
    lo = _iota((half, KV_W), 1) < HEAD
    return jnp.concatenate([jnp.where(lo, o0[:half], o1[:half]),
                            jnp.where(lo, o0[half:], o1[half:])], axis=1)


def _head_cols(sink_ref, layer, kvh, rows):
    half = rows // 2
    first = _iota((rows, 1), 0) < half
    slope = jnp.where(first, ALIBI[2 * kvh], ALIBI[2 * kvh + 1])
    sink = jnp.where(first, sink_ref[layer, 2 * kvh], sink_ref[layer, 2 * kvh + 1])
    return slope, sink


def _sum_matrices(same, causal):
    return jnp.where(same, 1.0, 0.0).astype(BF16), jnp.where(causal, 1.0, 0.0).astype(BF16)


def _ssd_intra(xbc, dt_raw, dtb, alog, same_b, tri_b, causal):
    xs = xbc[:, 0:SEG]
    bm_b = xbc[:, SEG:2 * SEG].astype(BF16)
    cm_b = xbc[:, 2 * SEG:3 * SEG].astype(BF16)
    xs_b = xs.astype(BF16)
    dt = _softplus(dt_raw + dtb)
    da_hi, da_lo = _hi_lo(dt * (-jnp.exp(alog)))
    acum = _dot(tri_b, da_hi) + _dot(tri_b, da_lo)
    alast = _dot(same_b, da_hi) + _dot(same_b, da_lo)
    acum_t = acum.T
    dt_t = dt.T
    head = _head_of_lane(SEG)
    y = jnp.zeros((CH, SEG), F32)
    for g in range(2):
        sc = _dot_nt(cm_b[:, g * D_STATE:(g + 1) * D_STATE], bm_b[:, g * D_STATE:(g + 1) * D_STATE])
        for hh in range(2):
            h = 2 * g + hh
            seg = acum[:, h:h + 1] - acum_t[h:h + 1, :]
            decay = jnp.exp(jnp.where(causal, seg, -jnp.inf))
            m = (sc * decay * dt_t[h:h + 1, :]).astype(BF16)
            y = jnp.where(head == h, _dot(m, xs_b), y)
    return xs, xbc[:, SEG:2 * SEG], cm_b, dt, acum, alast, y


def _ssd_finish(y, xs, z, dskip, dng):
    y = (y + dskip * xs) * _silu(z)
    parts = []
    for g in range(2):
        yg = y[:, g * D_STATE:(g + 1) * D_STATE]
        parts.append(_rms(yg, dng[:, g * D_STATE:(g + 1) * D_STATE]))
    return jnp.concatenate(parts, axis=1)


def _finish_layer(x, mix_s, wout_ref, gpost):
    return x + _rms(_dot(mix_s[...], wout_ref[0]), gpost)


def _prompt_kernel(layer, rows,
                   sink_ref, x_ref, *refs):
    vec_refs, refs = refs[:len(VEC_NAMES)], refs[len(VEC_NAMES):]
    (win, wdt, wout, aws, awb, bcw, bpw, dcw,
     y_ref, kn_ref, vn_ref, cbn_ref, cdn_ref, ssm_ref,
     proj_s, glu_s, xbc_s, conv_s, kprev_s, vprev_s, h_s, mix_s, bias_s) = refs
    gpre, gpost, alng, alnb, bcb, blng, blnb, bpb, dcb, dtb, alog, dskip, dng = _layer_rows(layer, vec_refs)
    j = pl.program_id(1)
    n_chunks = rows // CH
    b_hist, d_hist = B_HIST, D_HIST

    @pl.when(j == 0)
    def _():
        glu_s[:, 0:b_hist, :] = jnp.zeros((SEG // LANES, b_hist, LANES), F32)
        xbc_s[:, 0:d_hist, :] = jnp.zeros((D_CONV_DIM // LANES, d_hist, LANES), F32)
        kprev_s[...] = jnp.zeros((CH, KV_W), BF16)
        vprev_s[...] = jnp.zeros((CH, KV_W), BF16)
        h_s[...] = jnp.zeros((N_HEADS * HEAD, D_STATE), F32)
        row2 = _iota((2 * CH, CH), 0) & (CH - 1)
        col2 = _iota((2 * CH, CH), 1)
        for kvh in range(2):
            slope, _ = _head_cols(sink_ref, layer, kvh, 2 * CH)
            bias_s[2 * kvh] = _attn_bias(row2 + CH - col2, col2 >= row2, slope)
            bias_s[2 * kvh + 1] = _attn_bias(row2 - col2, col2 <= row2, slope)

    @pl.when(j > 0)
    def _():
        glu_s[:, 0:b_hist, :] = glu_s[:, rows:rows + b_hist, :]
        xbc_s[:, 0:d_hist, :] = xbc_s[:, rows:rows + d_hist, :]

    def project(lo, hi, after=None):
        _project(x_ref[0, lo:hi], gpre[0], win, wdt, proj_s.at[lo:hi], after)

    def conv_inputs(lo, hi):
        for c in range(SEG // LANES):
            glu_s[c, b_hist + lo:b_hist + hi, :] = (
                proj_s[lo:hi, O_BV + c * LANES:O_BV + (c + 1) * LANES]
                * jax.nn.sigmoid(proj_s[lo:hi, O_BG + c * LANES:O_BG + (c + 1) * LANES]))
        for c in range(D_CONV_DIM // LANES):
            xbc_s[c, d_hist + lo:d_hist + hi, :] = proj_s[lo:hi, O_DX + c * LANES:O_DX + (c + 1) * LANES]

    def conv(buf, hist, n_tiles, w_ref, b_ref, n_taps, out_col, act, lo, hi):
        tile = CONV_ROW_TILE
        for c in range(n_tiles):
            cs = slice(c * LANES, (c + 1) * LANES)
            for r0 in range(lo, hi, tile):
                first = r0 + hist - (n_taps - 1)
                acc = buf[c, pl.ds(first, tile, stride=1), :] * w_ref[0, 0:1, cs]
                for k in range(1, n_taps):
                    acc = acc + buf[c, pl.ds(first + k, tile, stride=1), :] * w_ref[0, k:k + 1, cs]
                conv_s[r0:r0 + tile, out_col + c * LANES:out_col + (c + 1) * LANES] = act(acc + b_ref[0, :, cs])

    def convs(lo, hi):
        conv(glu_s, b_hist, SEG // LANES, bcw, bcb, B_TAPS, 0, lambda a: a, lo, hi)
        conv(xbc_s, d_hist, D_CONV_DIM // LANES, dcw, dcb, D_TAPS, SEG, _silu, lo, hi)

    same, causal = _chunk_masks(CH)
    same_b, tri_b = _sum_matrices(same, causal)
    ws_masked = _masked_ws(aws, causal, CH)
    kv_carry = [kprev_s[...], vprev_s[...]]

    def mixers(c):
        rs = slice(c * CH, (c + 1) * CH)
        k_prev, v_prev = kv_carry

        ya, _ = _mixer_a(proj_s[rs, O_AU:O_AU + SEG], proj_s[rs, O_AV:O_AV + SEG], proj_s[rs, O_AZ:O_AZ + SEG],
                         alng[0], alnb[0], ws_masked, awb[0])
        mix_s[rs, 0:SEG] = ya.astype(BF16)

        yb = _mixer_b_post(conv_s[rs, 0:SEG], proj_s[rs, O_BZ:O_BZ + SEG], blng[0], blnb[0], bpw, bpb[0])
        mix_s[rs, SEG:2 * SEG] = yb.astype(BF16)

        q = _swap_inner_heads(proj_s[rs, O_CQ:O_CQ + SEG] * (HEAD ** -0.5))
        k_cur = proj_s[rs, O_CK:O_CK + KV_W].astype(BF16)
        v_cur = proj_s[rs, O_CV:O_CV + KV_W].astype(BF16)
        outs = []
        for kvh in range(2):
            _, sink = _head_cols(sink_ref, layer, kvh, 2 * CH)
            q_rows = _attn_q_rows(q, kvh)
            bias_prev = bias_s[2 * kvh]
            if c == 0:
                bias_prev = jnp.where(j > 0, bias_prev, -jnp.inf)
            outs.append(_attn([(_dot_nt(q_rows, k_prev), bias_prev, lambda p, v=v_prev: _dot(p.astype(BF16), v)),
                               (_dot_nt(q_rows, k_cur), bias_s[2 * kvh + 1],
                                lambda p, v=v_cur: _dot(p.astype(BF16), v))],
                              sink))
        yc = _swap_inner_heads(_attn_merge(outs[0], outs[1])) * _silu(proj_s[rs, O_CZ:O_CZ + SEG])
        mix_s[rs, 2 * SEG:3 * SEG] = yc.astype(BF16)
        kv_carry[:] = [k_cur, v_cur]

        xs, bm, cm_b, dt, acum, alast, y = _ssd_intra(conv_s[rs, SEG:SEG + D_CONV_DIM],
                                                      proj_s[rs, O_DT:O_DT + DT_PAD], dtb[0], alog[0],
                                                      same_b, tri_b, causal)
        h_prev = h_s[...]
        h_prev_b = h_prev.astype(BF16)
        y_inter = jnp.concatenate(
            [_dot_nt(cm_b[:, g * D_STATE:(g + 1) * D_STATE], h_prev_b[g * 2 * HEAD:(g + 1) * 2 * HEAD])
             for g in range(2)], axis=1)
        y = y + y_inter * _lane_bcast_heads(jnp.exp(acum))
        wcol = jnp.exp(alast - acum) * dt
        carry = jnp.exp(alast)
        xs_t = xs.T.astype(BF16)
        for h in range(N_HEADS):
            g = h // 2
            wb_h = (bm[:, g * D_STATE:(g + 1) * D_STATE] * wcol[:, h:h + 1]).astype(BF16)
            hs = slice(h * HEAD, (h + 1) * HEAD)
            h_s[hs, :] = h_prev[hs] * carry[0:HEAD, h:h + 1] + _dot(xs_t[hs], wb_h)
        yd = _ssd_finish(y, xs, proj_s[rs, O_DZ:O_DZ + SEG], dskip[0], dng[0])
        mix_s[rs, 3 * SEG:4 * SEG] = yd.astype(BF16)
        return [_after(v) for v in (ya, yb, yc, yd)]

    def finish(lo, hi):
        y_ref[0, lo:hi] = x_ref[0, lo:hi] + _rms(_dot(mix_s[lo:hi], wout[0]), gpost[0])

    blk = rows // PROMPT_SUBBLOCKS
    assert blk == 2 * CH
    project(0, blk)
    for lo in range(0, rows, blk):
        hi = lo + blk
        conv_inputs(lo, hi)
        convs(lo, hi)
        start = _after(conv_s[lo:lo + SUBLANES, 0:LANES])
        first = mixers(lo // CH)
        second = mixers(lo // CH + 1)
        if hi < rows:
            project(hi, hi + blk, [start, start, first[0], first[1], first[2], first[3], second[0], second[1]])
        finish(lo, hi)
    kprev_s[...] = kv_carry[0]
    vprev_s[...] = kv_carry[1]

    @pl.when(j == pl.num_programs(1) - 1)
    def _():
        kn_ref[0] = proj_s[rows - CH:rows, O_CK:O_CK + KV_W]
        vn_ref[0] = proj_s[rows - CH:rows, O_CV:O_CV + KV_W]
        for c in range(SEG // LANES):
            cbn_ref[0, :, c * LANES:(c + 1) * LANES] = glu_s[c, b_hist + rows - (B_TAPS - 1):b_hist + rows, :]
        for c in range(D_CONV_DIM // LANES):
            cdn_ref[0, :, c * LANES:(c + 1) * LANES] = xbc_s[c, d_hist + rows - (D_TAPS - 1):d_hist + rows, :]
        ssm_ref[0] = h_s[...].reshape(N_HEADS, HEAD, D_STATE)


def _prompt_layer(layer, x, p, rows):
    batch, seq, _ = x.shape
    nb = seq // rows

    def wspec(a):
        blk = (1,) + a.shape[1:]
        zeros = (0,) * (a.ndim - 1)
        return pl.BlockSpec(blk, lambda b, j: (layer,) + zeros)

    vectors = [p[k] for k in VEC_NAMES]
    weights = [p[k] for k in ("win", "wdt", "wout", "aws", "awb_p", "bcw", "bpw", "dcw")]
    x_spec = pl.BlockSpec((1, rows, D_MODEL), lambda b, j: (b, j, 0))
    in_specs = ([pl.BlockSpec(memory_space=pltpu.SMEM), x_spec]
                + [pl.BlockSpec(a.shape, lambda b, j: (0, 0)) for a in vectors] + [wspec(a) for a in weights])
    out_shape = (jax.ShapeDtypeStruct((batch, seq, D_MODEL), F32),
                 jax.ShapeDtypeStruct((batch, CH, KV_W), F32),
                 jax.ShapeDtypeStruct((batch, CH, KV_W), F32),
                 jax.ShapeDtypeStruct((batch, B_TAPS - 1, SEG), F32),
                 jax.ShapeDtypeStruct((batch, D_TAPS - 1, D_CONV_DIM), F32),
                 jax.ShapeDtypeStruct((batch, N_HEADS, HEAD, D_STATE), F32))
    out_specs = (x_spec,
                 pl.BlockSpec((1, CH, KV_W), lambda b, j: (b, 0, 0)),
                 pl.BlockSpec((1, CH, KV_W), lambda b, j: (b, 0, 0)),
                 pl.BlockSpec((1, B_TAPS - 1, SEG), lambda b, j: (b, 0, 0)),
                 pl.BlockSpec((1, D_TAPS - 1, D_CONV_DIM), lambda b, j: (b, 0, 0)),
                 pl.BlockSpec((1, N_HEADS, HEAD, D_STATE), lambda b, j: (b, 0, 0, 0)))
    scratch = [pltpu.VMEM((rows, IN_PAD), F32),
               pltpu.VMEM((SEG // LANES, rows + B_HIST, LANES), F32),
               pltpu.VMEM((D_CONV_DIM // LANES, rows + D_HIST, LANES), F32),
               pltpu.VMEM((rows, SEG + D_CONV_DIM), F32),
               pltpu.VMEM((CH, KV_W), BF16),
               pltpu.VMEM((CH, KV_W), BF16),
               pltpu.VMEM((N_HEADS * HEAD, D_STATE), F32),
               pltpu.VMEM((rows, D_MODEL), BF16),
               pltpu.VMEM((4, 2 * CH, CH), F32)]
    return pl.pallas_call(
        functools.partial(_prompt_kernel, layer, rows),
        grid=(batch, nb),
        in_specs=in_specs, out_specs=out_specs, out_shape=out_shape, scratch_shapes=scratch,
        compiler_params=pltpu.CompilerParams(dimension_semantics=("arbitrary", "arbitrary"),
                                             vmem_limit_bytes=VMEM_LIMIT_BYTES),
        name=f"prompt_layer{layer}",
    )(p["sinks"], x, *vectors, *weights)


def _sample_kernel(group,
                   sink_ref, x_ref, kc_ref, vc_ref, cbh_ref, cdh_ref, ssm0_ref, *refs):
    vec_refs, refs = refs[:len(VEC_NAMES)], refs[len(VEC_NAMES):]
    (win, wdt, wout, aws, awb, bcw, bpw, dcw,
     y_ref, kn_ref, vn_ref, cbn_ref, cdn_ref, ssm_ref, avn_ref,
     xin_s, proj_s, new_s, conv_s, mix_s, bias_a_s, bias_b_s) = refs
    layer = pl.program_id(0)
    gpre, gpost, alng, alnb, bcb, blng, blnb, bpb, dcb, dtb, alog, dskip, dng = _layer_rows(layer, vec_refs)
    s = pl.program_id(1)
    rows = group * SAMPLE_SEQ
    row0 = pl.multiple_of(s * rows, rows)

    @pl.when(layer == 0)
    def _():
        xin_s[...] = x_ref[...]

    @pl.when(layer > 0)
    def _():
        xin_s[...] = y_ref[pl.ds(row0, rows), :]

    x = xin_s[...]
    _project(x, gpre[0], win, wdt, proj_s)
    same, causal = _chunk_masks(SAMPLE_SEQ)
    same_b, tri_b = _sum_matrices(same, causal)

    ya, v_rows = _mixer_a(proj_s[:, O_AU:O_AU + SEG], proj_s[:, O_AV:O_AV + SEG], proj_s[:, O_AZ:O_AZ + SEG],
                          alng[0], alnb[0], _masked_ws(aws, causal, SAMPLE_SEQ), awb[0])
    mix_s[:, 0:SEG] = ya.astype(BF16)
    avn_ref[0] = v_rows

    glu = proj_s[:, O_BV:O_BV + SEG] * jax.nn.sigmoid(proj_s[:, O_BG:O_BG + SEG])
    cbn_ref[0] = glu
    cdn_ref[0] = proj_s[:, O_DX:O_DX + D_CONV_DIM]

    def conv_tiles(hist_ref, new, w_ref, b_ref, n_taps, tile0, act):
        n_hist = n_taps - 1
        n_tiles = new.shape[1] // LANES
        for c in range(n_tiles):
            cs = slice(c * LANES, (c + 1) * LANES)
            new_s[c] = new[:, cs]
            slabs = [hist_ref[0, t, :, cs] for t in range(n_hist)]
            slabs += [new_s[c, pl.ds(t, group, stride=SAMPLE_SEQ), :] for t in range(SAMPLE_SEQ)]
            for t in range(SAMPLE_SEQ):
                acc = slabs[t] * w_ref[0, 0:1, cs]
                for k in range(1, n_taps):
                    acc = acc + slabs[t + k] * w_ref[0, k:k + 1, cs]
                conv_s[tile0 + c, pl.ds(t, group, stride=SAMPLE_SEQ), :] = act(acc + b_ref[0, :, cs])

    conv_tiles(cbh_ref, glu, bcw, bcb, B_TAPS, 0, lambda a: a)
    conv_tiles(cdh_ref, proj_s[:, O_DX:O_DX + D_CONV_DIM], dcw, dcb, D_TAPS, SEG // LANES, _silu)
    conv_b = jnp.concatenate([conv_s[c] for c in range(SEG // LANES)], axis=1)
    conv_d = jnp.concatenate([conv_s[c] for c in range(SEG // LANES, (SEG + D_CONV_DIM) // LANES)], axis=1)

    yb = _mixer_b_post(conv_b, proj_s[:, O_BZ:O_BZ + SEG], blng[0], blnb[0], bpw, bpb[0])
    mix_s[:, SEG:2 * SEG] = yb.astype(BF16)

    kn_ref[0] = proj_s[:, O_CK:O_CK + KV_W]
    vn_ref[0] = proj_s[:, O_CV:O_CV + KV_W]
    sub_rows = ATT_SUB * SAMPLE_SEQ

    n_groups = 2 * ATT_SUB

    def own_tiles(full):
        return jnp.concatenate(
            [full[i * SAMPLE_SEQ:(i + 1) * SAMPLE_SEQ, (i % ATT_SUB) * CH:(i % ATT_SUB + 1) * CH]
             for i in range(n_groups)], axis=0)

    def spread_tiles(p):
        zero = jnp.zeros((SAMPLE_SEQ, CH), F32)
        return jnp.concatenate(
            [jnp.concatenate([p[i * SAMPLE_SEQ:(i + 1) * SAMPLE_SEQ] if j == i % ATT_SUB else zero
                              for j in range(ATT_SUB)], axis=1) for i in range(n_groups)], axis=0)

    @pl.when((layer == 0) & (s == 0))
    def _():
        q_t = _iota((2 * sub_rows, CH), 0) & (SAMPLE_SEQ - 1)
        key_pos = _iota((2 * sub_rows, CH), 1)
        rel_a = q_t + CH - key_pos
        ok_a = key_pos >= q_t
        row_n = _iota((2 * sub_rows, sub_rows), 0) & (sub_rows - 1)
        col_n = _iota((2 * sub_rows, sub_rows), 1)
        rel_b = (row_n & (SAMPLE_SEQ - 1)) - (col_n & (SAMPLE_SEQ - 1))
        ok_b = ((col_n >> SEQ_SHIFT) == (row_n >> SEQ_SHIFT)) & (rel_b >= 0)
        for kvh in range(2):
            slope, _ = _head_cols(sink_ref, layer, kvh, 2 * sub_rows)
            bias_a_s[kvh] = _attn_bias(rel_a, ok_a, slope)
            bias_b_s[kvh] = _attn_bias(rel_b, ok_b, slope)

    for u in range(group // ATT_SUB):
        us = slice(u * sub_rows, (u + 1) * sub_rows)
        q = proj_s[us, O_CQ:O_CQ + SEG] * (HEAD ** -0.5)
        outs = []
        for kvh in range(2):
            hs = [slice((2 * kvh + i) * HEAD, (2 * kvh + i + 1) * HEAD) for i in range(2)]
            q_rows = jnp.concatenate([q[:, hs[0]], q[:, hs[1]]], axis=0).astype(BF16)
            k_t = jnp.concatenate([kc_ref[0, u * ATT_SUB + i, kvh] for i in range(ATT_SUB)], axis=1).astype(BF16)
            v_t = jnp.concatenate([vc_ref[0, u * ATT_SUB + i, kvh] for i in range(ATT_SUB)], axis=1).astype(BF16)
            k_new = proj_s[us, O_CK + kvh * HEAD:O_CK + (kvh + 1) * HEAD].astype(BF16)
            v_new = proj_s[us, O_CV + kvh * HEAD:O_CV + (kvh + 1) * HEAD].astype(BF16)
            _, sink = _head_cols(sink_ref, layer, kvh, 2 * sub_rows)
            o = _attn([(own_tiles(_dot(q_rows, k_t)), bias_a_s[kvh],
                        lambda p, v_t=v_t: _dot_nt(spread_tiles(p).astype(BF16), v_t)),
                       (_dot_nt(q_rows, k_new), bias_b_s[kvh],
                        lambda p, v_new=v_new: _dot(p.astype(BF16), v_new))], sink)
            outs += [o[:sub_rows], o[sub_rows:]]
        yc = jnp.concatenate(outs, axis=1)
        mix_s[us, 2 * SEG:3 * SEG] = (yc * _silu(proj_s[us, O_CZ:O_CZ + SEG])).astype(BF16)

    xs, bm, cm_b, dt, acum, alast, y = _ssd_intra(conv_d, proj_s[:, O_DT:O_DT + DT_PAD], dtb[0], alog[0],
                                                  same_b, tri_b, causal)
    state_rows = group * HEAD
    own = (_iota((state_rows, CH), 0) >> HEAD_SHIFT) == (_iota((state_rows, CH), 1) >> SEQ_SHIFT)
    wcol = jnp.exp(alast - acum) * dt
    carry = jnp.exp(alast)
    xs_t = xs.T
    y_inter_t = []
    for h in range(N_HEADS):
        g = h // 2
        gs = slice(g * D_STATE, (g + 1) * D_STATE)
        h0 = ssm0_ref[0, :, h].reshape(state_rows, D_STATE)
        full = jnp.where(own, _dot_nt(h0.astype(BF16), cm_b[:, gs]), 0.0)
        y_inter_t.append(jnp.sum(full.reshape(group, HEAD, CH), axis=0))
        xt_h = xs_t[h * HEAD:(h + 1) * HEAD]
        lhs = jnp.where(own, jnp.concatenate([xt_h] * group, axis=0), 0.0).astype(BF16)
        wb_h = (bm[:, gs] * wcol[:, h:h + 1]).astype(BF16)
        carry_h = jnp.broadcast_to(carry[:, h:h + 1], (CH, D_STATE)).reshape(group, SAMPLE_SEQ, D_STATE)
        carry_h = jnp.concatenate([carry_h] * (HEAD // SAMPLE_SEQ), axis=1).reshape(state_rows, D_STATE)
        ssm_ref[0, :, h] = (h0 * carry_h + _dot(lhs, wb_h)).reshape(group, HEAD, D_STATE)
    y_inter = jnp.concatenate(y_inter_t, axis=0).T
    y = y + y_inter * _lane_bcast_heads(jnp.exp(acum))
    mix_s[:, 3 * SEG:4 * SEG] = _ssd_finish(y, xs, proj_s[:, O_DZ:O_DZ + SEG], dskip[0], dng[0]).astype(BF16)

    y_ref[pl.ds(row0, rows), :] = _finish_layer(x, mix_s, wout, gpost[0])


def _sample_layers(x, kc, vc, cbh, cdh, ssm0, p, group):
    depth = p["win"].shape[0]
    n_seq = ssm0.shape[1]
    n_rows = n_seq * SAMPLE_SEQ
    rows = group * SAMPLE_SEQ
    ns = n_seq // group

    def wspec(a):
        blk = (1,) + a.shape[1:]
        zeros = (0,) * (a.ndim - 1)
        return pl.BlockSpec(blk, lambda i, s: (i,) + zeros)

    def rspec(width, per_seq):
        return pl.BlockSpec((1, group * per_seq, width), lambda i, s: (i, s, 0))

    vectors = [p[k] for k in VEC_NAMES]
    weights = [p[k] for k in ("win", "wdt", "wout", "aws", "awb_s", "bcw", "bpw", "dcw")]
    ssm_spec = pl.BlockSpec((1, group, N_HEADS, HEAD, D_STATE), lambda i, s: (i, s, 0, 0, 0))
    cache_spec = pl.BlockSpec((1, group, 2, HEAD, CH), lambda i, s: (i, s, 0, 0, 0))

    def hist_spec(n_hist, width):
        return pl.BlockSpec((1, n_hist, group, width), lambda i, s: (i, 0, s, 0))

    in_specs = ([pl.BlockSpec(memory_space=pltpu.SMEM),
                 pl.BlockSpec((rows, D_MODEL), lambda i, s: (s, 0)),
                 cache_spec, cache_spec,
                 hist_spec(B_TAPS - 1, SEG), hist_spec(D_TAPS - 1, D_CONV_DIM), ssm_spec]
                + [pl.BlockSpec(a.shape, lambda i, s: (0, 0)) for a in vectors] + [wspec(a) for a in weights])
    out_shape = (jax.ShapeDtypeStruct((n_rows, D_MODEL), F32),
                 jax.ShapeDtypeStruct((depth, n_rows, KV_W), F32),
                 jax.ShapeDtypeStruct((depth, n_rows, KV_W), F32),
                 jax.ShapeDtypeStruct((depth, n_rows, SEG), F32),
                 jax.ShapeDtypeStruct((depth, n_rows, D_CONV_DIM), F32),
                 jax.ShapeDtypeStruct((depth, n_seq, N_HEADS, HEAD, D_STATE), F32),
                 jax.ShapeDtypeStruct((depth, n_rows, SEG), F32))
    out_specs = (pl.BlockSpec((n_rows, D_MODEL), lambda i, s: (0, 0)),
                 rspec(KV_W, SAMPLE_SEQ), rspec(KV_W, SAMPLE_SEQ),
                 rspec(SEG, SAMPLE_SEQ), rspec(D_CONV_DIM, SAMPLE_SEQ), ssm_spec,
                 rspec(SEG, SAMPLE_SEQ))
    scratch = [pltpu.VMEM((rows, D_MODEL), F32),
               pltpu.VMEM((rows, IN_PAD), F32),
               pltpu.VMEM((D_CONV_DIM // LANES, rows, LANES), F32),
               pltpu.VMEM(((SEG + D_CONV_DIM) // LANES, rows, LANES), F32),
               pltpu.VMEM((rows, D_MODEL), BF16),
               pltpu.VMEM((2, 2 * ATT_SUB * SAMPLE_SEQ, CH), F32),
               pltpu.VMEM((2, 2 * ATT_SUB * SAMPLE_SEQ, ATT_SUB * SAMPLE_SEQ), F32)]
    return pl.pallas_call(
        functools.partial(_sample_kernel, group),
        grid=(depth, ns),
        in_specs=in_specs, out_specs=out_specs, out_shape=out_shape, scratch_shapes=scratch,
        compiler_params=pltpu.CompilerParams(dimension_semantics=("arbitrary", "arbitrary"),
                                             vmem_limit_bytes=VMEM_LIMIT_BYTES),
        name="sample_layers",
    )(p["sinks"], x, kc, vc, cbh, cdh, ssm0, *vectors, *weights)


def _prepare_params(norm_pre, norm_post, w_in, w_out, a_ln_g, a_ln_b, a_ws, a_wb, b_conv_w, b_conv_b,
                    b_ln_g, b_ln_b, b_pw_w, b_pw_b, c_sinks, d_conv_w, d_conv_b, d_dt_bias, d_a_log,
                    d_skip, d_norm_g, group):
    depth = w_in.shape[0]
    win = w_in.astype(BF16)
    wdt = jnp.pad(w_in[:, :, O_DT:], ((0, 0), (0, 0), (0, DT_PAD - N_HEADS))).astype(BF16)
    wout = w_out.astype(BF16)

    def per_head(a):
        return jnp.repeat(a, HEAD, axis=-1)

    def lane_pad(a):
        return jnp.pad(a, ((0, 0), (0, DT_PAD - N_HEADS)))

    reps = CH // SAMPLE_SEQ
    return {
        "gpre": norm_pre, "gpost": norm_post, "alng": a_ln_g, "alnb": a_ln_b, "bcb": b_conv_b,
        "blng": b_ln_g, "blnb": b_ln_b, "bpb": b_pw_b, "dcb": d_conv_b, "dtb": lane_pad(d_dt_bias),
        "alog": lane_pad(d_a_log), "dskip": per_head(d_skip), "dng": d_norm_g,
        "sinks": c_sinks, "win": win, "wdt": wdt, "wout": wout,
        "aws": a_ws,
        "awb_p": per_head(jnp.swapaxes(a_wb, 1, 2)),
        "awb_s": per_head(jnp.tile(jnp.swapaxes(a_wb[:, :, :SAMPLE_SEQ], 1, 2), (1, reps, 1))),
        "bcw": b_conv_w, "bpw": b_pw_w.astype(BF16), "dcw": d_conv_w,
    }


PROMPT_ROWS = 512
PROMPT_SUBBLOCKS = 2
SAMPLE_GROUP = CH // SAMPLE_SEQ


def kernel(x_prompt, x_sample, cache_win_k, cache_win_v, state_conv_b, state_conv_d, state_ssm, norm_pre, norm_post, w_in, w_out, a_ln_g, a_ln_b, a_ws, a_wb, b_conv_w, b_conv_b, b_ln_g, b_ln_b, b_pw_w, b_pw_b, c_sinks, d_conv_w, d_conv_b, d_dt_bias, d_a_log, d_skip, d_norm_g):
    depth = w_in.shape[0]
    batch, seq, _ = x_prompt.shape
    n_seq, dec_seq, _ = x_sample.shape
    assert dec_seq == SAMPLE_SEQ and cache_win_k.shape[2] == CH and seq % PROMPT_ROWS == 0
    assert n_seq % SAMPLE_GROUP == 0
    p = _prepare_params(norm_pre, norm_post, w_in, w_out, a_ln_g, a_ln_b, a_ws, a_wb, b_conv_w, b_conv_b,
                        b_ln_g, b_ln_b, b_pw_w, b_pw_b, c_sinks, d_conv_w, d_conv_b, d_dt_bias, d_a_log,
                        d_skip, d_norm_g, SAMPLE_GROUP)

    xp = x_prompt
    per_layer = []
    for i in range(depth):
        xp, *state = _prompt_layer(i, xp, p, PROMPT_ROWS)
        per_layer.append(state)
    pk, pv, pcb, pcd, pssm = (jnp.stack(list(a)) for a in zip(*per_layer))

    n_rows = n_seq * SAMPLE_SEQ
    ys, sk, sv, scb, scd, sssm, sav = _sample_layers(
        x_sample.reshape(n_rows, D_MODEL),
        jnp.transpose(cache_win_k, (0, 1, 3, 4, 2)), jnp.transpose(cache_win_v, (0, 1, 3, 4, 2)),
        jnp.swapaxes(state_conv_b, 1, 2), jnp.swapaxes(state_conv_d, 1, 2),
        state_ssm, p, SAMPLE_GROUP)

    kv_p = (depth, batch, CH, 2, HEAD)
    kv_s = (depth, n_seq, SAMPLE_SEQ, 2, HEAD)
    return (xp, ys.reshape(n_seq, SAMPLE_SEQ, D_MODEL),
            pk.reshape(kv_p), sk.reshape(kv_s), pv.reshape(kv_p), sv.reshape(kv_s),
            pcb, scb.reshape(depth, n_seq, SAMPLE_SEQ, SEG),
            pcd, scd.reshape(depth, n_seq, SAMPLE_SEQ, D_CONV_DIM),
            pssm, sssm, sav.reshape(depth, n_seq, SAMPLE_SEQ, SEG))
```

```python
import functools

import jax
import jax.numpy as jnp
from jax import lax
from jax.experimental import pallas as pl
from jax.experimental.pallas import tpu as pltpu

F32 = jnp.float32
BF16 = jnp.bfloat16
EPS = 1e-6

D_MODEL = 1024
LANES = 128
CH = 128
HEAD = 64
N_HEADS = 4
SEG = 256
KV_W = 128
B_TAPS = 31
D_TAPS = 4
D_CONV_DIM = 768
D_STATE = 128
DT_PAD = 128
SAMPLE_SEQ = 8
ATT_SUB = 8
HEAD_SHIFT = HEAD.bit_length() - 1
CH_SHIFT = CH.bit_length() - 1
SEQ_SHIFT = SAMPLE_SEQ.bit_length() - 1
SUBLANES = 8
B_HIST = -(-(B_TAPS - 1) // SUBLANES) * SUBLANES
D_HIST = -(-(D_TAPS - 1) // SUBLANES) * SUBLANES
CONV_ROW_TILE = 32

O_AU, O_AV, O_AZ = 0, 256, 512
O_BV, O_BG, O_BZ = 768, 1024, 1280
O_CQ, O_CK, O_CV, O_CZ = 1536, 1792, 1920, 2048
O_DZ, O_DX, O_DT = 2304, 2560, 3328
IN_RAW = 3332
IN_PAD = 3456
PROJ_TILE = 1152
HELD_TILE = 512

ALIBI = tuple(float(2.0 ** (-8.0 * (i + 1) / N_HEADS)) for i in range(N_HEADS))
VMEM_LIMIT_BYTES = 56 * 1024 * 1024


VEC_NAMES = ("gpre", "gpost", "alng", "alnb", "bcb", "blng", "blnb", "bpb", "dcb", "dtb", "alog", "dskip", "dng")


def _layer_rows(layer, refs):
    return [r[pl.ds(layer, 1), :][None] for r in refs]


def _iota(shape, dim):
    return lax.broadcasted_iota(jnp.int32, shape, dim)


def _dot(a, b):
    return jnp.dot(a, b, preferred_element_type=F32)


def _dot_nt(a, b):
    return lax.dot_general(a, b, (((1,), (1,)), ((), ())), preferred_element_type=F32)


def _rms(x, g):
    return x * lax.rsqrt(jnp.mean(x * x, axis=-1, keepdims=True) + EPS) * g


def _ln(x, g, b):
    mu = jnp.mean(x, axis=-1, keepdims=True)
    xc = x - mu
    return xc * lax.rsqrt(jnp.mean(xc * xc, axis=-1, keepdims=True) + EPS) * g + b


def _silu(x):
    return x * jax.nn.sigmoid(x)


def _softplus(x):
    return jnp.maximum(x, 0.0) + jnp.log1p(jnp.exp(-jnp.abs(x)))


def _hi_lo(x):
    hi = x.astype(BF16)
    return hi, (x - hi.astype(F32)).astype(BF16)


def _head_of_lane(width):
    return _iota((CH, width), 1) >> HEAD_SHIFT


def _lane_bcast_heads(cols, width=SEG):
    head = _head_of_lane(width)
    out = jnp.zeros((CH, width), F32)
    for h in range(N_HEADS):
        out = jnp.where(head == h, cols[:, h:h + 1], out)
    return out


def _chunk_masks(seq_rows):
    shift = seq_rows.bit_length() - 1
    r = _iota((CH, CH), 0)
    c = _iota((CH, CH), 1)
    same = (r >> shift) == (c >> shift)
    causal = same & (c <= r)
    return same, causal


def _after(v):
    return jnp.minimum(jnp.abs(v[0:1, 0:1]), 0.0).astype(BF16)


def _project(x, gpre, win_ref, wdt_ref, proj, after=None):
    h = _rms(x, gpre).astype(BF16)
    tile = PROJ_TILE if after is None else HELD_TILE
    bounds = [(n0, min(n0 + tile, O_DT)) for n0 in range(0, O_DT, tile)]
    if after is None:
        after = [None] * (len(bounds) + 1)
    assert len(after) == len(bounds) + 1
    for (n0, n1), zero in zip(bounds, after):
        proj[:, n0:n1] = _dot(h if zero is None else h + zero, win_ref[0, :, n0:n1])
    proj[:, O_DT:O_DT + DT_PAD] = _dot(h if after[-1] is None else h + after[-1], wdt_ref[0])


def _masked_ws(ws_ref, causal, seq_rows):
    if seq_rows < CH:
        r = _iota((CH, CH), 0)
        c = _iota((CH, CH), 1)
        lane_tile = jnp.where((r < seq_rows) & ((c & (seq_rows - 1)) == r), 1.0, 0.0).astype(BF16)
    out = []
    for h in range(N_HEADS):
        if seq_rows == CH:
            w = ws_ref[0, h]
        else:
            top = _dot(ws_ref[0, h, 0:seq_rows, :].astype(BF16), lane_tile)
            w = jnp.concatenate([top] * (CH // seq_rows), axis=0)
        out.append(jnp.where(causal, w, 0.0).astype(BF16))
    return out


def _mixer_a(u_raw, v_raw, z, ln_g, ln_b, ws_masked, wb):
    u = jax.nn.gelu(u_raw)
    v = _ln(jax.nn.gelu(v_raw), ln_g, ln_b)
    vb = v.astype(BF16)
    head = _head_of_lane(SEG)
    s = jnp.zeros((CH, SEG), F32)
    for h in range(N_HEADS):
        s = jnp.where(head == h, _dot(ws_masked[h], vb), s)
    return u * (s + wb) * _silu(z), v


def _mixer_b_post(c, z, ln_g, ln_b, pw_ref, pw_b):
    y = _dot(_silu(_ln(c, ln_g, ln_b)).astype(BF16), pw_ref[0]) + pw_b
    return y * _silu(z)


def _attn_bias(rel, ok, slope):
    return jnp.where(ok, -slope * rel.astype(F32), -jnp.inf)


def _attn(parts, sink):
    scores = [qk + bias for qk, bias, _ in parts]
    same_shape = all(s.shape == scores[0].shape for s in scores)

    def row_reduce(xs, combine, reduce):
        if same_shape:
            xs = [functools.reduce(combine, xs)]
        return functools.reduce(combine, [reduce(x, axis=-1, keepdims=True) for x in xs])

    m = jnp.maximum(sink, row_reduce(scores, jnp.maximum, jnp.max))
    probs = [jnp.exp(s - m) for s in scores]
    den = jnp.exp(sink - m) + row_reduce(probs, jnp.add, jnp.sum)
    out = functools.reduce(jnp.add, [pv(p.astype(BF16)) for p, (_, _, pv) in zip(probs, parts)])
    return out / den


def _swap_inner_heads(x):
    t0, t1 = x[:, 0:KV_W], x[:, KV_W:2 * KV_W]
    lo = _iota(t0.shape, 1) < HEAD
    return jnp.concatenate([jnp.where(lo, t0, pltpu.roll(t1, HEAD, 1)),
                            jnp.where(lo, pltpu.roll(t0, HEAD, 1), t1)], axis=1)


def _attn_q_rows(q, kvh):
    lane = _iota((q.shape[0], KV_W), 1)
    keep = (lane < HEAD) if kvh == 0 else (lane >= HEAD)
    t0 = jnp.where(keep, q[:, 0:KV_W], 0.0)
    t1 = jnp.where(keep, q[:, KV_W:2 * KV_W], 0.0)
    return jnp.concatenate([t0, t1], axis=0).astype(BF16)


def _attn_merge(o0, o1):
    half = o0.shape[0] // 2
    lo = _iota((half, KV_W), 1) < HEAD
    return jnp.concatenate([jnp.where(lo, o0[:half], o1[:half]),
                            jnp.where(lo, o0[half:], o1[half:])], axis=1)


def _head_cols(sink_ref, layer, kvh, rows):
    half = rows // 2
    first = _iota((rows, 1), 0) < half
    slope = jnp.where(first, ALIBI[2 * kvh], ALIBI[2 * kvh + 1])
    sink = jnp.where(first, sink_ref[layer, 2 * kvh], sink_ref[layer, 2 * kvh + 1])
    return slope, sink


def _sum_matrices(same, causal):
    return jnp.where(same, 1.0, 0.0).astype(BF16), jnp.where(causal, 1.0, 0.0).astype(BF16)


def _ssd_intra(xbc, dt_raw, dtb, alog, same_b, tri_b, causal):
    xs = xbc[:, 0:SEG]
    bm_b = xbc[:, SEG:2 * SEG].astype(BF16)
    cm_b = xbc[:, 2 * SEG:3 * SEG].astype(BF16)
    xs_b = xs.astype(BF16)
    dt = _softplus(dt_raw + dtb)
    da_hi, da_lo = _hi_lo(dt * (-jnp.exp(alog)))
    acum = _dot(tri_b, da_hi) + _dot(tri_b, da_lo)
    alast = _dot(same_b, da_hi) + _dot(same_b, da_lo)
    acum_t = acum.T
    dt_t = dt.T
    head = _head_of_lane(SEG)
    y = jnp.zeros((CH, SEG), F32)
    for g in range(2):
        sc = _dot_nt(cm_b[:, g * D_STATE:(g + 1) * D_STATE], bm_b[:, g * D_STATE:(g + 1) * D_STATE])
        for hh in range(2):
            h = 2 * g + hh
            seg = acum[:, h:h + 1] - acum_t[h:h + 1, :]
            decay = jnp.exp(jnp.where(causal, seg, -jnp.inf))
            m = (sc * decay * dt_t[h:h + 1, :]).astype(BF16)
            y = jnp.where(head == h, _dot(m, xs_b), y)
    return xs, xbc[:, SEG:2 * SEG], cm_b, dt, acum, alast, y


def _ssd_finish(y, xs, z, dskip, dng):
    y = (y + dskip * xs) * _silu(z)
    parts = []
    for g in range(2):
        yg = y[:, g * D_STATE:(g + 1) * D_STATE]
        parts.append(_rms(yg, dng[:, g * D_STATE:(g + 1) * D_STATE]))
    return jnp.concatenate(parts, axis=1)


def _finish_layer(x, mix_s, wout_ref, gpost):
    return x + _rms(_dot(mix_s[...], wout_ref[0]), gpost)


def _prompt_kernel(layer, rows,
                   sink_ref, x_ref, *refs):
    vec_refs, refs = refs[:len(VEC_NAMES)], refs[len(VEC_NAMES):]
    (win, wdt, wout, aws, awb, bcw, bpw, dcw,
     y_ref, kn_ref, vn_ref, cbn_ref, cdn_ref, ssm_ref,
     proj_s, glu_s, xbc_s, conv_s, kprev_s, vprev_s, h_s, mix_s, bias_s) = refs
    gpre, gpost, alng, alnb, bcb, blng, blnb, bpb, dcb, dtb, alog, dskip, dng = _layer_rows(layer, vec_refs)
    j = pl.program_id(0)
    blk = rows // PROMPT_SUBBLOCKS
    cpb = blk // CH
    b_hist, d_hist = B_HIST, D_HIST

    @pl.when(j == 0)
    def _():
        glu_s[:, :, 0:b_hist, :] = jnp.zeros((PROMPT_SUBBLOCKS, SEG // LANES, b_hist, LANES), F32)
        xbc_s[:, :, 0:d_hist, :] = jnp.zeros((PROMPT_SUBBLOCKS, D_CONV_DIM // LANES, d_hist, LANES), F32)
        kprev_s[...] = jnp.zeros((PROMPT_SUBBLOCKS, CH, KV_W), BF16)
        vprev_s[...] = jnp.zeros((PROMPT_SUBBLOCKS, CH, KV_W), BF16)
        h_s[...] = jnp.zeros((PROMPT_SUBBLOCKS, N_HEADS * HEAD, D_STATE), F32)
        row2 = _iota((2 * CH, CH), 0) & (CH - 1)
        col2 = _iota((2 * CH, CH), 1)
        for kvh in range(2):
            slope, _ = _head_cols(sink_ref, layer, kvh, 2 * CH)
            bias_s[2 * kvh] = _attn_bias(row2 + CH - col2, col2 >= row2, slope)
            bias_s[2 * kvh + 1] = _attn_bias(row2 - col2, col2 <= row2, slope)

    @pl.when(j > 0)
    def _():
        glu_s[:, :, 0:b_hist, :] = glu_s[:, :, blk:blk + b_hist, :]
        xbc_s[:, :, 0:d_hist, :] = xbc_s[:, :, blk:blk + d_hist, :]

    def project(sb, after=None):
        _project(x_ref[sb], gpre[0], win, wdt, proj_s.at[sb * blk:(sb + 1) * blk], after)

    def conv_inputs(sb):
        lo, hi = sb * blk, (sb + 1) * blk
        for c in range(SEG // LANES):
            glu_s[sb, c, b_hist:b_hist + blk, :] = (
                proj_s[lo:hi, O_BV + c * LANES:O_BV + (c + 1) * LANES]
                * jax.nn.sigmoid(proj_s[lo:hi, O_BG + c * LANES:O_BG + (c + 1) * LANES]))
        for c in range(D_CONV_DIM // LANES):
            xbc_s[sb, c, d_hist:d_hist + blk, :] = proj_s[lo:hi, O_DX + c * LANES:O_DX + (c + 1) * LANES]

    def conv(buf, hist, n_tiles, w_ref, b_ref, n_taps, out_col, act, sb):
        tile = CONV_ROW_TILE
        for c in range(n_tiles):
            cs = slice(c * LANES, (c + 1) * LANES)
            for r0 in range(0, blk, tile):
                first = r0 + hist - (n_taps - 1)
                acc = buf[sb, c, pl.ds(first, tile, stride=1), :] * w_ref[0, 0:1, cs]
                for k in range(1, n_taps):
                    acc = acc + buf[sb, c, pl.ds(first + k, tile, stride=1), :] * w_ref[0, k:k + 1, cs]
                out_rows = slice(sb * blk + r0, sb * blk + r0 + tile)
                conv_s[out_rows, out_col + c * LANES:out_col + (c + 1) * LANES] = act(acc + b_ref[0, :, cs])

    def convs(sb):
        conv(glu_s, b_hist, SEG // LANES, bcw, bcb, B_TAPS, 0, lambda a: a, sb)
        conv(xbc_s, d_hist, D_CONV_DIM // LANES, dcw, dcb, D_TAPS, SEG, _silu, sb)

    same, causal = _chunk_masks(CH)
    same_b, tri_b = _sum_matrices(same, causal)
    ws_masked = _masked_ws(aws, causal, CH)
    kv_carry = [None, None]

    def mixers(c):
        rs = slice(c * CH, (c + 1) * CH)
        sb = c // cpb
        k_prev, v_prev = kv_carry

        ya, _ = _mixer_a(proj_s[rs, O_AU:O_AU + SEG], proj_s[rs, O_AV:O_AV + SEG], proj_s[rs, O_AZ:O_AZ + SEG],
                         alng[0], alnb[0], ws_masked, awb[0])
        mix_s[rs, 0:SEG] = ya.astype(BF16)

        yb = _mixer_b_post(conv_s[rs, 0:SEG], proj_s[rs, O_BZ:O_BZ + SEG], blng[0], blnb[0], bpw, bpb[0])
        mix_s[rs, SEG:2 * SEG] = yb.astype(BF16)

        q = _swap_inner_heads(proj_s[rs, O_CQ:O_CQ + SEG] * (HEAD ** -0.5))
        k_cur = proj_s[rs, O_CK:O_CK + KV_W].astype(BF16)
        v_cur = proj_s[rs, O_CV:O_CV + KV_W].astype(BF16)
        outs = []
        for kvh in range(2):
            _, sink = _head_cols(sink_ref, layer, kvh, 2 * CH)
            q_rows = _attn_q_rows(q, kvh)
            bias_prev = bias_s[2 * kvh]
            if c % cpb == 0:
                bias_prev = jnp.where(j > 0, bias_prev, -jnp.inf)
            outs.append(_attn([(_dot_nt(q_rows, k_prev), bias_prev, functools.partial(_dot, b=v_prev)),
                               (_dot_nt(q_rows, k_cur), bias_s[2 * kvh + 1], functools.partial(_dot, b=v_cur))],
                              sink))
        yc = _swap_inner_heads(_attn_merge(outs[0], outs[1])) * _silu(proj_s[rs, O_CZ:O_CZ + SEG])
        mix_s[rs, 2 * SEG:3 * SEG] = yc.astype(BF16)
        kv_carry[:] = [k_cur, v_cur]

        xs, bm, cm_b, dt, acum, alast, y = _ssd_intra(conv_s[rs, SEG:SEG + D_CONV_DIM],
                                                      proj_s[rs, O_DT:O_DT + DT_PAD], dtb[0], alog[0],
                                                      same_b, tri_b, causal)
        h_prev = h_s[sb]
        h_prev_b = h_prev.astype(BF16)
        y_inter = jnp.concatenate(
            [_dot_nt(cm_b[:, g * D_STATE:(g + 1) * D_STATE], h_prev_b[g * 2 * HEAD:(g + 1) * 2 * HEAD])
             for g in range(2)], axis=1)
        y = y + y_inter * _lane_bcast_heads(jnp.exp(acum))
        wcol = jnp.exp(alast - acum) * dt
        carry = jnp.exp(alast)
        xs_t = xs.T.astype(BF16)
        for h in range(N_HEADS):
            g = h // 2
            wb_h = (bm[:, g * D_STATE:(g + 1) * D_STATE] * wcol[:, h:h + 1]).astype(BF16)
            hs = slice(h * HEAD, (h + 1) * HEAD)
            h_s[sb, hs, :] = h_prev[hs] * carry[0:HEAD, h:h + 1] + _dot(xs_t[hs], wb_h)
        yd = _ssd_finish(y, xs, proj_s[rs, O_DZ:O_DZ + SEG], dskip[0], dng[0])
        mix_s[rs, 3 * SEG:4 * SEG] = yd.astype(BF16)
        return [_after(v) for v in (ya, yb, yc, yd)]

    def finish(sb):
        lo, hi = sb * blk, (sb + 1) * blk
        y_ref[sb] = x_ref[sb] + _rms(_dot(mix_s[lo:hi], wout[0]), gpost[0])

    assert cpb == 2
    project(0)
    for sb in range(PROMPT_SUBBLOCKS):
        kv_carry[:] = [kprev_s[sb], vprev_s[sb]]
        conv_inputs(sb)
        convs(sb)
        start = _after(conv_s[sb * blk:sb * blk + SUBLANES, 0:LANES])
        first = mixers(sb * cpb)
        second = mixers(sb * cpb + 1)
        kprev_s[sb] = kv_carry[0]
        vprev_s[sb] = kv_carry[1]
        if sb + 1 < PROMPT_SUBBLOCKS:
            project(sb + 1, [start, start, first[0], first[1], first[2], first[3], second[0], second[1]])
        finish(sb)

    @pl.when(j == pl.num_programs(0) - 1)
    def _():
        for sb in range(PROMPT_SUBBLOCKS):
            hi = (sb + 1) * blk
            kn_ref[sb] = proj_s[hi - CH:hi, O_CK:O_CK + KV_W]
            vn_ref[sb] = proj_s[hi - CH:hi, O_CV:O_CV + KV_W]
            for c in range(SEG // LANES):
                cbn_ref[sb, :, c * LANES:(c + 1) * LANES] = glu_s[sb, c, b_hist + blk - (B_TAPS - 1):b_hist + blk, :]
            for c in range(D_CONV_DIM // LANES):
                cdn_ref[sb, :, c * LANES:(c + 1) * LANES] = xbc_s[sb, c, d_hist + blk - (D_TAPS - 1):d_hist + blk, :]
            ssm_ref[sb] = h_s[sb].reshape(N_HEADS, HEAD, D_STATE)


def _prompt_layer(layer, x, p, rows):
    batch, seq, _ = x.shape
    assert batch == PROMPT_SUBBLOCKS
    blk_rows = rows // batch
    nb = seq // blk_rows

    def wspec(a):
        blk = (1,) + a.shape[1:]
        zeros = (0,) * (a.ndim - 1)
        return pl.BlockSpec(blk, lambda j: (layer,) + zeros)

    def whole(*dims):
        return pl.BlockSpec((batch,) + dims, lambda j: (0,) * (len(dims) + 1))

    vectors = [p[k] for k in VEC_NAMES]
    weights = [p[k] for k in ("win", "wdt", "wout", "aws", "awb_p", "bcw", "bpw", "dcw")]
    x_spec = pl.BlockSpec((batch, blk_rows, D_MODEL), lambda j: (0, j, 0))
    in_specs = ([pl.BlockSpec(memory_space=pltpu.SMEM), x_spec]
                + [pl.BlockSpec(a.shape, lambda j: (0, 0)) for a in vectors] + [wspec(a) for a in weights])
    out_shape = (jax.ShapeDtypeStruct((batch, seq, D_MODEL), F32),
                 jax.ShapeDtypeStruct((batch, CH, KV_W), F32),
                 jax.ShapeDtypeStruct((batch, CH, KV_W), F32),
                 jax.ShapeDtypeStruct((batch, B_TAPS - 1, SEG), F32),
                 jax.ShapeDtypeStruct((batch, D_TAPS - 1, D_CONV_DIM), F32),
                 jax.ShapeDtypeStruct((batch, N_HEADS, HEAD, D_STATE), F32))
    out_specs = (x_spec, whole(CH, KV_W), whole(CH, KV_W), whole(B_TAPS - 1, SEG),
                 whole(D_TAPS - 1, D_CONV_DIM), whole(N_HEADS, HEAD, D_STATE))
    scratch = [pltpu.VMEM((rows, IN_PAD), F32),
               pltpu.VMEM((batch, SEG // LANES, blk_rows + B_HIST, LANES), F32),
               pltpu.VMEM((batch, D_CONV_DIM // LANES, blk_rows + D_HIST, LANES), F32),
               pltpu.VMEM((rows, SEG + D_CONV_DIM), F32),
               pltpu.VMEM((batch, CH, KV_W), BF16),
               pltpu.VMEM((batch, CH, KV_W), BF16),
               pltpu.VMEM((batch, N_HEADS * HEAD, D_STATE), F32),
               pltpu.VMEM((rows, D_MODEL), BF16),
               pltpu.VMEM((4, 2 * CH, CH), F32)]
    return pl.pallas_call(
        functools.partial(_prompt_kernel, layer, rows),
        grid=(nb,),
        in_specs=in_specs, out_specs=out_specs, out_shape=out_shape, scratch_shapes=scratch,
        compiler_params=pltpu.CompilerParams(dimension_semantics=("arbitrary",),
                                             vmem_limit_bytes=VMEM_LIMIT_BYTES),
        name=f"prompt_layer{layer}",
    )(p["sinks"], x, *vectors, *weights)


def _sample_kernel(group,
                   sink_ref, x_ref, kc_ref, vc_ref, cbh_ref, cdh_ref, ssm0_ref, *refs):
    vec_refs, refs = refs[:len(VEC_NAMES)], refs[len(VEC_NAMES):]
    (win, wdt, wout, aws, awb, bcw, bpw, dcw,
     y_ref, kn_ref, vn_ref, cbn_ref, cdn_ref, ssm_ref, avn_ref,
     xin_s, proj_s, new_s, conv_s, mix_s, bias_a_s, bias_b_s) = refs
    layer = pl.program_id(0)
    gpre, gpost, alng, alnb, bcb, blng, blnb, bpb, dcb, dtb, alog, dskip, dng = _layer_rows(layer, vec_refs)
    s = pl.program_id(1)
    rows = group * SAMPLE_SEQ
    row0 = pl.multiple_of(s * rows, rows)

    @pl.when(layer == 0)
    def _():
        xin_s[...] = x_ref[...]

    @pl.when(layer > 0)
    def _():
        xin_s[...] = y_ref[pl.ds(row0, rows), :]

    x = xin_s[...]
    _project(x, gpre[0], win, wdt, proj_s)
    same, causal = _chunk_masks(SAMPLE_SEQ)
    same_b, tri_b = _sum_matrices(same, causal)

    ya, v_rows = _mixer_a(proj_s[:, O_AU:O_AU + SEG], proj_s[:, O_AV:O_AV + SEG], proj_s[:, O_AZ:O_AZ + SEG],
                          alng[0], alnb[0], _masked_ws(aws, causal, SAMPLE_SEQ), awb[0])
    mix_s[:, 0:SEG] = ya.astype(BF16)
    avn_ref[0] = v_rows

    glu = proj_s[:, O_BV:O_BV + SEG] * jax.nn.sigmoid(proj_s[:, O_BG:O_BG + SEG])
    cbn_ref[0] = glu
    cdn_ref[0] = proj_s[:, O_DX:O_DX + D_CONV_DIM]

    def conv_tiles(hist_ref, new, w_ref, b_ref, n_taps, tile0, act):
        n_hist = n_taps - 1
        n_tiles = new.shape[1] // LANES
        for c in range(n_tiles):
            cs = slice(c * LANES, (c + 1) * LANES)
            new_s[c] = new[:, cs]
            slabs = [hist_ref[0, t, :, cs] for t in range(n_hist)]
            slabs += [new_s[c, pl.ds(t, group, stride=SAMPLE_SEQ), :] for t in range(SAMPLE_SEQ)]
            for t in range(SAMPLE_SEQ):
                acc = slabs[t] * w_ref[0, 0:1, cs]
                for k in range(1, n_taps):
                    acc = acc + slabs[t + k] * w_ref[0, k:k + 1, cs]
                conv_s[tile0 + c, pl.ds(t, group, stride=SAMPLE_SEQ), :] = act(acc + b_ref[0, :, cs])

    conv_tiles(cbh_ref, glu, bcw, bcb, B_TAPS, 0, lambda a: a)
    conv_tiles(cdh_ref, proj_s[:, O_DX:O_DX + D_CONV_DIM], dcw, dcb, D_TAPS, SEG // LANES, _silu)
    conv_b = jnp.concatenate([conv_s[c] for c in range(SEG // LANES)], axis=1)
    conv_d = jnp.concatenate([conv_s[c] for c in range(SEG // LANES, (SEG + D_CONV_DIM) // LANES)], axis=1)

    yb = _mixer_b_post(conv_b, proj_s[:, O_BZ:O_BZ + SEG], blng[0], blnb[0], bpw, bpb[0])
    mix_s[:, SEG:2 * SEG] = yb.astype(BF16)

    kn_ref[0] = proj_s[:, O_CK:O_CK + KV_W]
    vn_ref[0] = proj_s[:, O_CV:O_CV + KV_W]
    sub_rows = ATT_SUB * SAMPLE_SEQ
    n_past = ATT_SUB * CH

    @pl.when((layer == 0) & (s == 0))
    def _():
        row = _iota((2 * sub_rows, n_past), 0) & (sub_rows - 1)
        col = _iota((2 * sub_rows, n_past), 1)
        q_t = row & (SAMPLE_SEQ - 1)
        key_pos = col & (CH - 1)
        rel_a = q_t + CH - key_pos
        ok_a = ((col >> CH_SHIFT) == (row >> SEQ_SHIFT)) & (key_pos >= q_t)
        row_n = _iota((2 * sub_rows, sub_rows), 0) & (sub_rows - 1)
        col_n = _iota((2 * sub_rows, sub_rows), 1)
        rel_b = (row_n & (SAMPLE_SEQ - 1)) - (col_n & (SAMPLE_SEQ - 1))
        ok_b = ((col_n >> SEQ_SHIFT) == (row_n >> SEQ_SHIFT)) & (rel_b >= 0)
        for kvh in range(2):
            slope, _ = _head_cols(sink_ref, layer, kvh, 2 * sub_rows)
            bias_a_s[kvh] = _attn_bias(rel_a, ok_a, slope)
            bias_b_s[kvh] = _attn_bias(rel_b, ok_b, slope)

    for u in range(group // ATT_SUB):
        us = slice(u * sub_rows, (u + 1) * sub_rows)
        q = proj_s[us, O_CQ:O_CQ + SEG] * (HEAD ** -0.5)
        outs = []
        for kvh in range(2):
            hs = [slice((2 * kvh + i) * HEAD, (2 * kvh + i + 1) * HEAD) for i in range(2)]
            q_rows = jnp.concatenate([q[:, hs[0]], q[:, hs[1]]], axis=0).astype(BF16)
            k_t = jnp.concatenate([kc_ref[0, u * ATT_SUB + i, kvh] for i in range(ATT_SUB)], axis=1).astype(BF16)
            v_t = jnp.concatenate([vc_ref[0, u * ATT_SUB + i, kvh] for i in range(ATT_SUB)], axis=1).astype(BF16)
            k_new = proj_s[us, O_CK + kvh * HEAD:O_CK + (kvh + 1) * HEAD].astype(BF16)
            v_new = proj_s[us, O_CV + kvh * HEAD:O_CV + (kvh + 1) * HEAD].astype(BF16)
            _, sink = _head_cols(sink_ref, layer, kvh, 2 * sub_rows)
            o = _attn([(_dot(q_rows, k_t), bias_a_s[kvh], functools.partial(_dot_nt, b=v_t)),
                       (_dot_nt(q_rows, k_new), bias_b_s[kvh], functools.partial(_dot, b=v_new))], sink)
            outs += [o[:sub_rows], o[sub_rows:]]
        yc = jnp.concatenate(outs, axis=1)
        mix_s[us, 2 * SEG:3 * SEG] = (yc * _silu(proj_s[us, O_CZ:O_CZ + SEG])).astype(BF16)

    xs, bm, cm_b, dt, acum, alast, y = _ssd_intra(conv_d, proj_s[:, O_DT:O_DT + DT_PAD], dtb[0], alog[0],
                                                  same_b, tri_b, causal)
    state_rows = group * HEAD
    own = (_iota((state_rows, CH), 0) >> HEAD_SHIFT) == (_iota((state_rows, CH), 1) >> SEQ_SHIFT)
    wcol = jnp.exp(alast - acum) * dt
    carry = jnp.exp(alast)
    xs_t = xs.T
    y_inter_t = []
    for h in range(N_HEADS):
        g = h // 2
        gs = slice(g * D_STATE, (g + 1) * D_STATE)
        h0 = ssm0_ref[0, :, h].reshape(state_rows, D_STATE)
        full = jnp.where(own, _dot_nt(h0.astype(BF16), cm_b[:, gs]), 0.0)
        y_inter_t.append(jnp.sum(full.reshape(group, HEAD, CH), axis=0))
        xt_h = xs_t[h * HEAD:(h + 1) * HEAD]
        lhs = jnp.where(own, jnp.concatenate([xt_h] * group, axis=0), 0.0).astype(BF16)
        wb_h = (bm[:, gs] * wcol[:, h:h + 1]).astype(BF16)
        carry_h = jnp.broadcast_to(carry[:, h:h + 1], (CH, D_STATE)).reshape(group, SAMPLE_SEQ, D_STATE)
        carry_h = jnp.concatenate([carry_h] * (HEAD // SAMPLE_SEQ), axis=1).reshape(state_rows, D_STATE)
        ssm_ref[0, :, h] = (h0 * carry_h + _dot(lhs, wb_h)).reshape(group, HEAD, D_STATE)
    y_inter = jnp.concatenate(y_inter_t, axis=0).T
    y = y + y_inter * _lane_bcast_heads(jnp.exp(acum))
    mix_s[:, 3 * SEG:4 * SEG] = _ssd_finish(y, xs, proj_s[:, O_DZ:O_DZ + SEG], dskip[0], dng[0]).astype(BF16)

    y_ref[pl.ds(row0, rows), :] = _finish_layer(x, mix_s, wout, gpost[0])


def _sample_layers(x, kc, vc, cbh, cdh, ssm0, p, group):
    depth = p["win"].shape[0]
    n_seq = ssm0.shape[1]
    n_rows = n_seq * SAMPLE_SEQ
    rows = group * SAMPLE_SEQ
    ns = n_seq // group

    def wspec(a):
        blk = (1,) + a.shape[1:]
        zeros = (0,) * (a.ndim - 1)
        return pl.BlockSpec(blk, lambda i, s: (i,) + zeros)

    def rspec(width, per_seq):
        return pl.BlockSpec((1, group * per_seq, width), lambda i, s: (i, s, 0))

    vectors = [p[k] for k in VEC_NAMES]
    weights = [p[k] for k in ("win", "wdt", "wout", "aws", "awb_s", "bcw", "bpw", "dcw")]
    ssm_spec = pl.BlockSpec((1, group, N_HEADS, HEAD, D_STATE), lambda i, s: (i, s, 0, 0, 0))
    cache_spec = pl.BlockSpec((1, group, 2, HEAD, CH), lambda i, s: (i, s, 0, 0, 0))

    def hist_spec(n_hist, width):
        return pl.BlockSpec((1, n_hist, group, width), lambda i, s: (i, 0, s, 0))

    in_specs = ([pl.BlockSpec(memory_space=pltpu.SMEM),
                 pl.BlockSpec((rows, D_MODEL), lambda i, s: (s, 0)),
                 cache_spec, cache_spec,
                 hist_spec(B_TAPS - 1, SEG), hist_spec(D_TAPS - 1, D_CONV_DIM), ssm_spec]
                + [pl.BlockSpec(a.shape, lambda i, s: (0, 0)) for a in vectors] + [wspec(a) for a in weights])
    out_shape = (jax.ShapeDtypeStruct((n_rows, D_MODEL), F32),
                 jax.ShapeDtypeStruct((depth, n_rows, KV_W), F32),
                 jax.ShapeDtypeStruct((depth, n_rows, KV_W), F32),
                 jax.ShapeDtypeStruct((depth, n_rows, SEG), F32),
                 jax.ShapeDtypeStruct((depth, n_rows, D_CONV_DIM), F32),
                 jax.ShapeDtypeStruct((depth, n_seq, N_HEADS, HEAD, D_STATE), F32),
                 jax.ShapeDtypeStruct((depth, n_rows, SEG), F32))
    out_specs = (pl.BlockSpec((n_rows, D_MODEL), lambda i, s: (0, 0)),
                 rspec(KV_W, SAMPLE_SEQ), rspec(KV_W, SAMPLE_SEQ),
                 rspec(SEG, SAMPLE_SEQ), rspec(D_CONV_DIM, SAMPLE_SEQ), ssm_spec,
                 rspec(SEG, SAMPLE_SEQ))
    scratch = [pltpu.VMEM((rows, D_MODEL), F32),
               pltpu.VMEM((rows, IN_PAD), F32),
               pltpu.VMEM((D_CONV_DIM // LANES, rows, LANES), F32),
               pltpu.VMEM(((SEG + D_CONV_DIM) // LANES, rows, LANES), F32),
               pltpu.VMEM((rows, D_MODEL), BF16),
               pltpu.VMEM((2, 2 * ATT_SUB * SAMPLE_SEQ, ATT_SUB * CH), F32),
               pltpu.VMEM((2, 2 * ATT_SUB * SAMPLE_SEQ, ATT_SUB * SAMPLE_SEQ), F32)]
    return pl.pallas_call(
        functools.partial(_sample_kernel, group),
        grid=(depth, ns),
        in_specs=in_specs, out_specs=out_specs, out_shape=out_shape, scratch_shapes=scratch,
        compiler_params=pltpu.CompilerParams(dimension_semantics=("arbitrary", "arbitrary"),
                                             vmem_limit_bytes=VMEM_LIMIT_BYTES),
        name="sample_layers",
    )(p["sinks"], x, kc, vc, cbh, cdh, ssm0, *vectors, *weights)


def _prepare_params(norm_pre, norm_post, w_in, w_out, a_ln_g, a_ln_b, a_ws, a_wb, b_conv_w, b_conv_b,
                    b_ln_g, b_ln_b, b_pw_w, b_pw_b, c_sinks, d_conv_w, d_conv_b, d_dt_bias, d_a_log,
                    d_skip, d_norm_g, group):
    depth = w_in.shape[0]
    win = w_in.astype(BF16)
    wdt = jnp.pad(w_in[:, :, O_DT:], ((0, 0), (0, 0), (0, DT_PAD - N_HEADS))).astype(BF16)
    wout = w_out.astype(BF16)

    def per_head(a):
        return jnp.repeat(a, HEAD, axis=-1)

    def lane_pad(a):
        return jnp.pad(a, ((0, 0), (0, DT_PAD - N_HEADS)))

    reps = CH // SAMPLE_SEQ
    return {
        "gpre": norm_pre, "gpost": norm_post, "alng": a_ln_g, "alnb": a_ln_b, "bcb": b_conv_b,
        "blng": b_ln_g, "blnb": b_ln_b, "bpb": b_pw_b, "dcb": d_conv_b, "dtb": lane_pad(d_dt_bias),
        "alog": lane_pad(d_a_log), "dskip": per_head(d_skip), "dng": d_norm_g,
        "sinks": c_sinks, "win": win, "wdt": wdt, "wout": wout,
        "aws": a_ws,
        "awb_p": per_head(jnp.swapaxes(a_wb, 1, 2)),
        "awb_s": per_head(jnp.tile(jnp.swapaxes(a_wb[:, :, :SAMPLE_SEQ], 1, 2), (1, reps, 1))),
        "bcw": b_conv_w, "bpw": b_pw_w.astype(BF16), "dcw": d_conv_w,
    }


PROMPT_ROWS = 512
PROMPT_SUBBLOCKS = 2
SAMPLE_GROUP = CH // SAMPLE_SEQ


def kernel(x_prompt, x_sample, cache_win_k, cache_win_v, state_conv_b, state_conv_d, state_ssm, norm_pre, norm_post, w_in, w_out, a_ln_g, a_ln_b, a_ws, a_wb, b_conv_w, b_conv_b, b_ln_g, b_ln_b, b_pw_w, b_pw_b, c_sinks, d_conv_w, d_conv_b, d_dt_bias, d_a_log, d_skip, d_norm_g):
    depth = w_in.shape[0]
    batch, seq, _ = x_prompt.shape
    n_seq, dec_seq, _ = x_sample.shape
    assert dec_seq == SAMPLE_SEQ and cache_win_k.shape[2] == CH and seq % PROMPT_ROWS == 0
    assert n_seq % SAMPLE_GROUP == 0
    p = _prepare_params(norm_pre, norm_post, w_in, w_out, a_ln_g, a_ln_b, a_ws, a_wb, b_conv_w, b_conv_b,
                        b_ln_g, b_ln_b, b_pw_w, b_pw_b, c_sinks, d_conv_w, d_conv_b, d_dt_bias, d_a_log,
                        d_skip, d_norm_g, SAMPLE_GROUP)

    xp = x_prompt
    per_layer = []
    for i in range(depth):
        xp, *state = _prompt_layer(i, xp, p, PROMPT_ROWS)
        per_layer.append(state)
    pk, pv, pcb, pcd, pssm = (jnp.stack(list(a)) for a in zip(*per_layer))

    n_rows = n_seq * SAMPLE_SEQ
    ys, sk, sv, scb, scd, sssm, sav = _sample_layers(
        x_sample.reshape(n_rows, D_MODEL),
        jnp.transpose(cache_win_k, (0, 1, 3, 4, 2)), jnp.transpose(cache_win_v, (0, 1, 3, 4, 2)),
        jnp.swapaxes(state_conv_b, 1, 2), jnp.swapaxes(state_conv_d, 1, 2),
        state_ssm, p, SAMPLE_GROUP)

    kv_p = (depth, batch, CH, 2, HEAD)
    kv_s = (depth, n_seq, SAMPLE_SEQ, 2, HEAD)
    return (xp, ys.reshape(n_seq, SAMPLE_SEQ, D_MODEL),
            pk.reshape(kv_p), sk.reshape(kv_s), pv.reshape(kv_p), sv.reshape(kv_s),
            pcb, scb.reshape(depth, n_seq, SAMPLE_SEQ, SEG),
            pcd, scd.reshape(depth, n_seq, SAMPLE_SEQ, D_CONV_DIM),
            pssm, sssm, sav.reshape(depth, n_seq, SAMPLE_SEQ, SEG))
```

```python
import functools

import jax
import jax.numpy as jnp
from jax import lax
from jax.experimental import pallas as pl
from jax.experimental.pallas import tpu as pltpu

F32 = jnp.float32
BF16 = jnp.bfloat16
EPS = 1e-6

D_MODEL = 1024
LANES = 128
CH = 128
HEAD = 64
N_HEADS = 4
SEG = 256
KV_W = 128
B_TAPS = 31
D_TAPS = 4
D_CONV_DIM = 768
D_STATE = 128
DT_PAD = 128
SAMPLE_SEQ = 8
ATT_SUB = 8
HEAD_SHIFT = HEAD.bit_length() - 1
CH_SHIFT = CH.bit_length() - 1
SEQ_SHIFT = SAMPLE_SEQ.bit_length() - 1
SUBLANES = 8
B_HIST = -(-(B_TAPS - 1) // SUBLANES) * SUBLANES
D_HIST = -(-(D_TAPS - 1) // SUBLANES) * SUBLANES
CONV_ROW_TILE = 32

O_AU, O_AV, O_AZ = 0, 256, 512
O_BV, O_BG, O_BZ = 768, 1024, 1280
O_CQ, O_CK, O_CV, O_CZ = 1536, 1792, 1920, 2048
O_DZ, O_DX, O_DT = 2304, 2560, 3328
IN_RAW = 3332
IN_PAD = 3456
PROJ_TILE = 1152
HELD_TILE = 512

ALIBI = tuple(float(2.0 ** (-8.0 * (i + 1) / N_HEADS)) for i in range(N_HEADS))
VMEM_LIMIT_BYTES = 56 * 1024 * 1024


VEC_NAMES = ("gpre", "gpost", "alng", "alnb", "bcb", "blng", "blnb", "bpb", "dcb", "dtb", "alog", "dskip", "dng")


def _layer_rows(layer, refs):
    return [r[pl.ds(layer, 1), :][None] for r in refs]


def _iota(shape, dim):
    return lax.broadcasted_iota(jnp.int32, shape, dim)


def _dot(a, b):
    return jnp.dot(a, b, preferred_element_type=F32)


def _dot_nt(a, b):
    return lax.dot_general(a, b, (((1,), (1,)), ((), ())), preferred_element_type=F32)


def _rms(x, g):
    return x * lax.rsqrt(jnp.mean(x * x, axis=-1, keepdims=True) + EPS) * g


def _ln(x, g, b):
    mu = jnp.mean(x, axis=-1, keepdims=True)
    xc = x - mu
    return xc * lax.rsqrt(jnp.mean(xc * xc, axis=-1, keepdims=True) + EPS) * g + b


def _silu(x):
    return x * jax.nn.sigmoid(x)


def _softplus(x):
    return jnp.maximum(x, 0.0) + jnp.log1p(jnp.exp(-jnp.abs(x)))


def _hi_lo(x):
    hi = x.astype(BF16)
    return hi, (x - hi.astype(F32)).astype(BF16)


def _head_of_lane(width):
    return _iota((CH, width), 1) >> HEAD_SHIFT


def _lane_bcast_heads(cols, width=SEG):
    head = _head_of_lane(width)
    out = jnp.zeros((CH, width), F32)
    for h in range(N_HEADS):
        out = jnp.where(head == h, cols[:, h:h + 1], out)
    return out


def _chunk_masks(seq_rows):
    shift = seq_rows.bit_length() - 1
    r = _iota((CH, CH), 0)
    c = _iota((CH, CH), 1)
    same = (r >> shift) == (c >> shift)
    causal = same & (c <= r)
    return same, causal


def _after(v):
    return jnp.minimum(jnp.abs(v[0:1, 0:1]), 0.0).astype(BF16)


def _project(x, gpre, win_ref, wdt_ref, proj, after=None):
    h = _rms(x, gpre).astype(BF16)
    tile = PROJ_TILE if after is None else HELD_TILE
    bounds = [(n0, min(n0 + tile, O_DT)) for n0 in range(0, O_DT, tile)]
    if after is None:
        after = [None] * (len(bounds) + 1)
    assert len(after) == len(bounds) + 1
    for (n0, n1), zero in zip(bounds, after):
        proj[:, n0:n1] = _dot(h if zero is None else h + zero, win_ref[0, :, n0:n1])
    proj[:, O_DT:O_DT + DT_PAD] = _dot(h if after[-1] is None else h + after[-1], wdt_ref[0])


def _masked_ws(ws_ref, causal, seq_rows):
    if seq_rows < CH:
        r = _iota((CH, CH), 0)
        c = _iota((CH, CH), 1)
        lane_tile = jnp.where((r < seq_rows) & ((c & (seq_rows - 1)) == r), 1.0, 0.0).astype(BF16)
    out = []
    for h in range(N_HEADS):
        if seq_rows == CH:
            w = ws_ref[0, h]
        else:
            top = _dot(ws_ref[0, h, 0:seq_rows, :].astype(BF16), lane_tile)
            w = jnp.concatenate([top] * (CH // seq_rows), axis=0)
        out.append(jnp.where(causal, w, 0.0).astype(BF16))
    return out


def _mixer_a(u_raw, v_raw, z, ln_g, ln_b, ws_masked, wb):
    u = jax.nn.gelu(u_raw)
    v = _ln(jax.nn.gelu(v_raw), ln_g, ln_b)
    vb = v.astype(BF16)
    head = _head_of_lane(SEG)
    s = jnp.zeros((CH, SEG), F32)
    for h in range(N_HEADS):
        s = jnp.where(head == h, _dot(ws_masked[h], vb), s)
    return u * (s + wb) * _silu(z), v


def _mixer_b_post(c, z, ln_g, ln_b, pw_ref, pw_b):
    y = _dot(_silu(_ln(c, ln_g, ln_b)).astype(BF16), pw_ref[0]) + pw_b
    return y * _silu(z)


def _attn_bias(rel, ok, slope):
    return jnp.where(ok, -slope * rel.astype(F32), -jnp.inf)


def _attn(parts, sink):
    scores = [qk + bias for qk, bias, _ in parts]
    same_shape = all(s.shape == scores[0].shape for s in scores)

    def row_reduce(xs, combine, reduce):
        if same_shape:
            xs = [functools.reduce(combine, xs)]
        return functools.reduce(combine, [reduce(x, axis=-1, keepdims=True) for x in xs])

    m = jnp.maximum(sink, row_reduce(scores, jnp.maximum, jnp.max))
    probs = [jnp.exp(s - m) for s in scores]
    den = jnp.exp(sink - m) + row_reduce(probs, jnp.add, jnp.sum)
    out = functools.reduce(jnp.add, [pv(p.astype(BF16)) for p, (_, _, pv) in zip(probs, parts)])
    return out / den


def _swap_inner_heads(x):
    t0, t1 = x[:, 0:KV_W], x[:, KV_W:2 * KV_W]
    lo = _iota(t0.shape, 1) < HEAD
    return jnp.concatenate([jnp.where(lo, t0, pltpu.roll(t1, HEAD, 1)),
                            jnp.where(lo, pltpu.roll(t0, HEAD, 1), t1)], axis=1)


def _attn_q_rows(q, kvh):
    lane = _iota((q.shape[0], KV_W), 1)
    keep = (lane < HEAD) if kvh == 0 else (lane >= HEAD)
    t0 = jnp.where(keep, q[:, 0:KV_W], 0.0)
    t1 = jnp.where(keep, q[:, KV_W:2 * KV_W], 0.0)
    return jnp.concatenate([t0, t1], axis=0).astype(BF16)


def _attn_merge(o0, o1):
    half = o0.shape[0] // 2
    lo = _iota((half, KV_W), 1) < HEAD
    return jnp.concatenate([jnp.where(lo, o0[:half], o1[:half]),
                            jnp.where(lo, o0[half:], o1[half:])], axis=1)


def _head_cols(sink_ref, layer, kvh, rows):
    half = rows // 2
    first = _iota((rows, 1), 0) < half
    slope = jnp.where(first, ALIBI[2 * kvh], ALIBI[2 * kvh + 1])
    sink = jnp.where(first, sink_ref[layer, 2 * kvh], sink_ref[layer, 2 * kvh + 1])
    return slope, sink


def _sum_matrices(same, causal):
    return jnp.where(same, 1.0, 0.0).astype(BF16), jnp.where(causal, 1.0, 0.0).astype(BF16)


def _ssd_intra(xbc, dt_raw, dtb, alog, same_b, tri_b, causal):
    xs = xbc[:, 0:SEG]
    bm_b = xbc[:, SEG:2 * SEG].astype(BF16)
    cm_b = xbc[:, 2 * SEG:3 * SEG].astype(BF16)
    xs_b = xs.astype(BF16)
    dt = _softplus(dt_raw + dtb)
    da_hi, da_lo = _hi_lo(dt * (-jnp.exp(alog)))
    acum = _dot(tri_b, da_hi) + _dot(tri_b, da_lo)
    alast = _dot(same_b, da_hi) + _dot(same_b, da_lo)
    acum_t = acum.T
    dt_t = dt.T
    head = _head_of_lane(SEG)
    y = jnp.zeros((CH, SEG), F32)
    for g in range(2):
        sc = _dot_nt(cm_b[:, g * D_STATE:(g + 1) * D_STATE], bm_b[:, g * D_STATE:(g + 1) * D_STATE])
        for hh in range(2):
            h = 2 * g + hh
            seg = acum[:, h:h + 1] - acum_t[h:h + 1, :]
            decay = jnp.exp(jnp.where(causal, seg, -jnp.inf))
            m = (sc * decay * dt_t[h:h + 1, :]).astype(BF16)
            y = jnp.where(head == h, _dot(m, xs_b), y)
    return xs, xbc[:, SEG:2 * SEG], cm_b, dt, acum, alast, y


def _ssd_finish(y, xs, z, dskip, dng):
    y = (y + dskip * xs) * _silu(z)
    parts = []
    for g in range(2):
        yg = y[:, g * D_STATE:(g + 1) * D_STATE]
        parts.append(_rms(yg, dng[:, g * D_STATE:(g + 1) * D_STATE]))
    return jnp.concatenate(parts, axis=1)


def _finish_layer(x, mix_s, wout_ref, gpost):
    return x + _rms(_dot(mix_s[...], wout_ref[0]), gpost)


def _prompt_kernel(layer, rows,
                   sink_ref, x_ref, *refs):
    vec_refs, refs = refs[:len(VEC_NAMES)], refs[len(VEC_NAMES):]
    (win, wdt, wout, aws, awb, bcw, bpw, dcw,
     y_ref, kn_ref, vn_ref, cbn_ref, cdn_ref, ssm_ref,
     proj_s, glu_s, xbc_s, conv_s, kprev_s, vprev_s, h_s, mix_s, bias_s) = refs
    gpre, gpost, alng, alnb, bcb, blng, blnb, bpb, dcb, dtb, alog, dskip, dng = _layer_rows(layer, vec_refs)
    j = pl.program_id(1)
    n_chunks = rows // CH
    b_hist, d_hist = B_HIST, D_HIST

    @pl.when(j == 0)
    def _():
        glu_s[:, 0:b_hist, :] = jnp.zeros((SEG // LANES, b_hist, LANES), F32)
        xbc_s[:, 0:d_hist, :] = jnp.zeros((D_CONV_DIM // LANES, d_hist, LANES), F32)
        kprev_s[...] = jnp.zeros((CH, KV_W), BF16)
        vprev_s[...] = jnp.zeros((CH, KV_W), BF16)
        h_s[...] = jnp.zeros((N_HEADS * HEAD, D_STATE), F32)
        row2 = _iota((2 * CH, CH), 0) & (CH - 1)
        col2 = _iota((2 * CH, CH), 1)
        for kvh in range(2):
            slope, _ = _head_cols(sink_ref, layer, kvh, 2 * CH)
            bias_s[2 * kvh] = _attn_bias(row2 + CH - col2, col2 >= row2, slope)
            bias_s[2 * kvh + 1] = _attn_bias(row2 - col2, col2 <= row2, slope)

    @pl.when(j > 0)
    def _():
        glu_s[:, 0:b_hist, :] = glu_s[:, rows:rows + b_hist, :]
        xbc_s[:, 0:d_hist, :] = xbc_s[:, rows:rows + d_hist, :]

    def project(lo, hi, after=None):
        _project(x_ref[0, lo:hi], gpre[0], win, wdt, proj_s.at[lo:hi], after)

    def conv_inputs(lo, hi):
        for c in range(SEG // LANES):
            glu_s[c, b_hist + lo:b_hist + hi, :] = (
                proj_s[lo:hi, O_BV + c * LANES:O_BV + (c + 1) * LANES]
                * jax.nn.sigmoid(proj_s[lo:hi, O_BG + c * LANES:O_BG + (c + 1) * LANES]))
        for c in range(D_CONV_DIM // LANES):
            xbc_s[c, d_hist + lo:d_hist + hi, :] = proj_s[lo:hi, O_DX + c * LANES:O_DX + (c + 1) * LANES]

    def conv(buf, hist, n_tiles, w_ref, b_ref, n_taps, out_col, act, lo, hi):
        tile = CONV_ROW_TILE
        for c in range(n_tiles):
            cs = slice(c * LANES, (c + 1) * LANES)
            for r0 in range(lo, hi, tile):
                first = r0 + hist - (n_taps - 1)
                acc = buf[c, pl.ds(first, tile, stride=1), :] * w_ref[0, 0:1, cs]
                for k in range(1, n_taps):
                    acc = acc + buf[c, pl.ds(first + k, tile, stride=1), :] * w_ref[0, k:k + 1, cs]
                conv_s[r0:r0 + tile, out_col + c * LANES:out_col + (c + 1) * LANES] = act(acc + b_ref[0, :, cs])

    def convs(lo, hi):
        conv(glu_s, b_hist, SEG // LANES, bcw, bcb, B_TAPS, 0, lambda a: a, lo, hi)
        conv(xbc_s, d_hist, D_CONV_DIM // LANES, dcw, dcb, D_TAPS, SEG, _silu, lo, hi)

    same, causal = _chunk_masks(CH)
    same_b, tri_b = _sum_matrices(same, causal)
    ws_masked = _masked_ws(aws, causal, CH)
    kv_carry = [kprev_s[...], vprev_s[...]]

    def mixers(c):
        rs = slice(c * CH, (c + 1) * CH)
        k_prev, v_prev = kv_carry

        ya, _ = _mixer_a(proj_s[rs, O_AU:O_AU + SEG], proj_s[rs, O_AV:O_AV + SEG], proj_s[rs, O_AZ:O_AZ + SEG],
                         alng[0], alnb[0], ws_masked, awb[0])
        mix_s[rs, 0:SEG] = ya.astype(BF16)

        yb = _mixer_b_post(conv_s[rs, 0:SEG], proj_s[rs, O_BZ:O_BZ + SEG], blng[0], blnb[0], bpw, bpb[0])
        mix_s[rs, SEG:2 * SEG] = yb.astype(BF16)

        q = _swap_inner_heads(proj_s[rs, O_CQ:O_CQ + SEG] * (HEAD ** -0.5))
        k_cur = proj_s[rs, O_CK:O_CK + KV_W].astype(BF16)
        v_cur = proj_s[rs, O_CV:O_CV + KV_W].astype(BF16)
        outs = []
        for kvh in range(2):
            _, sink = _head_cols(sink_ref, layer, kvh, 2 * CH)
            q_rows = _attn_q_rows(q, kvh)
            bias_prev = bias_s[2 * kvh]
            if c == 0:
                bias_prev = jnp.where(j > 0, bias_prev, -jnp.inf)
            outs.append(_attn([(_dot_nt(q_rows, k_prev), bias_prev, functools.partial(_dot, b=v_prev)),
                               (_dot_nt(q_rows, k_cur), bias_s[2 * kvh + 1], functools.partial(_dot, b=v_cur))],
                              sink))
        yc = _swap_inner_heads(_attn_merge(outs[0], outs[1])) * _silu(proj_s[rs, O_CZ:O_CZ + SEG])
        mix_s[rs, 2 * SEG:3 * SEG] = yc.astype(BF16)
        kv_carry[:] = [k_cur, v_cur]

        xs, bm, cm_b, dt, acum, alast, y = _ssd_intra(conv_s[rs, SEG:SEG + D_CONV_DIM],
                                                      proj_s[rs, O_DT:O_DT + DT_PAD], dtb[0], alog[0],
                                                      same_b, tri_b, causal)
        h_prev = h_s[...]
        h_prev_b = h_prev.astype(BF16)
        y_inter = jnp.concatenate(
            [_dot_nt(cm_b[:, g * D_STATE:(g + 1) * D_STATE], h_prev_b[g * 2 * HEAD:(g + 1) * 2 * HEAD])
             for g in range(2)], axis=1)
        y = y + y_inter * _lane_bcast_heads(jnp.exp(acum))
        wcol = jnp.exp(alast - acum) * dt
        carry = jnp.exp(alast)
        xs_t = xs.T.astype(BF16)
        for h in range(N_HEADS):
            g = h // 2
            wb_h = (bm[:, g * D_STATE:(g + 1) * D_STATE] * wcol[:, h:h + 1]).astype(BF16)
            hs = slice(h * HEAD, (h + 1) * HEAD)
            h_s[hs, :] = h_prev[hs] * carry[0:HEAD, h:h + 1] + _dot(xs_t[hs], wb_h)
        yd = _ssd_finish(y, xs, proj_s[rs, O_DZ:O_DZ + SEG], dskip[0], dng[0])
        mix_s[rs, 3 * SEG:4 * SEG] = yd.astype(BF16)
        return [_after(v) for v in (ya, yb, yc, yd)]

    def finish(lo, hi):
        y_ref[0, lo:hi] = x_ref[0, lo:hi] + _rms(_dot(mix_s[lo:hi], wout[0]), gpost[0])

    blk = rows // PROMPT_SUBBLOCKS
    assert blk == 2 * CH
    project(0, blk)
    for lo in range(0, rows, blk):
        hi = lo + blk
        conv_inputs(lo, hi)
        convs(lo, hi)
        start = _after(conv_s[lo:lo + SUBLANES, 0:LANES])
        first = mixers(lo // CH)
        second = mixers(lo // CH + 1)
        if hi < rows:
            project(hi, hi + blk, [start, start, first[0], first[1], first[2], first[3], second[0], second[1]])
        finish(lo, hi)
    kprev_s[...] = kv_carry[0]
    vprev_s[...] = kv_carry[1]

    @pl.when(j == pl.num_programs(1) - 1)
    def _():
        kn_ref[0] = proj_s[rows - CH:rows, O_CK:O_CK + KV_W]
        vn_ref[0] = proj_s[rows - CH:rows, O_CV:O_CV + KV_W]
        for c in range(SEG // LANES):
            cbn_ref[0, :, c * LANES:(c + 1) * LANES] = glu_s[c, b_hist + rows - (B_TAPS - 1):b_hist + rows, :]
        for c in range(D_CONV_DIM // LANES):
            cdn_ref[0, :, c * LANES:(c + 1) * LANES] = xbc_s[c, d_hist + rows - (D_TAPS - 1):d_hist + rows, :]
        ssm_ref[0] = h_s[...].reshape(N_HEADS, HEAD, D_STATE)


def _prompt_layer(layer, x, p, rows):
    batch, seq, _ = x.shape
    nb = seq // rows

    def wspec(a):
        blk = (1,) + a.shape[1:]
        zeros = (0,) * (a.ndim - 1)
        return pl.BlockSpec(blk, lambda b, j: (layer,) + zeros, pipeline_mode=pl.Buffered(1))

    vectors = [p[k] for k in VEC_NAMES]
    weights = [p[k] for k in ("win", "wdt", "wout", "aws", "awb_p", "bcw", "bpw", "dcw")]
    x_spec = pl.BlockSpec((1, rows, D_MODEL), lambda b, j: (b, j, 0))
    in_specs = ([pl.BlockSpec(memory_space=pltpu.SMEM), x_spec]
                + [pl.BlockSpec(a.shape, lambda b, j: (0, 0)) for a in vectors] + [wspec(a) for a in weights])
    out_shape = (jax.ShapeDtypeStruct((batch, seq, D_MODEL), F32),
                 jax.ShapeDtypeStruct((batch, CH, KV_W), F32),
                 jax.ShapeDtypeStruct((batch, CH, KV_W), F32),
                 jax.ShapeDtypeStruct((batch, B_TAPS - 1, SEG), F32),
                 jax.ShapeDtypeStruct((batch, D_TAPS - 1, D_CONV_DIM), F32),
                 jax.ShapeDtypeStruct((batch, N_HEADS, HEAD, D_STATE), F32))
    out_specs = (x_spec,
                 pl.BlockSpec((1, CH, KV_W), lambda b, j: (b, 0, 0)),
                 pl.BlockSpec((1, CH, KV_W), lambda b, j: (b, 0, 0)),
                 pl.BlockSpec((1, B_TAPS - 1, SEG), lambda b, j: (b, 0, 0)),
                 pl.BlockSpec((1, D_TAPS - 1, D_CONV_DIM), lambda b, j: (b, 0, 0)),
                 pl.BlockSpec((1, N_HEADS, HEAD, D_STATE), lambda b, j: (b, 0, 0, 0)))
    scratch = [pltpu.VMEM((rows, IN_PAD), F32),
               pltpu.VMEM((SEG // LANES, rows + B_HIST, LANES), F32),
               pltpu.VMEM((D_CONV_DIM // LANES, rows + D_HIST, LANES), F32),
               pltpu.VMEM((rows, SEG + D_CONV_DIM), F32),
               pltpu.VMEM((CH, KV_W), BF16),
               pltpu.VMEM((CH, KV_W), BF16),
               pltpu.VMEM((N_HEADS * HEAD, D_STATE), F32),
               pltpu.VMEM((rows, D_MODEL), BF16),
               pltpu.VMEM((4, 2 * CH, CH), F32)]
    return pl.pallas_call(
        functools.partial(_prompt_kernel, layer, rows),
        grid=(batch, nb),
        in_specs=in_specs, out_specs=out_specs, out_shape=out_shape, scratch_shapes=scratch,
        compiler_params=pltpu.CompilerParams(dimension_semantics=("arbitrary", "arbitrary"),
                                             vmem_limit_bytes=VMEM_LIMIT_BYTES),
        name=f"prompt_layer{layer}",
    )(p["sinks"], x, *vectors, *weights)


def _sample_kernel(group,
                   sink_ref, x_ref, kc_ref, vc_ref, cbh_ref, cdh_ref, ssm0_ref, *refs):
    vec_refs, refs = refs[:len(VEC_NAMES)], refs[len(VEC_NAMES):]
    (win, wdt, wout, aws, awb, bcw, bpw, dcw,
     y_ref, kn_ref, vn_ref, cbn_ref, cdn_ref, ssm_ref, avn_ref,
     xin_s, proj_s, new_s, conv_s, mix_s, bias_a_s, bias_b_s) = refs
    layer = pl.program_id(0)
    gpre, gpost, alng, alnb, bcb, blng, blnb, bpb, dcb, dtb, alog, dskip, dng = _layer_rows(layer, vec_refs)
    s = pl.program_id(1)
    rows = group * SAMPLE_SEQ
    row0 = pl.multiple_of(s * rows, rows)

    @pl.when(layer == 0)
    def _():
        xin_s[...] = x_ref[...]

    @pl.when(layer > 0)
    def _():
        xin_s[...] = y_ref[pl.ds(row0, rows), :]

    x = xin_s[...]
    _project(x, gpre[0], win, wdt, proj_s)
    same, causal = _chunk_masks(SAMPLE_SEQ)
    same_b, tri_b = _sum_matrices(same, causal)

    ya, v_rows = _mixer_a(proj_s[:, O_AU:O_AU + SEG], proj_s[:, O_AV:O_AV + SEG], proj_s[:, O_AZ:O_AZ + SEG],
                          alng[0], alnb[0], _masked_ws(aws, causal, SAMPLE_SEQ), awb[0])
    mix_s[:, 0:SEG] = ya.astype(BF16)
    avn_ref[0] = v_rows

    glu = proj_s[:, O_BV:O_BV + SEG] * jax.nn.sigmoid(proj_s[:, O_BG:O_BG + SEG])
    cbn_ref[0] = glu
    cdn_ref[0] = proj_s[:, O_DX:O_DX + D_CONV_DIM]

    def conv_tiles(hist_ref, new, w_ref, b_ref, n_taps, tile0, act):
        n_hist = n_taps - 1
        n_tiles = new.shape[1] // LANES
        for c in range(n_tiles):
            cs = slice(c * LANES, (c + 1) * LANES)
            new_s[c] = new[:, cs]
            slabs = [hist_ref[0, t, :, cs] for t in range(n_hist)]
            slabs += [new_s[c, pl.ds(t, group, stride=SAMPLE_SEQ), :] for t in range(SAMPLE_SEQ)]
            for t in range(SAMPLE_SEQ):
                acc = slabs[t] * w_ref[0, 0:1, cs]
                for k in range(1, n_taps):
                    acc = acc + slabs[t + k] * w_ref[0, k:k + 1, cs]
                conv_s[tile0 + c, pl.ds(t, group, stride=SAMPLE_SEQ), :] = act(acc + b_ref[0, :, cs])

    conv_tiles(cbh_ref, glu, bcw, bcb, B_TAPS, 0, lambda a: a)
    conv_tiles(cdh_ref, proj_s[:, O_DX:O_DX + D_CONV_DIM], dcw, dcb, D_TAPS, SEG // LANES, _silu)
    conv_b = jnp.concatenate([conv_s[c] for c in range(SEG // LANES)], axis=1)
    conv_d = jnp.concatenate([conv_s[c] for c in range(SEG // LANES, (SEG + D_CONV_DIM) // LANES)], axis=1)

    yb = _mixer_b_post(conv_b, proj_s[:, O_BZ:O_BZ + SEG], blng[0], blnb[0], bpw, bpb[0])
    mix_s[:, SEG:2 * SEG] = yb.astype(BF16)

    kn_ref[0] = proj_s[:, O_CK:O_CK + KV_W]
    vn_ref[0] = proj_s[:, O_CV:O_CV + KV_W]
    sub_rows = ATT_SUB * SAMPLE_SEQ
    n_past = ATT_SUB * CH

    @pl.when((layer == 0) & (s == 0))
    def _():
        row = _iota((2 * sub_rows, n_past), 0) & (sub_rows - 1)
        col = _iota((2 * sub_rows, n_past), 1)
        q_t = row & (SAMPLE_SEQ - 1)
        key_pos = col & (CH - 1)
        rel_a = q_t + CH - key_pos
        ok_a = ((col >> CH_SHIFT) == (row >> SEQ_SHIFT)) & (key_pos >= q_t)
        row_n = _iota((2 * sub_rows, sub_rows), 0) & (sub_rows - 1)
        col_n = _iota((2 * sub_rows, sub_rows), 1)
        rel_b = (row_n & (SAMPLE_SEQ - 1)) - (col_n & (SAMPLE_SEQ - 1))
        ok_b = ((col_n >> SEQ_SHIFT) == (row_n >> SEQ_SHIFT)) & (rel_b >= 0)
        for kvh in range(2):
            slope, _ = _head_cols(sink_ref, layer, kvh, 2 * sub_rows)
            bias_a_s[kvh] = _attn_bias(rel_a, ok_a, slope)
            bias_b_s[kvh] = _attn_bias(rel_b, ok_b, slope)

    for u in range(group // ATT_SUB):
        us = slice(u * sub_rows, (u + 1) * sub_rows)
        q = proj_s[us, O_CQ:O_CQ + SEG] * (HEAD ** -0.5)
        outs = []
        for kvh in range(2):
            hs = [slice((2 * kvh + i) * HEAD, (2 * kvh + i + 1) * HEAD) for i in range(2)]
            q_rows = jnp.concatenate([q[:, hs[0]], q[:, hs[1]]], axis=0).astype(BF16)
            k_t = jnp.concatenate([kc_ref[0, u * ATT_SUB + i, kvh] for i in range(ATT_SUB)], axis=1).astype(BF16)
            v_t = jnp.concatenate([vc_ref[0, u * ATT_SUB + i, kvh] for i in range(ATT_SUB)], axis=1).astype(BF16)
            k_new = proj_s[us, O_CK + kvh * HEAD:O_CK + (kvh + 1) * HEAD].astype(BF16)
            v_new = proj_s[us, O_CV + kvh * HEAD:O_CV + (kvh + 1) * HEAD].astype(BF16)
            _, sink = _head_cols(sink_ref, layer, kvh, 2 * sub_rows)
            o = _attn([(_dot(q_rows, k_t), bias_a_s[kvh], functools.partial(_dot_nt, b=v_t)),
                       (_dot_nt(q_rows, k_new), bias_b_s[kvh], functools.partial(_dot, b=v_new))], sink)
            outs += [o[:sub_rows], o[sub_rows:]]
        yc = jnp.concatenate(outs, axis=1)
        mix_s[us, 2 * SEG:3 * SEG] = (yc * _silu(proj_s[us, O_CZ:O_CZ + SEG])).astype(BF16)

    xs, bm, cm_b, dt, acum, alast, y = _ssd_intra(conv_d, proj_s[:, O_DT:O_DT + DT_PAD], dtb[0], alog[0],
                                                  same_b, tri_b, causal)
    state_rows = group * HEAD
    own = (_iota((state_rows, CH), 0) >> HEAD_SHIFT) == (_iota((state_rows, CH), 1) >> SEQ_SHIFT)
    wcol = jnp.exp(alast - acum) * dt
    carry = jnp.exp(alast)
    xs_t = xs.T
    y_inter_t = []
    for h in range(N_HEADS):
        g = h // 2
        gs = slice(g * D_STATE, (g + 1) * D_STATE)
        h0 = ssm0_ref[0, :, h].reshape(state_rows, D_STATE)
        full = jnp.where(own, _dot_nt(h0.astype(BF16), cm_b[:, gs]), 0.0)
        y_inter_t.append(jnp.sum(full.reshape(group, HEAD, CH), axis=0))
        xt_h = xs_t[h * HEAD:(h + 1) * HEAD]
        lhs = jnp.where(own, jnp.concatenate([xt_h] * group, axis=0), 0.0).astype(BF16)
        wb_h = (bm[:, gs] * wcol[:, h:h + 1]).astype(BF16)
        carry_h = jnp.broadcast_to(carry[:, h:h + 1], (CH, D_STATE)).reshape(group, SAMPLE_SEQ, D_STATE)
        carry_h = jnp.concatenate([carry_h] * (HEAD // SAMPLE_SEQ), axis=1).reshape(state_rows, D_STATE)
        ssm_ref[0, :, h] = (h0 * carry_h + _dot(lhs, wb_h)).reshape(group, HEAD, D_STATE)
    y_inter = jnp.concatenate(y_inter_t, axis=0).T
    y = y + y_inter * _lane_bcast_heads(jnp.exp(acum))
    mix_s[:, 3 * SEG:4 * SEG] = _ssd_finish(y, xs, proj_s[:, O_DZ:O_DZ + SEG], dskip[0], dng[0]).astype(BF16)

    y_ref[pl.ds(row0, rows), :] = _finish_layer(x, mix_s, wout, gpost[0])


def _sample_layers(x, kc, vc, cbh, cdh, ssm0, p, group):
    depth = p["win"].shape[0]
    n_seq = ssm0.shape[1]
    n_rows = n_seq * SAMPLE_SEQ
    rows = group * SAMPLE_SEQ
    ns = n_seq // group

    def wspec(a):
        blk = (1,) + a.shape[1:]
        zeros = (0,) * (a.ndim - 1)
        return pl.BlockSpec(blk, lambda i, s: (i,) + zeros)

    def rspec(width, per_seq):
        return pl.BlockSpec((1, group * per_seq, width), lambda i, s: (i, s, 0))

    vectors = [p[k] for k in VEC_NAMES]
    weights = [p[k] for k in ("win", "wdt", "wout", "aws", "awb_s", "bcw", "bpw", "dcw")]
    ssm_spec = pl.BlockSpec((1, group, N_HEADS, HEAD, D_STATE), lambda i, s: (i, s, 0, 0, 0))
    cache_spec = pl.BlockSpec((1, group, 2, HEAD, CH), lambda i, s: (i, s, 0, 0, 0))

    def hist_spec(n_hist, width):
        return pl.BlockSpec((1, n_hist, group, width), lambda i, s: (i, 0, s, 0))

    in_specs = ([pl.BlockSpec(memory_space=pltpu.SMEM),
                 pl.BlockSpec((rows, D_MODEL), lambda i, s: (s, 0)),
                 cache_spec, cache_spec,
                 hist_spec(B_TAPS - 1, SEG), hist_spec(D_TAPS - 1, D_CONV_DIM), ssm_spec]
                + [pl.BlockSpec(a.shape, lambda i, s: (0, 0)) for a in vectors] + [wspec(a) for a in weights])
    out_shape = (jax.ShapeDtypeStruct((n_rows, D_MODEL), F32),
                 jax.ShapeDtypeStruct((depth, n_rows, KV_W), F32),
                 jax.ShapeDtypeStruct((depth, n_rows, KV_W), F32),
                 jax.ShapeDtypeStruct((depth, n_rows, SEG), F32),
                 jax.ShapeDtypeStruct((depth, n_rows, D_CONV_DIM), F32),
                 jax.ShapeDtypeStruct((depth, n_seq, N_HEADS, HEAD, D_STATE), F32),
                 jax.ShapeDtypeStruct((depth, n_rows, SEG), F32))
    out_specs = (pl.BlockSpec((n_rows, D_MODEL), lambda i, s: (0, 0)),
                 rspec(KV_W, SAMPLE_SEQ), rspec(KV_W, SAMPLE_SEQ),
                 rspec(SEG, SAMPLE_SEQ), rspec(D_CONV_DIM, SAMPLE_SEQ), ssm_spec,
                 rspec(SEG, SAMPLE_SEQ))
    scratch = [pltpu.VMEM((rows, D_MODEL), F32),
               pltpu.VMEM((rows, IN_PAD), F32),
               pltpu.VMEM((D_CONV_DIM // LANES, rows, LANES), F32),
               pltpu.VMEM(((SEG + D_CONV_DIM) // LANES, rows, LANES), F32),
               pltpu.VMEM((rows, D_MODEL), BF16),
               pltpu.VMEM((2, 2 * ATT_SUB * SAMPLE_SEQ, ATT_SUB * CH), F32),
               pltpu.VMEM((2, 2 * ATT_SUB * SAMPLE_SEQ, ATT_SUB * SAMPLE_SEQ), F32)]
    return pl.pallas_call(
        functools.partial(_sample_kernel, group),
        grid=(depth, ns),
        in_specs=in_specs, out_specs=out_specs, out_shape=out_shape, scratch_shapes=scratch,
        compiler_params=pltpu.CompilerParams(dimension_semantics=("arbitrary", "arbitrary"),
                                             vmem_limit_bytes=VMEM_LIMIT_BYTES),
        name="sample_layers",
    )(p["sinks"], x, kc, vc, cbh, cdh, ssm0, *vectors, *weights)


def _prepare_params(norm_pre, norm_post, w_in, w_out, a_ln_g, a_ln_b, a_ws, a_wb, b_conv_w, b_conv_b,
                    b_ln_g, b_ln_b, b_pw_w, b_pw_b, c_sinks, d_conv_w, d_conv_b, d_dt_bias, d_a_log,
                    d_skip, d_norm_g, group):
    depth = w_in.shape[0]
    win = w_in.astype(BF16)
    wdt = jnp.pad(w_in[:, :, O_DT:], ((0, 0), (0, 0), (0, DT_PAD - N_HEADS))).astype(BF16)
    wout = w_out.astype(BF16)

    def per_head(a):
        return jnp.repeat(a, HEAD, axis=-1)

    def lane_pad(a):
        return jnp.pad(a, ((0, 0), (0, DT_PAD - N_HEADS)))

    reps = CH // SAMPLE_SEQ
    return {
        "gpre": norm_pre, "gpost": norm_post, "alng": a_ln_g, "alnb": a_ln_b, "bcb": b_conv_b,
        "blng": b_ln_g, "blnb": b_ln_b, "bpb": b_pw_b, "dcb": d_conv_b, "dtb": lane_pad(d_dt_bias),
        "alog": lane_pad(d_a_log), "dskip": per_head(d_skip), "dng": d_norm_g,
        "sinks": c_sinks, "win": win, "wdt": wdt, "wout": wout,
        "aws": a_ws,
        "awb_p": per_head(jnp.swapaxes(a_wb, 1, 2)),
        "awb_s": per_head(jnp.tile(jnp.swapaxes(a_wb[:, :, :SAMPLE_SEQ], 1, 2), (1, reps, 1))),
        "bcw": b_conv_w, "bpw": b_pw_w.astype(BF16), "dcw": d_conv_w,
    }


PROMPT_ROWS = 512
PROMPT_SUBBLOCKS = 2
SAMPLE_GROUP = CH // SAMPLE_SEQ


def kernel(x_prompt, x_sample, cache_win_k, cache_win_v, state_conv_b, state_conv_d, state_ssm, norm_pre, norm_post, w_in, w_out, a_ln_g, a_ln_b, a_ws, a_wb, b_conv_w, b_conv_b, b_ln_g, b_ln_b, b_pw_w, b_pw_b, c_sinks, d_conv_w, d_conv_b, d_dt_bias, d_a_log, d_skip, d_norm_g):
    depth = w_in.shape[0]
    batch, seq, _ = x_prompt.shape
    n_seq, dec_seq, _ = x_sample.shape
    assert dec_seq == SAMPLE_SEQ and cache_win_k.shape[2] == CH and seq % PROMPT_ROWS == 0
    assert n_seq % SAMPLE_GROUP == 0
    p = _prepare_params(norm_pre, norm_post, w_in, w_out, a_ln_g, a_ln_b, a_ws, a_wb, b_conv_w, b_conv_b,
                        b_ln_g, b_ln_b, b_pw_w, b_pw_b, c_sinks, d_conv_w, d_conv_b, d_dt_bias, d_a_log,
                        d_skip, d_norm_g, SAMPLE_GROUP)

    xp = x_prompt
    per_layer = []
    for i in range(depth):
        xp, *state = _prompt_layer(i, xp, p, PROMPT_ROWS)
        per_layer.append(state)
    pk, pv, pcb, pcd, pssm = (jnp.stack(list(a)) for a in zip(*per_layer))

    n_rows = n_seq * SAMPLE_SEQ
    ys, sk, sv, scb, scd, sssm, sav = _sample_layers(
        x_sample.reshape(n_rows, D_MODEL),
        jnp.transpose(cache_win_k, (0, 1, 3, 4, 2)), jnp.transpose(cache_win_v, (0, 1, 3, 4, 2)),
        jnp.swapaxes(state_conv_b, 1, 2), jnp.swapaxes(state_conv_d, 1, 2),
        state_ssm, p, SAMPLE_GROUP)

    kv_p = (depth, batch, CH, 2, HEAD)
    kv_s = (depth, n_seq, SAMPLE_SEQ, 2, HEAD)
    return (xp, ys.reshape(n_seq, SAMPLE_SEQ, D_MODEL),
            pk.reshape(kv_p), sk.reshape(kv_s), pv.reshape(kv_p), sv.reshape(kv_s),
            pcb, scb.reshape(depth, n_seq, SAMPLE_SEQ, SEG),
            pcd, scd.reshape(depth, n_seq, SAMPLE_SEQ, D_CONV_DIM),
            pssm, sssm, sav.reshape(depth, n_seq, SAMPLE_SEQ, SEG))
```
